```python
import math
import jax
import jax.numpy as jnp
from jax import lax
import numpy as np

D_MODEL = 1024
BATCH = 8
SEQ = 4096
DEPTH = 1

CHUNK = 64
PLE_DIM = 256
EPS = 1e-6

SSM_WIDTH = D_MODEL // 2
SSM_GROUP = 16
SSM_GROUPS = SSM_WIDTH // SSM_GROUP
SSM_STATE = 64
DT_MIN = 1e-3
DT_MAX = 1e-1

N_HEADS = 8
HEAD_DIM = 64
ATTN_WIDTH = N_HEADS * HEAD_DIM
IDX_HEADS = 8
IDX_DIM = 32
TOPK_MAX = 256
Q_BLOCK = 128
ROPE_THETA = 10000.0

IN_SIZES = (SSM_WIDTH, ATTN_WIDTH, ATTN_WIDTH, ATTN_WIDTH, IDX_HEADS * IDX_DIM, IDX_DIM, IDX_HEADS, D_MODEL, D_MODEL)
IN_COLS = sum(IN_SIZES)

N_EXPERTS = 32
TOP_K = 4
D_FF = D_MODEL
SWIGLU_LIMIT = 7.0
SWIGLU_ALPHA = 1.702

kernel_name = 'hybrid_s5_dsa_moe_ple_block'


def rmsnorm(x, g):
    xf = x.astype(jnp.float32)
    y = xf * lax.rsqrt(jnp.mean(xf * xf, axis=-1, keepdims=True) + EPS)
    return (y * g.astype(jnp.float32)).astype(x.dtype)


def rope(x, pos):
    d = x.shape[-1]
    half = d // 2
    inv = ROPE_THETA ** (-jnp.arange(half, dtype=jnp.float32) * 2.0 / d)
    ang = pos.astype(jnp.float32)[:, None] * inv[None, :]
    cos = jnp.cos(ang)[:, None, :]
    sin = jnp.sin(ang)[:, None, :]
    x1 = x[..., :half].astype(jnp.float32)
    x2 = x[..., half:].astype(jnp.float32)
    out = jnp.concatenate([x1 * cos - x2 * sin, x2 * cos + x1 * sin], axis=-1)
    return out.astype(x.dtype)


def split_in(z):
    pts = np.cumsum(IN_SIZES)[:-1].tolist()
    return jnp.split(z, pts, axis=-1)


def s5_scan(u, lam_re, lam_im, log_dt, b_re, b_im, c_re, c_im, d_skip):
    f32 = jnp.float32
    lam = lax.complex(lam_re.astype(f32), lam_im.astype(f32))
    dt = jnp.exp(log_dt.astype(f32))[:, None]
    lam_bar = jnp.exp(lam * dt)
    b = lax.complex(b_re.astype(f32), b_im.astype(f32))
    b_bar = ((lam_bar - 1.0) / lam)[..., None] * b
    c = lax.complex(c_re.astype(f32), c_im.astype(f32))

    def combine(e1, e2):
        a1, s1 = e1
        a2, s2 = e2
        return a1 * a2, a2 * s1 + s2

    def per_seq(u_seq):
        bu = jnp.einsum('gph,lgh->lgp', b_bar, u_seq.astype(f32))
        a = jnp.broadcast_to(lam_bar, bu.shape)
        _, states = lax.associative_scan(combine, (a, bu), axis=0)
        return jnp.einsum('ghp,lgp->lgh', c, states).real

    y = lax.map(per_seq, u)
    y = y + d_skip.astype(f32) * u.astype(f32)
    return y.astype(u.dtype)


def dsa_attention(q, k, v, q_idx, k_idx, w_idx):
    f32 = jnp.float32
    bsz, seq = q.shape[0], q.shape[1]
    topk = min(TOPK_MAX, seq // 4)
    key_chunk = jnp.arange(seq) // CHUNK
    scale = HEAD_DIM ** -0.5
    idx_scale = (IDX_HEADS * IDX_DIM) ** -0.5
    kf = k_idx.astype(f32)
    gather = jax.vmap(lambda t, i: t[i])

    def block(start):
        q_chunk = (start + jnp.arange(Q_BLOCK)) // CHUNK
        qb = lax.dynamic_slice_in_dim(q, start, Q_BLOCK, axis=1)
        qib = lax.dynamic_slice_in_dim(q_idx, start, Q_BLOCK, axis=1).astype(f32)
        wb = lax.dynamic_slice_in_dim(w_idx, start, Q_BLOCK, axis=1).astype(f32) * idx_scale
        dots = jnp.einsum('bthd,bsd->bths', qib, kf)
        score = jnp.einsum('bth,bths->bts', wb, jax.nn.relu(dots))
        allowed = key_chunk[None, :] <= q_chunk[:, None]
        score = jnp.where(allowed[None], score, -jnp.inf)
        _, sel = lax.top_k(score, topk)
        valid = key_chunk[sel] <= q_chunk[None, :, None]
        kg = gather(k, sel)
        vg = gather(v, sel)
        logits = jnp.einsum('bthd,btjhd->bthj', qb, kg).astype(f32) * scale
        logits = jnp.where(valid[:, :, None, :], logits, -jnp.inf)
        probs = jax.nn.softmax(logits, axis=-1).astype(v.dtype)
        return jnp.einsum('bthj,btjhd->bthd', probs, vg)

    starts = jnp.arange(seq // Q_BLOCK) * Q_BLOCK
    out = lax.map(block, starts)
    return jnp.moveaxis(out, 0, 1).reshape(bsz, seq, N_HEADS * HEAD_DIM)


def moe(h, w_router, b_router, w_gate, b_gate, w_up, b_up, w_down, b_down):
    bsz, seq, d = h.shape
    t = h.reshape(-1, d)
    logits = (t @ w_router + b_router).astype(jnp.float32)
    top_val, top_idx = lax.top_k(logits, TOP_K)
    weights = jax.nn.softmax(top_val, axis=-1)
    flat_e = top_idx.reshape(-1)
    order = jnp.argsort(flat_e)
    e_sorted = flat_e[order]
    xs = t[order // TOP_K]
    sizes = jnp.bincount(flat_e, length=N_EXPERTS).astype(jnp.int32)
    g = lax.ragged_dot(xs, w_gate, sizes) + b_gate[e_sorted]
    u = lax.ragged_dot(xs, w_up, sizes) + b_up[e_sorted]
    g = jnp.minimum(g, SWIGLU_LIMIT)
    u = jnp.clip(u, -SWIGLU_LIMIT, SWIGLU_LIMIT)
    a = (u + 1.0) * (g * jax.nn.sigmoid(SWIGLU_ALPHA * g))
    ys = lax.ragged_dot(a, w_down, sizes) + b_down[e_sorted]
    inv = jnp.argsort(order)
    y = ys[inv].reshape(-1, TOP_K, d)
    out = jnp.einsum('tk,tkd->td', weights.astype(y.dtype), y)
    return out.reshape(bsz, seq, d)


def setup_inputs(seed: int = 0) -> dict:
    key = jax.random.key(seed)
    ks = iter(jax.random.split(key, 40))
    f32 = jnp.float32

    def nrm(shape, scale):
        return jax.random.normal(next(ks), shape, f32) * scale

    def gain(shape):
        return 1.0 + nrm(shape, 0.05)

    nl, G, P, H, D, E = DEPTH, SSM_GROUPS, SSM_STATE, SSM_GROUP, D_MODEL, N_EXPERTS
    x = nrm((BATCH, SEQ, D), 1.0)
    p = nrm((DEPTH, BATCH, SEQ, PLE_DIM), 1.0)
    w_in = nrm((nl, D, IN_COLS), D ** -0.5)
    b_gates = nrm((nl, 2 * D), 0.01)
    n_idx = jnp.arange(P, dtype=f32)
    lam_re = -0.5 + nrm((nl, G, P), 0.01)
    lam_im = jnp.pi * n_idx + nrm((nl, G, P), 0.01)
    log_dt = jax.random.uniform(next(ks), (nl, G), f32, math.log(DT_MIN), math.log(DT_MAX))
    b_re = nrm((nl, G, P, H), (2 * H) ** -0.5)
    b_im = nrm((nl, G, P, H), (2 * H) ** -0.5)
    c_re = nrm((nl, G, H, P), (2 * P) ** -0.5)
    c_im = nrm((nl, G, H, P), (2 * P) ** -0.5)
    d_skip = nrm((nl, G, H), 1.0)
    w_glu = nrm((nl, SSM_WIDTH, SSM_WIDTH), SSM_WIDTH ** -0.5)
    b_glu = nrm((nl, SSM_WIDTH), 0.01)
    w_ssm_br = nrm((nl, SSM_WIDTH, D), SSM_WIDTH ** -0.5)
    w_attn_br = nrm((nl, ATTN_WIDTH, D), ATTN_WIDTH ** -0.5)
    w_o = nrm((nl, D, D), D ** -0.5)
    norm_mix = gain((nl, D))
    norm_ffn = gain((nl, D))
    w_router = nrm((nl, D, E), D ** -0.5)
    b_router = nrm((nl, E), 0.01)
    w_gate = nrm((nl, E, D, D_FF), D ** -0.5)
    b_gate = nrm((nl, E, D_FF), 0.01)
    w_up = nrm((nl, E, D, D_FF), D ** -0.5)
    b_up = nrm((nl, E, D_FF), 0.01)
    w_down = nrm((nl, E, D_FF, D), D_FF ** -0.5)
    b_down = nrm((nl, E, D), 0.01)
    w_ple_gate = nrm((nl, D, D), D ** -0.5)
    w_ple_proj = nrm((nl, PLE_DIM, D), PLE_DIM ** -0.5)
    norm_final = gain((D,))
    return {'x': x, 'p': p, 'w_in': w_in, 'b_gates': b_gates, 'lam_re': lam_re, 'lam_im': lam_im,
            'log_dt': log_dt, 'b_re': b_re, 'b_im': b_im, 'c_re': c_re, 'c_im': c_im, 'd_skip': d_skip,
            'w_glu': w_glu, 'b_glu': b_glu, 'w_ssm_br': w_ssm_br, 'w_attn_br': w_attn_br, 'w_o': w_o,
            'norm_mix': norm_mix, 'norm_ffn': norm_ffn, 'w_router': w_router, 'b_router': b_router,
            'w_gate': w_gate, 'b_gate': b_gate, 'w_up': w_up, 'b_up': b_up, 'w_down': w_down,
            'b_down': b_down, 'w_ple_gate': w_ple_gate, 'w_ple_proj': w_ple_proj, 'norm_final': norm_final}


def reference(x, p, w_in, b_gates, lam_re, lam_im, log_dt, b_re, b_im, c_re, c_im, d_skip,
              w_glu, b_glu, w_ssm_br, w_attn_br, w_o, norm_mix, norm_ffn, w_router, b_router,
              w_gate, b_gate, w_up, b_up, w_down, b_down, w_ple_gate, w_ple_proj, norm_final):
    bsz, seq, _ = x.shape
    pos = jnp.arange(seq)
    r = x
    for i in range(DEPTH):
        h = rmsnorm(r, norm_mix[i])
        z = h @ w_in[i]
        u_ssm, q, k, v, q_idx, k_idx, w_idx, z_gs, z_ga = split_in(z)

        u_ssm = u_ssm.reshape(bsz, seq, SSM_GROUPS, SSM_GROUP)
        y_ssm = s5_scan(u_ssm, lam_re[i], lam_im[i], log_dt[i], b_re[i], b_im[i],
                        c_re[i], c_im[i], d_skip[i]).reshape(bsz, seq, SSM_WIDTH)
        y_ssm = jax.nn.gelu(y_ssm)
        y_ssm = y_ssm * jax.nn.sigmoid(y_ssm @ w_glu[i] + b_glu[i])
        br_ssm = y_ssm @ w_ssm_br[i]

        q = rope(q.reshape(bsz, seq, N_HEADS, HEAD_DIM), pos)
        k = rope(k.reshape(bsz, seq, N_HEADS, HEAD_DIM), pos)
        v = v.reshape(bsz, seq, N_HEADS, HEAD_DIM)
        q_idx = rope(q_idx.reshape(bsz, seq, IDX_HEADS, IDX_DIM), pos)
        k_idx = rope(k_idx[:, :, None, :], pos)[:, :, 0, :]
        y_attn = dsa_attention(q, k, v, q_idx, k_idx, w_idx)
        br_attn = y_attn @ w_attn_br[i]

        g_ssm = jax.nn.sigmoid(z_gs + b_gates[i, :D_MODEL])
        g_attn = jax.nn.sigmoid(z_ga + b_gates[i, D_MODEL:])
        r = r + (g_ssm * br_ssm + g_attn * br_attn) @ w_o[i]

        h2 = rmsnorm(r, norm_ffn[i])
        r = r + moe(h2, w_router[i], b_router[i], w_gate[i], b_gate[i], w_up[i], b_up[i],
                    w_down[i], b_down[i])

        r = r + jax.nn.sigmoid(r @ w_ple_gate[i]) * (p[i] @ w_ple_proj[i])
    return rmsnorm(r, norm_final)
```

```python
import functools
import math

import jax
import jax.numpy as jnp
from jax import lax
from jax.experimental import pallas as pl
from jax.experimental.pallas import tpu as pltpu

F32 = jnp.float32
BF16 = jnp.bfloat16
I32 = jnp.int32

D_MODEL = 1024
CHUNK = 64
PLE_DIM = 256
EPS = 1e-6
SSM_WIDTH = 512
SSM_GROUP = 16
SSM_GROUPS = 32
SSM_STATE = 64
N_HEADS = 8
HEAD_DIM = 64
ATTN_WIDTH = 512
IDX_HEADS = 8
IDX_DIM = 32
TOPK_MAX = 256
ROPE_THETA = 10000.0
N_EXPERTS = 32
TOP_K = 4
SWIGLU_LIMIT = 7.0
SWIGLU_ALPHA = 1.702

LANES = 128
SSM_TC = 16
SSM_GL = LANES // SSM_GROUP
SSM_NLB = SSM_WIDTH // LANES
NEG_BIG = -1e30
INT_MIN = -2147483648
VMEM_LIMIT = 56 * 1024 * 1024

_C_U, _C_Q, _C_QR, _C_K, _C_KR, _C_V = 0, 512, 1024, 1536, 2048, 2560
_C_QI, _C_QIR, _C_KI, _C_KIR, _C_WI, _C_GS, _C_GA = 3072, 3328, 3584, 3840, 4096, 4224, 5248
_C_END = 6272


def _rms(x, g):
    return x * lax.rsqrt(jnp.mean(x * x, axis=-1, keepdims=True) + EPS) * g


def _dot(a, b):
    return jnp.dot(a, b, preferred_element_type=F32)


def _dot_nt(a, b):
    return lax.dot_general(a, b, (((1,), (1,)), ((), ())), preferred_element_type=F32)


def _inproj_kernel(x_ref, g_ref, w_ref, bg_ref, cq_ref, sq_ref, ci_ref, si_ref,
                   u_ref, q_ref, k_ref, v_ref, qi_ref, ki_ref, wi_ref, gs_ref, ga_ref):
    h = _rms(x_ref[...], g_ref[...]).astype(BF16)

    def mm(c0, n):
        return _dot(h, w_ref[:, c0:c0 + n])

    u = mm(_C_U, 512)
    for lb in range(SSM_NLB):
        u_ref[lb] = u[:, lb * LANES:(lb + 1) * LANES].astype(BF16)
    cq = cq_ref[...]
    sq = sq_ref[...]
    q_ref[...] = ((mm(_C_Q, 512) * cq + mm(_C_QR, 512) * sq) * (HEAD_DIM ** -0.5)).astype(BF16)
    k_ref[...] = (mm(_C_K, 512) * cq + mm(_C_KR, 512) * sq).astype(BF16)
    v_ref[...] = mm(_C_V, 512).astype(BF16)
    ci = ci_ref[...]
    si = si_ref[...]
    qi_ref[...] = (mm(_C_QI, 256) * ci + mm(_C_QIR, 256) * si).astype(BF16)
    ki_ref[...] = (mm(_C_KI, 256) * ci + mm(_C_KIR, 256) * si).astype(BF16)
    wi_ref[...] = mm(_C_WI, 128) * ((IDX_HEADS * IDX_DIM) ** -0.5)
    bg = bg_ref[...]
    gs_ref[...] = jax.nn.sigmoid(mm(_C_GS, 1024) + bg[:, :1024]).astype(BF16)
    ga_ref[...] = jax.nn.sigmoid(mm(_C_GA, 1024) + bg[:, 1024:]).astype(BF16)


def _rope_tables(seq, d, reps):
    half = d // 2
    inv = ROPE_THETA ** (-jnp.arange(half, dtype=F32) * 2.0 / d)
    ang = jnp.arange(seq, dtype=F32)[:, None] * inv[None, :]
    cos, sin = jnp.cos(ang), jnp.sin(ang)
    c = jnp.concatenate([cos, cos], axis=-1)
    s = jnp.concatenate([-sin, sin], axis=-1)
    return jnp.tile(c, (1, reps)), jnp.tile(s, (1, reps))


def _swap_halves(w, d):
    n = w.shape[1]
    return w.reshape(w.shape[0], n // d, 2, d // 2)[:, :, ::-1, :].reshape(w.shape[0], n)


def _inproj(x2, norm_mix, w_in, b_gates, seq, tm):
    t = x2.shape[0]
    w_u, w_q, w_k, w_v, w_qi, w_ki, w_wi, w_gs, w_ga = jnp.split(
        w_in, [512, 1024, 1536, 2048, 2304, 2336, 2344, 3368], axis=1)
    w_ki8 = jnp.tile(w_ki, (1, IDX_HEADS))
    w_wi_p = jnp.pad(w_wi, ((0, 0), (0, LANES - IDX_HEADS)))
    w_all = jnp.concatenate(
        [w_u, w_q, _swap_halves(w_q, HEAD_DIM), w_k, _swap_halves(w_k, HEAD_DIM), w_v,
         w_qi, _swap_halves(w_qi, IDX_DIM), w_ki8, _swap_halves(w_ki8, IDX_DIM), w_wi_p, w_gs, w_ga],
        axis=1).astype(BF16)
    assert w_all.shape[1] == _C_END
    cq, sq = _rope_tables(seq, HEAD_DIM, N_HEADS)
    ci, si = _rope_tables(seq, IDX_DIM, IDX_HEADS)
    nt = seq // tm
    row = lambda i: (i, 0)
    pos = lambda i: (i % nt, 0)
    const = lambda i: (0, 0)
    outs = pl.pallas_call(
        _inproj_kernel,
        grid=(t // tm,),
        in_specs=[
            pl.BlockSpec((tm, D_MODEL), row),
            pl.BlockSpec((1, D_MODEL), const),
            pl.BlockSpec((D_MODEL, _C_END), const),
            pl.BlockSpec((1, 2 * D_MODEL), const),
            pl.BlockSpec((tm, 512), pos), pl.BlockSpec((tm, 512), pos),
            pl.BlockSpec((tm, 256), pos), pl.BlockSpec((tm, 256), pos),
        ],
        out_specs=[
            pl.BlockSpec((SSM_NLB, tm, LANES), lambda i: (0, i, 0)),
            pl.BlockSpec((tm, 512), row), pl.BlockSpec((tm, 512), row), pl.BlockSpec((tm, 512), row),
            pl.BlockSpec((tm, 256), row), pl.BlockSpec((tm, 256), row), pl.BlockSpec((tm, LANES), row),
            pl.BlockSpec((tm, D_MODEL), row), pl.BlockSpec((tm, D_MODEL), row),
        ],
        out_shape=[
            jax.ShapeDtypeStruct((SSM_NLB, t, LANES), BF16),
            jax.ShapeDtypeStruct((t, 512), BF16), jax.ShapeDtypeStruct((t, 512), BF16),
            jax.ShapeDtypeStruct((t, 512), BF16),
            jax.ShapeDtypeStruct((t, 256), BF16), jax.ShapeDtypeStruct((t, 256), BF16),
            jax.ShapeDtypeStruct((t, LANES), F32),
            jax.ShapeDtypeStruct((t, D_MODEL), BF16), jax.ShapeDtypeStruct((t, D_MODEL), BF16),
        ],
        compiler_params=pltpu.CompilerParams(dimension_semantics=("arbitrary",),
                                             vmem_limit_bytes=VMEM_LIMIT),
        name="inproj",
    )(x2, norm_mix.reshape(1, D_MODEL), w_all, b_gates.reshape(1, 2 * D_MODEL), cq, sq, ci, si)
    return outs


def _ssm_mats(lam_re, lam_im, log_dt, b_re, b_im, c_re, c_im, d_skip):
    g_, p_, h_, tc, gl, nlb = SSM_GROUPS, SSM_STATE, SSM_GROUP, SSM_TC, SSM_GL, SSM_NLB
    lam = lax.complex(lam_re.astype(F32), lam_im.astype(F32))
    dt = jnp.exp(log_dt.astype(F32))[:, None]
    lam_dt = lam * dt
    lam_bar = jnp.exp(lam_dt)
    b_bar = ((lam_bar - 1.0) / lam)[..., None] * lax.complex(b_re.astype(F32), b_im.astype(F32))
    c = lax.complex(c_re.astype(F32), c_im.astype(F32))
    steps = jnp.arange(tc + 1, dtype=F32)
    pw = jnp.exp(lam_dt[None] * steps[:, None, None])
    taps = jnp.einsum('gop,tgp,gpi->gtoi', c, pw[:tc], b_bar).real
    lag = jnp.arange(tc)[None, :] - jnp.arange(tc)[:, None]
    kt = jnp.where((lag >= 0)[None, :, :, None, None], taps[:, jnp.clip(lag, 0, tc - 1)], 0.0)
    eye = jnp.eye(gl, dtype=F32)
    kt = kt.reshape(nlb, gl, tc, tc, h_, h_)
    m_intra = jnp.einsum('bgjloi,gm->bjgilmo', kt, eye).reshape(nlb, tc * LANES, tc * LANES)
    sc = pw[:tc][::-1][:, :, :, None] * b_bar[None]
    sc = jnp.stack([sc.real, sc.imag], axis=0)
    sc = sc.reshape(2, tc, nlb, gl, p_, h_)
    m_state = jnp.einsum('rjbgpi,gm->bjgirmp', sc, eye).reshape(nlb, tc * LANES, 2 * gl * p_)
    oc = c[None] * pw[1:tc + 1][:, :, None, :]
    oc = jnp.stack([oc.real, -oc.imag], axis=0)
    oc = oc.reshape(2, tc, nlb, gl, h_, p_)
    m_out = jnp.einsum('rlbgop,gm->brgplmo', oc, eye).reshape(nlb, 2 * gl * p_, tc * LANES)
    a = pw[tc].reshape(nlb, 1, gl * p_)
    d_l = jnp.tile(d_skip.astype(F32).reshape(nlb, 1, LANES), (1, 1, tc))
    return (m_intra.astype(BF16), m_state.astype(BF16), m_out.astype(BF16),
            a.real.astype(F32), a.imag.astype(F32), d_l)


def _ssm_kernel(u_ref, mi_ref, ms_ref, mo_ref, are_ref, aim_ref, d_ref, y_ref, con_ref, sp_ref):
    nch = u_ref.shape[0]
    ns = are_ref.shape[1]
    u = u_ref[...]
    con_ref[...] = _dot(u, ms_ref[...])
    a_re = are_ref[...]
    a_im = aim_ref[...]

    def step(c, carry):
        s_re, s_im = carry
        sp_ref[pl.ds(c, 1), :ns] = s_re
        sp_ref[pl.ds(c, 1), ns:] = s_im
        c_re = con_ref[pl.ds(c, 1), :ns]
        c_im = con_ref[pl.ds(c, 1), ns:]
        return (a_re * s_re - a_im * s_im + c_re, a_re * s_im + a_im * s_re + c_im)

    zero = jnp.zeros((1, ns), F32)
    lax.fori_loop(0, nch, step, (zero, zero))
    y = _dot(u, mi_ref[...]) + _dot(sp_ref[...].astype(BF16), mo_ref[...])
    y_ref[...] = y + d_ref[...] * u.astype(F32)


def _ssm(u4, mats, bsz, seq):
    m_intra, m_state, m_out, a_re, a_im, d_l = mats
    t = u4.shape[1]
    nch = seq // SSM_TC
    wc = SSM_TC * LANES
    ns = SSM_GL * SSM_STATE
    u_c = u4.reshape(SSM_NLB, t // SSM_TC, wc)
    wmap = lambda lb, b: (lb, 0, 0)
    y = pl.pallas_call(
        _ssm_kernel,
        grid=(SSM_NLB, bsz),
        in_specs=[
            pl.BlockSpec((None, nch, wc), lambda lb, b: (lb, b, 0)),
            pl.BlockSpec((None, wc, wc), wmap),
            pl.BlockSpec((None, wc, 2 * ns), wmap),
            pl.BlockSpec((None, 2 * ns, wc), wmap),
            pl.BlockSpec((None, 1, ns), wmap), pl.BlockSpec((None, 1, ns), wmap),
            pl.BlockSpec((None, 1, wc), wmap),
        ],
        out_specs=pl.BlockSpec((None, nch, wc), lambda lb, b: (lb, b, 0)),
        out_shape=jax.ShapeDtypeStruct((SSM_NLB, t // SSM_TC, wc), F32),
        scratch_shapes=[pltpu.VMEM((nch, 2 * ns), F32), pltpu.VMEM((nch, 2 * ns), F32)],
        compiler_params=pltpu.CompilerParams(dimension_semantics=("arbitrary", "arbitrary"),
                                             vmem_limit_bytes=VMEM_LIMIT),
        name="ssm",
    )(u_c, m_intra, m_state, m_out, a_re, a_im, d_l)
    return y.reshape(SSM_NLB, t, LANES)


def _attn_kernel(q_ref, k_ref, v_ref, qi_ref, ki_ref, wi_ref, o_ref, keys_ref, qh_ref, qm_ref, jcut_ref,
                 *, tq, tk, seq, topk):
    i = pl.program_id(1)
    s0 = i * tq
    nk = (s0 + tq + tk - 1) // tk
    row = s0 + lax.broadcasted_iota(I32, (tq, 1), 0)
    lim = ((row >> 6) + 1) << 6
    kvec = jnp.minimum(lim, topk).astype(F32)
    col0 = lax.broadcasted_iota(I32, (tq, tk), 1)
    col_l = lax.broadcasted_iota(I32, (tq, LANES), 1)

    qi = qi_ref[...]
    lane_i = lax.broadcasted_iota(I32, qi.shape, 1)
    for h in range(IDX_HEADS):
        qh_ref[h] = jnp.where((lane_i >> 5) == h, qi, jnp.zeros_like(qi))
    wi = wi_ref[...]
    wcol = [wi[:, h:h + 1] for h in range(IDX_HEADS)]

    def score_tile(j, carry):
        kt = ki_ref[pl.ds(pl.multiple_of(j * tk, tk), tk), :]
        acc = jnp.zeros((tq, tk), F32)
        for h in range(IDX_HEADS):
            acc = acc + wcol[h] * jnp.maximum(_dot_nt(qh_ref[h], kt), 0.0)
        sc = jnp.where(col0 + j * tk < lim, acc, -jnp.inf)
        bits = pltpu.bitcast(sc, I32)
        keys_ref[j] = bits ^ ((bits >> 31) & 0x7FFFFFFF)
        return carry

    lax.fori_loop(0, nk, score_tile, 0)

    def count(ind_fn):
        def body(j, part):
            kk = keys_ref[j]
            for c in range(tk // LANES):
                sl = slice(c * LANES, (c + 1) * LANES)
                part = part + ind_fn(kk[:, sl], col_l + (j * tk + c * LANES))
            return part
        part = lax.fori_loop(0, nk, body, jnp.zeros((tq, LANES), F32))
        return jnp.sum(part, axis=1, keepdims=True)

    def bit_step(it, u_ans):
        bit = lax.shift_left(jnp.int32(1), 31 - it)
        cand_u = u_ans | bit
        cand = cand_u ^ INT_MIN
        cnt = count(lambda kk, col: jnp.where(kk >= cand, 1.0, 0.0))
        return jnp.where(cnt >= kvec, cand_u, u_ans)

    u_ans = lax.fori_loop(0, 32, bit_step, jnp.zeros((tq, 1), I32))
    vth = u_ans ^ INT_MIN
    cnt_gt = count(lambda kk, col: jnp.where(kk > vth, 1.0, 0.0))
    cnt_eq = count(lambda kk, col: jnp.where(kk == vth, 1.0, 0.0))
    need = kvec - cnt_gt
    has_tie = jnp.max(cnt_eq - need) > 0.0
    jcut_ref[...] = jnp.full((tq, 1), seq, I32)

    @pl.when(has_tie)
    def _():
        def bis(it, lohi):
            lo, hi = lohi
            mid = (lo + hi) >> 1
            c = count(lambda kk, col: jnp.where(col <= mid, jnp.where(kk == vth, 1.0, 0.0), 0.0))
            ok = c >= need
            return jnp.where(ok, lo, mid + 1), jnp.where(ok, mid, hi)
        nbits = int(math.log2(seq))
        lo, hi = lax.fori_loop(0, nbits, bis,
                               (jnp.zeros((tq, 1), I32), jnp.full((tq, 1), seq - 1, I32)))
        jcut_ref[...] = hi

    jcut = jcut_ref[...]

    def to_bias(j, carry):
        kk = keys_ref[j]
        thr = jnp.where(col0 + j * tk <= jcut, vth, vth + 1)
        keys_ref[j] = pltpu.bitcast(jnp.where(kk >= thr, 0.0, NEG_BIG).astype(F32), I32)
        return carry

    lax.fori_loop(0, nk, to_bias, 0)

    lane_q = lax.broadcasted_iota(I32, (tq, LANES), 1)
    for h in range(N_HEADS):
        qp = q_ref[:, (h // 2) * LANES:(h // 2 + 1) * LANES]
        keep = (lane_q >= HEAD_DIM) if h % 2 else (lane_q < HEAD_DIM)
        qm_ref[h] = jnp.where(keep, qp, jnp.zeros_like(qp))

    def head_out(h):
        cs = slice((h // 2) * LANES, (h // 2 + 1) * LANES)

        def body(j, carry):
            m, l, acc = carry
            rows = pl.ds(pl.multiple_of(j * tk, tk), tk)
            s = _dot_nt(qm_ref[h], k_ref[rows, cs]) + pltpu.bitcast(keys_ref[j], F32)
            m_new = jnp.maximum(m, jnp.max(s, axis=1, keepdims=True))
            alpha = jnp.exp(m - m_new)
            p = jnp.exp(s - m_new)
            l = alpha * l + jnp.sum(p, axis=1, keepdims=True)
            acc = alpha * acc + _dot(p.astype(BF16), v_ref[rows, cs])
            return m_new, l, acc

        m, l, acc = lax.fori_loop(
            0, nk, body,
            (jnp.full((tq, 1), NEG_BIG, F32), jnp.zeros((tq, 1), F32), jnp.zeros((tq, LANES), F32)))
        return acc / l

    for hp in range(N_HEADS // 2):
        o_ref[:, hp * LANES:(hp + 1) * LANES] = jnp.where(
            lane_q < HEAD_DIM, head_out(2 * hp), head_out(2 * hp + 1)).astype(o_ref.dtype)


def _attn(q, k, v, qi, ki8, wi, bsz, seq, tq, tk):
    t = q.shape[0]
    nq = seq // tq
    qrow = lambda b, i: (b * nq + i, 0)
    kv = lambda b, i: (b, 0)
    kern = functools.partial(_attn_kernel, tq=tq, tk=tk, seq=seq, topk=min(TOPK_MAX, seq // 4))
    return pl.pallas_call(
        kern,
        grid=(bsz, nq),
        in_specs=[
            pl.BlockSpec((tq, ATTN_WIDTH), qrow),
            pl.BlockSpec((seq, ATTN_WIDTH), kv), pl.BlockSpec((seq, ATTN_WIDTH), kv),
            pl.BlockSpec((tq, 256), qrow), pl.BlockSpec((seq, 256), kv),
            pl.BlockSpec((tq, LANES), qrow),
        ],
        out_specs=pl.BlockSpec((tq, ATTN_WIDTH), qrow),
        out_shape=jax.ShapeDtypeStruct((t, ATTN_WIDTH), BF16),
        scratch_shapes=[
            pltpu.VMEM((seq // tk, tq, tk), I32),
            pltpu.VMEM((IDX_HEADS, tq, 256), BF16),
            pltpu.VMEM((N_HEADS, tq, LANES), BF16),
            pltpu.VMEM((tq, 1), I32),
        ],
        compiler_params=pltpu.CompilerParams(dimension_semantics=("arbitrary", "arbitrary"),
                                             vmem_limit_bytes=VMEM_LIMIT),
        name="attn",
    )(q, k, v, qi, ki8, wi)


def _mix_kernel(y4_ref, ya_ref, gs_ref, ga_ref, x_ref, wglu_ref, bglu_ref, wsb_ref, wab_ref, wo_ref,
                nf_ref, wrh_ref, wrl_ref, br_ref, r1_ref, h2_ref, route_ref):
    ys = jnp.concatenate([y4_ref[lb] for lb in range(SSM_NLB)], axis=1)
    ys = 0.5 * ys * (1.0 + jnp.tanh(math.sqrt(2.0 / math.pi) * (ys + 0.044715 * (ys * ys * ys))))
    ys = ys * jax.nn.sigmoid(_dot(ys.astype(BF16), wglu_ref[...]) + bglu_ref[...])
    br_s = _dot(ys.astype(BF16), wsb_ref[...])
    br_a = _dot(ya_ref[...], wab_ref[...])
    mix = gs_ref[...].astype(F32) * br_s + ga_ref[...].astype(F32) * br_a
    r1 = x_ref[...] + _dot(mix.astype(BF16), wo_ref[...])
    r1_ref[...] = r1
    h2 = _rms(r1, nf_ref[...])
    h2_ref[...] = h2
    hi = h2.astype(BF16)
    lo = (h2 - hi.astype(F32)).astype(BF16)
    logits = _dot(hi, wrh_ref[...]) + (_dot(hi, wrl_ref[...]) + _dot(lo, wrh_ref[...])) + br_ref[...]
    tm = logits.shape[0]
    lane = lax.broadcasted_iota(I32, (tm, LANES), 1).astype(F32)
    vals = logits
    tops, idxs = [], []
    for _ in range(TOP_K):
        m = jnp.max(vals, axis=1, keepdims=True)
        idx = jnp.min(jnp.where(vals == m, lane, float(LANES)), axis=1, keepdims=True)
        tops.append(m)
        idxs.append(idx)
        vals = jnp.where(lane == idx, -jnp.inf, vals)
    es = [jnp.exp(tv - tops[0]) for tv in tops]
    den = es[0] + es[1] + es[2] + es[3]
    route = jnp.zeros((tm, LANES), F32)
    for kk in range(TOP_K):
        route = jnp.where(lane == float(kk), idxs[kk], route)
        route = jnp.where(lane == float(TOP_K + kk), es[kk] / den, route)
    route_ref[...] = route


def _mix(y4, y_attn, gs, ga, x2, w_glu, b_glu, w_ssm_br, w_attn_br, w_o, norm_ffn, w_router, b_router, tm):
    t = x2.shape[0]
    wr = jnp.pad(w_router.astype(F32), ((0, 0), (0, LANES - N_EXPERTS)))
    wr_hi = wr.astype(BF16)
    wr_lo = (wr - wr_hi.astype(F32)).astype(BF16)
    br = jnp.concatenate([b_router.astype(F32), jnp.full((LANES - N_EXPERTS,), NEG_BIG, F32)]).reshape(1, LANES)
    row = lambda i: (i, 0)
    const = lambda i: (0, 0)
    return pl.pallas_call(
        _mix_kernel,
        grid=(t // tm,),
        in_specs=[
            pl.BlockSpec((SSM_NLB, tm, LANES), lambda i: (0, i, 0)),
            pl.BlockSpec((tm, ATTN_WIDTH), row),
            pl.BlockSpec((tm, D_MODEL), row), pl.BlockSpec((tm, D_MODEL), row), pl.BlockSpec((tm, D_MODEL), row),
            pl.BlockSpec((SSM_WIDTH, SSM_WIDTH), const), pl.BlockSpec((1, SSM_WIDTH), const),
            pl.BlockSpec((SSM_WIDTH, D_MODEL), const), pl.BlockSpec((ATTN_WIDTH, D_MODEL), const),
            pl.BlockSpec((D_MODEL, D_MODEL), const), pl.BlockSpec((1, D_MODEL), const),
            pl.BlockSpec((D_MODEL, LANES), const), pl.BlockSpec((D_MODEL, LANES), const),
            pl.BlockSpec((1, LANES), const),
        ],
        out_specs=[pl.BlockSpec((tm, D_MODEL), row), pl.BlockSpec((tm, D_MODEL), row),
                   pl.BlockSpec((tm, LANES), row)],
        out_shape=[jax.ShapeDtypeStruct((t, D_MODEL), F32), jax.ShapeDtypeStruct((t, D_MODEL), F32),
                   jax.ShapeDtypeStruct((t, LANES), F32)],
        compiler_params=pltpu.CompilerParams(dimension_semantics=("arbitrary",),
                                             vmem_limit_bytes=VMEM_LIMIT),
        name="mix",
    )(y4, y_attn, gs, ga, x2, w_glu.astype(BF16), b_glu.reshape(1, SSM_WIDTH).astype(F32),
      w_ssm_br.astype(BF16), w_attn_br.astype(BF16), w_o.astype(BF16), norm_ffn.reshape(1, D_MODEL),
      wr_hi, wr_lo, br)


def _prefix_kernel(route_ref, tri_ref, pre_ref, tot_ref, carry_ref):
    @pl.when(pl.program_id(0) == 0)
    def _():
        carry_ref[...] = jnp.zeros_like(carry_ref)

    route = route_ref[...]
    tm = route.shape[0]
    lane = lax.broadcasted_iota(I32, (tm, LANES), 1).astype(F32)
    mask = jnp.zeros((tm, LANES), F32)
    for kk in range(TOP_K):
        mask = jnp.where(lane == route[:, kk:kk + 1], 1.0, mask)
    incl = _dot(tri_ref[...], mask.astype(BF16))
    carry = carry_ref[...]
    pre_ref[...] = carry + incl - mask
    carry = carry + incl[tm - 1:tm, :]
    carry_ref[...] = carry
    tot_ref[...] = carry


def _prefix(route, tm):
    t = route.shape[0]
    tri = (jnp.arange(tm)[:, None] >= jnp.arange(tm)[None, :]).astype(BF16)
    return pl.pallas_call(
        _prefix_kernel,
        grid=(t // tm,),
        in_specs=[pl.BlockSpec((tm, LANES), lambda i: (i, 0)), pl.BlockSpec((tm, tm), lambda i: (0, 0))],
        out_specs=[pl.BlockSpec((tm, LANES), lambda i: (i, 0)), pl.BlockSpec((1, LANES), lambda i: (0, 0))],
        out_shape=[jax.ShapeDtypeStruct((t, LANES), F32), jax.ShapeDtypeStruct((1, LANES), F32)],
        scratch_shapes=[pltpu.VMEM((1, LANES), F32)],
        compiler_params=pltpu.CompilerParams(dimension_semantics=("arbitrary",)),
        name="prefix",
    )(route, tri)


def _scatter_kernel(dest_ref, h2_ref, xs_in_ref, xs_ref, sem):
    del xs_in_ref
    tm = h2_ref.shape[0]

    def row_copy(r, kk):
        d = dest_ref[0, r * TOP_K + kk]
        return pltpu.make_async_copy(h2_ref.at[pl.ds(r, 1)], xs_ref.at[pl.ds(d, 1)], sem)

    def start(r, carry):
        for kk in range(TOP_K):
            row_copy(r, kk).start()
        return carry

    lax.fori_loop(0, tm, start, 0)

    def wait(r, carry):
        for kk in range(TOP_K):
            row_copy(r, kk).wait()
        return carry

    lax.fori_loop(0, tm, wait, 0)


def _scatter(dest, h2, n_rows, tm):
    t = h2.shape[0]
    dest3 = dest.reshape(t // tm, 1, tm * TOP_K)
    xs0 = jnp.zeros((n_rows, D_MODEL), F32)
    return pl.pallas_call(
        _scatter_kernel,
        grid=(t // tm,),
        in_specs=[
            pl.BlockSpec((None, 1, tm * TOP_K), lambda i: (i, 0, 0), memory_space=pltpu.SMEM),
            pl.BlockSpec((tm, D_MODEL), lambda i: (i, 0)),
            pl.BlockSpec(memory_space=pl.ANY),
        ],
        out_specs=pl.BlockSpec(memory_space=pl.ANY),
        out_shape=jax.ShapeDtypeStruct((n_rows, D_MODEL), F32),
        scratch_shapes=[pltpu.SemaphoreType.DMA(())],
        input_output_aliases={2: 0},
        compiler_params=pltpu.CompilerParams(dimension_semantics=("arbitrary",), has_side_effects=True),
        name="scatter",
    )(dest3, h2, xs0)


def _expert_kernel(te_ref, tv_ref, x_ref, wg_ref, bg_ref, wu_ref, bu_ref, wd_ref, bd_ref, y_ref,
                   wgb_ref, wub_ref, wdb_ref):
    i = pl.program_id(0)
    e = te_ref[i]
    prev = te_ref[jnp.maximum(i - 1, 0)]

    @pl.when((i == 0) | (e != prev))
    def _():
        wgb_ref[...] = wg_ref[...].astype(BF16)
        wub_ref[...] = wu_ref[...].astype(BF16)
        wdb_ref[...] = wd_ref[...].astype(BF16)

    @pl.when(tv_ref[i] != 0)
    def _():
        x = x_ref[...].astype(BF16)
        g = jnp.minimum(_dot(x, wgb_ref[...]) + bg_ref[...], SWIGLU_LIMIT)
        u = jnp.clip(_dot(x, wub_ref[...]) + bu_ref[...], -SWIGLU_LIMIT, SWIGLU_LIMIT)
        a = (u + 1.0) * (g * jax.nn.sigmoid(SWIGLU_ALPHA * g))
        y_ref[...] = _dot(a.astype(BF16), wdb_ref[...]) + bd_ref[...]

    @pl.when(tv_ref[i] == 0)
    def _():
        y_ref[...] = jnp.zeros_like(y_ref)


def _experts(tile_expert, tile_valid, xs, w_gate, b_gate, w_up, b_up, w_down, b_down, tmx):
    n_rows = xs.shape[0]
    wmap = lambda i, te, tv: (te[i], 0, 0)
    row = lambda i, te, tv: (i, 0)
    d_ff = w_gate.shape[2]
    grid_spec = pltpu.PrefetchScalarGridSpec(
        num_scalar_prefetch=2,
        grid=(n_rows // tmx,),
        in_specs=[
            pl.BlockSpec((tmx, D_MODEL), row),
            pl.BlockSpec((None, D_MODEL, d_ff), wmap), pl.BlockSpec((None, 1, d_ff), wmap),
            pl.BlockSpec((None, D_MODEL, d_ff), wmap), pl.BlockSpec((None, 1, d_ff), wmap),
            pl.BlockSpec((None, d_ff, D_MODEL), wmap), pl.BlockSpec((None, 1, D_MODEL), wmap),
        ],
        out_specs=pl.BlockSpec((tmx, D_MODEL), row),
        scratch_shapes=[pltpu.VMEM((D_MODEL, d_ff), BF16), pltpu.VMEM((D_MODEL, d_ff), BF16),
                        pltpu.VMEM((d_ff, D_MODEL), BF16)],
    )
    return pl.pallas_call(
        _expert_kernel,
        grid_spec=grid_spec,
        out_shape=jax.ShapeDtypeStruct((n_rows, D_MODEL), F32),
        compiler_params=pltpu.CompilerParams(dimension_semantics=("arbitrary",),
                                             vmem_limit_bytes=VMEM_LIMIT),
        name="experts",
    )(tile_expert, tile_valid, xs, w_gate, b_gate.reshape(N_EXPERTS, 1, d_ff), w_up,
      b_up.reshape(N_EXPERTS, 1, d_ff), w_down, b_down.reshape(N_EXPERTS, 1, D_MODEL))


def _combine_kernel(dest_ref, route_ref, r1_ref, p_ref, ys_ref, wpg_ref, wpp_ref, nfin_ref, o_ref,
                    buf_ref, sem):
    tm = r1_ref.shape[0]

    def row_copy(r, kk):
        d = dest_ref[0, r * TOP_K + kk]
        return pltpu.make_async_copy(ys_ref.at[pl.ds(d, 1)], buf_ref.at[kk, pl.ds(r, 1)], sem)

    def start(r, carry):
        for kk in range(TOP_K):
            row_copy(r, kk).start()
        return carry

    lax.fori_loop(0, tm, start, 0)

    def wait(r, carry):
        for kk in range(TOP_K):
            row_copy(r, kk).wait()
        return carry

    lax.fori_loop(0, tm, wait, 0)

    route = route_ref[...]
    moe = route[:, TOP_K:TOP_K + 1] * buf_ref[0]
    for kk in range(1, TOP_K):
        moe = moe + route[:, TOP_K + kk:TOP_K + kk + 1] * buf_ref[kk]
    r2 = r1_ref[...] + moe
    gate = jax.nn.sigmoid(_dot(r2.astype(BF16), wpg_ref[...]))
    r3 = r2 + gate * _dot(p_ref[...].astype(BF16), wpp_ref[...])
    o_ref[...] = _rms(r3, nfin_ref[...])


def _combine(dest, route, r1, p2, ys, w_ple_gate, w_ple_proj, norm_final, tm):
    t = r1.shape[0]
    dest3 = dest.reshape(t // tm, 1, tm * TOP_K)
    row = lambda i: (i, 0)
    const = lambda i: (0, 0)
    return pl.pallas_call(
        _combine_kernel,
        grid=(t // tm,),
        in_specs=[
            pl.BlockSpec((None, 1, tm * TOP_K), lambda i: (i, 0, 0), memory_space=pltpu.SMEM),
            pl.BlockSpec((tm, LANES), row),
            pl.BlockSpec((tm, D_MODEL), row),
            pl.BlockSpec((tm, PLE_DIM), row),
            pl.BlockSpec(memory_space=pl.ANY),
            pl.BlockSpec((D_MODEL, D_MODEL), const), pl.BlockSpec((PLE_DIM, D_MODEL), const),
            pl.BlockSpec((1, D_MODEL), const),
        ],
        out_specs=pl.BlockSpec((tm, D_MODEL), row),
        out_shape=jax.ShapeDtypeStruct((t, D_MODEL), F32),
        scratch_shapes=[pltpu.VMEM((TOP_K, tm, D_MODEL), F32), pltpu.SemaphoreType.DMA(())],
        compiler_params=pltpu.CompilerParams(dimension_semantics=("arbitrary",),
                                             vmem_limit_bytes=VMEM_LIMIT),
        name="combine",
    )(dest3, route, r1, p2, ys, w_ple_gate.astype(BF16), w_ple_proj.astype(BF16),
      norm_final.reshape(1, D_MODEL))


def _moe(h2, route, r1, p2, w_gate, b_gate, w_up, b_up, w_down, b_down, w_ple_gate, w_ple_proj,
         norm_final, tm_prefix, tm_rows, tmx):
    t = h2.shape[0]
    pre, tot = _prefix(route, tm_prefix)
    counts = tot[0, :N_EXPERTS].astype(I32)
    padded = ((counts + tmx - 1) // tmx) * tmx
    ends = jnp.cumsum(padded)
    off = ends - padded
    n_tiles = (t * TOP_K) // tmx + N_EXPERTS
    tile_start = jnp.arange(n_tiles, dtype=I32) * tmx
    tile_valid = (tile_start < ends[-1]).astype(I32)
    tile_expert = jnp.minimum(jnp.searchsorted(ends, tile_start, side='right'), N_EXPERTS - 1).astype(I32)
    last_e = jnp.max(jnp.where(tile_valid != 0, tile_expert, 0))
    tile_expert = jnp.where(tile_valid != 0, tile_expert, last_e)
    top_idx = route[:, :TOP_K].astype(I32)
    dest = jnp.take_along_axis(pre[:, :N_EXPERTS].astype(I32) + off[None, :], top_idx, axis=1)
    xs = _scatter(dest, h2, n_tiles * tmx, tm_rows)
    ys = _experts(tile_expert, tile_valid, xs, w_gate, b_gate, w_up, b_up, w_down, b_down, tmx)
    return _combine(dest, route, r1, p2, ys, w_ple_gate, w_ple_proj, norm_final, tm_rows)


def kernel(x, p, w_in, b_gates, lam_re, lam_im, log_dt, b_re, b_im, c_re, c_im, d_skip, w_glu, b_glu,
           w_ssm_br, w_attn_br, w_o, norm_mix, norm_ffn, w_router, b_router, w_gate, b_gate, w_up, b_up,
           w_down, b_down, w_ple_gate, w_ple_proj, norm_final):
    bsz, seq, _ = x.shape
    t = bsz * seq
    x2 = x.reshape(t, D_MODEL)
    u4, q, k, v, qi, ki8, wi, gs, ga = _inproj(x2, norm_mix[0], w_in[0], b_gates[0], seq, tm=256)
    mats = _ssm_mats(lam_re[0], lam_im[0], log_dt[0], b_re[0], b_im[0], c_re[0], c_im[0], d_skip[0])
    y4 = _ssm(u4, mats, bsz, seq)
    y_attn = _attn(q, k, v, qi, ki8, wi, bsz, seq, tq=256, tk=512)
    r1, h2, route = _mix(y4, y_attn, gs, ga, x2, w_glu[0], b_glu[0], w_ssm_br[0], w_attn_br[0], w_o[0],
                         norm_ffn[0], w_router[0], b_router[0], tm=256)
    out = _moe(h2, route, r1, p[0].reshape(t, PLE_DIM), w_gate[0], b_gate[0], w_up[0], b_up[0], w_down[0],
               b_down[0], w_ple_gate[0], w_ple_proj[0], norm_final, tm_prefix=512, tm_rows=256, tmx=512)
    return out.reshape(bsz, seq, D_MODEL)
```

```python
import functools
import math

import jax
import jax.numpy as jnp
from jax import lax
from jax.experimental import pallas as pl
from jax.experimental.pallas import tpu as pltpu

F32 = jnp.float32
BF16 = jnp.bfloat16
I32 = jnp.int32

D_MODEL = 1024
CHUNK = 64
PLE_DIM = 256
EPS = 1e-6
SSM_WIDTH = 512
SSM_GROUP = 16
SSM_GROUPS = 32
SSM_STATE = 64
N_HEADS = 8
HEAD_DIM = 64
ATTN_WIDTH = 512
IDX_HEADS = 8
IDX_DIM = 32
TOPK_MAX = 256
ROPE_THETA = 10000.0
N_EXPERTS = 32
TOP_K = 4
SWIGLU_LIMIT = 7.0
SWIGLU_ALPHA = 1.702

LANES = 128
SSM_TC = 16
SSM_GL = LANES // SSM_GROUP
SSM_NLB = SSM_WIDTH // LANES
NEG_BIG = -1e30
LOG2E = 1.4426950408889634
INT_MIN = -2147483648
VMEM_LIMIT = 56 * 1024 * 1024

_C_U, _C_Q, _C_QR, _C_K, _C_KR, _C_V = 0, 512, 1024, 1536, 2048, 2560
_C_QI, _C_QIR, _C_KI, _C_KIR, _C_WI, _C_GS, _C_GA = 3072, 3328, 3584, 3840, 4096, 4224, 5248
_C_END = 6272


def _rms(x, g):
    return x * lax.rsqrt(jnp.mean(x * x, axis=-1, keepdims=True) + EPS) * g


def _dot(a, b):
    return jnp.dot(a, b, preferred_element_type=F32)


def _dot_nt(a, b):
    return lax.dot_general(a, b, (((1,), (1,)), ((), ())), preferred_element_type=F32)


def _inproj_kernel(x_ref, g_ref, w_ref, bg_ref, cq_ref, sq_ref, ci_ref, si_ref,
                   u_ref, q_ref, k_ref, v_ref, qi_ref, ki_ref, wi_ref, gs_ref, ga_ref):
    h = _rms(x_ref[...], g_ref[...]).astype(BF16)

    def mm(c0, n):
        return _dot(h, w_ref[:, c0:c0 + n])

    u = mm(_C_U, 512)
    for lb in range(SSM_NLB):
        u_ref[lb] = u[:, lb * LANES:(lb + 1) * LANES]
    cq = cq_ref[...]
    sq = sq_ref[...]
    q_ref[...] = ((mm(_C_Q, 512) * cq + mm(_C_QR, 512) * sq) * (HEAD_DIM ** -0.5 * LOG2E)).astype(BF16)
    k_ref[...] = (mm(_C_K, 512) * cq + mm(_C_KR, 512) * sq).astype(BF16)
    v_ref[...] = mm(_C_V, 512).astype(BF16)
    ci = ci_ref[...]
    si = si_ref[...]
    qi_ref[...] = (mm(_C_QI, 256) * ci + mm(_C_QIR, 256) * si).astype(BF16)
    ki_ref[...] = (mm(_C_KI, 256) * ci + mm(_C_KIR, 256) * si).astype(BF16)
    wi_ref[...] = mm(_C_WI, 128) * ((IDX_HEADS * IDX_DIM) ** -0.5)
    bg = bg_ref[...]
    gs_ref[...] = jax.nn.sigmoid(mm(_C_GS, 1024) + bg[:, :1024]).astype(BF16)
    ga_ref[...] = jax.nn.sigmoid(mm(_C_GA, 1024) + bg[:, 1024:]).astype(BF16)


def _rope_tables(seq, d, reps):
    half = d // 2
    inv = ROPE_THETA ** (-jnp.arange(half, dtype=F32) * 2.0 / d)
    ang = jnp.arange(seq, dtype=F32)[:, None] * inv[None, :]
    cos, sin = jnp.cos(ang), jnp.sin(ang)
    c = jnp.concatenate([cos, cos], axis=-1)
    s = jnp.concatenate([-sin, sin], axis=-1)
    return jnp.tile(c, (1, reps)), jnp.tile(s, (1, reps))


def _swap_halves(w, d):
    n = w.shape[1]
    return w.reshape(w.shape[0], n // d, 2, d // 2)[:, :, ::-1, :].reshape(w.shape[0], n)


def _inproj(x2, norm_mix, w_in, b_gates, seq, tm):
    t = x2.shape[0]
    w_u, w_q, w_k, w_v, w_qi, w_ki, w_wi, w_gs, w_ga = jnp.split(
        w_in, [512, 1024, 1536, 2048, 2304, 2336, 2344, 3368], axis=1)
    w_ki8 = jnp.tile(w_ki, (1, IDX_HEADS))
    w_wi_p = jnp.pad(w_wi, ((0, 0), (0, LANES - IDX_HEADS)))
    w_all = jnp.concatenate(
        [w_u, w_q, _swap_halves(w_q, HEAD_DIM), w_k, _swap_halves(w_k, HEAD_DIM), w_v,
         w_qi, _swap_halves(w_qi, IDX_DIM), w_ki8, _swap_halves(w_ki8, IDX_DIM), w_wi_p, w_gs, w_ga],
        axis=1).astype(BF16)
    assert w_all.shape[1] == _C_END
    cq, sq = _rope_tables(seq, HEAD_DIM, N_HEADS)
    ci, si = _rope_tables(seq, IDX_DIM, IDX_HEADS)
    nt = seq // tm
    row = lambda i: (i, 0)
    pos = lambda i: (i % nt, 0)
    const = lambda i: (0, 0)
    outs = pl.pallas_call(
        _inproj_kernel,
        grid=(t // tm,),
        in_specs=[
            pl.BlockSpec((tm, D_MODEL), row),
            pl.BlockSpec((1, D_MODEL), const),
            pl.BlockSpec((D_MODEL, _C_END), const),
            pl.BlockSpec((1, 2 * D_MODEL), const),
            pl.BlockSpec((tm, 512), pos), pl.BlockSpec((tm, 512), pos),
            pl.BlockSpec((tm, 256), pos), pl.BlockSpec((tm, 256), pos),
        ],
        out_specs=[
            pl.BlockSpec((SSM_NLB, tm, LANES), lambda i: (0, i, 0)),
            pl.BlockSpec((tm, 512), row), pl.BlockSpec((tm, 512), row), pl.BlockSpec((tm, 512), row),
            pl.BlockSpec((tm, 256), row), pl.BlockSpec((tm, 256), row), pl.BlockSpec((tm, LANES), row),
            pl.BlockSpec((tm, D_MODEL), row), pl.BlockSpec((tm, D_MODEL), row),
        ],
        out_shape=[
            jax.ShapeDtypeStruct((SSM_NLB, t, LANES), F32),
            jax.ShapeDtypeStruct((t, 512), BF16), jax.ShapeDtypeStruct((t, 512), BF16),
            jax.ShapeDtypeStruct((t, 512), BF16),
            jax.ShapeDtypeStruct((t, 256), BF16), jax.ShapeDtypeStruct((t, 256), BF16),
            jax.ShapeDtypeStruct((t, LANES), F32),
            jax.ShapeDtypeStruct((t, D_MODEL), BF16), jax.ShapeDtypeStruct((t, D_MODEL), BF16),
        ],
        compiler_params=pltpu.CompilerParams(dimension_semantics=("arbitrary",),
                                             vmem_limit_bytes=VMEM_LIMIT),
        name="inproj",
    )(x2, norm_mix.reshape(1, D_MODEL), w_all, b_gates.reshape(1, 2 * D_MODEL), cq, sq, ci, si)
    return outs


def _ssm_mats(lam_re, lam_im, log_dt, b_re, b_im, c_re, c_im, d_skip):
    g_, p_, h_, tc, gl, nlb = SSM_GROUPS, SSM_STATE, SSM_GROUP, SSM_TC, SSM_GL, SSM_NLB
    lam = lax.complex(lam_re.astype(F32), lam_im.astype(F32))
    dt = jnp.exp(log_dt.astype(F32))[:, None]
    lam_dt = lam * dt
    lam_bar = jnp.exp(lam_dt)
    b_bar = ((lam_bar - 1.0) / lam)[..., None] * lax.complex(b_re.astype(F32), b_im.astype(F32))
    c = lax.complex(c_re.astype(F32), c_im.astype(F32))
    steps = jnp.arange(tc + 1, dtype=F32)
    pw = jnp.exp(lam_dt[None] * steps[:, None, None])
    taps = jnp.einsum('gop,tgp,gpi->gtoi', c, pw[:tc], b_bar).real
    lag = jnp.arange(tc)[None, :] - jnp.arange(tc)[:, None]
    kt = jnp.where((lag >= 0)[None, :, :, None, None], taps[:, jnp.clip(lag, 0, tc - 1)], 0.0)
    eye = jnp.eye(gl, dtype=F32)
    kt = kt.reshape(nlb, gl, tc, tc, h_, h_)
    m_intra = jnp.einsum('bgjloi,gm->bjgilmo', kt, eye).reshape(nlb, tc * LANES, tc * LANES)
    sc = pw[:tc][::-1][:, :, :, None] * b_bar[None]
    sc = jnp.stack([sc.real, sc.imag], axis=0)
    sc = sc.reshape(2, tc, nlb, gl, p_, h_)
    m_state = jnp.einsum('rjbgpi,gm->bjgirmp', sc, eye).reshape(nlb, tc * LANES, 2 * gl * p_)
    oc = c[None] * pw[1:tc + 1][:, :, None, :]
    oc = jnp.stack([oc.real, -oc.imag], axis=0)
    oc = oc.reshape(2, tc, nlb, gl, h_, p_)
    m_out = jnp.einsum('rlbgop,gm->brgplmo', oc, eye).reshape(nlb, 2 * gl * p_, tc * LANES)
    a = pw[tc].reshape(nlb, 1, gl * p_)
    d_l = jnp.tile(d_skip.astype(F32).reshape(nlb, 1, LANES), (1, 1, tc))
    return (m_intra.astype(BF16), m_state.astype(BF16), m_out.astype(BF16),
            a.real.astype(F32), a.imag.astype(F32), d_l)


def _ssm_kernel(u_ref, mi_ref, ms_ref, mo_ref, are_ref, aim_ref, d_ref, y_ref, con_ref, sp_ref):
    nch = u_ref.shape[0] // SSM_TC
    ns = are_ref.shape[1]
    uf = jnp.concatenate([u_ref[pl.ds(j, nch, stride=SSM_TC), :] for j in range(SSM_TC)], axis=1)
    u = uf.astype(BF16)
    con_ref[...] = _dot(u, ms_ref[...])
    a_re = are_ref[...]
    a_im = aim_ref[...]

    def step(c, carry):
        s_re, s_im = carry
        sp_ref[pl.ds(c, 1), :ns] = s_re
        sp_ref[pl.ds(c, 1), ns:] = s_im
        c_re = con_ref[pl.ds(c, 1), :ns]
        c_im = con_ref[pl.ds(c, 1), ns:]
        return (a_re * s_re - a_im * s_im + c_re, a_re * s_im + a_im * s_re + c_im)

    zero = jnp.zeros((1, ns), F32)
    lax.fori_loop(0, nch, step, (zero, zero))
    y = _dot(u, mi_ref[...]) + _dot(sp_ref[...].astype(BF16), mo_ref[...]) + d_ref[...] * uf
    for l in range(SSM_TC):
        y_ref[pl.ds(l, nch, stride=SSM_TC), :] = y[:, l * LANES:(l + 1) * LANES]


def _ssm(u4, mats, bsz, seq):
    m_intra, m_state, m_out, a_re, a_im, d_l = mats
    t = u4.shape[1]
    nch = seq // SSM_TC
    wc = SSM_TC * LANES
    ns = SSM_GL * SSM_STATE
    wmap = lambda lb, b: (lb, 0, 0)
    return pl.pallas_call(
        _ssm_kernel,
        grid=(SSM_NLB, bsz),
        in_specs=[
            pl.BlockSpec((None, seq, LANES), lambda lb, b: (lb, b, 0)),
            pl.BlockSpec((None, wc, wc), wmap),
            pl.BlockSpec((None, wc, 2 * ns), wmap),
            pl.BlockSpec((None, 2 * ns, wc), wmap),
            pl.BlockSpec((None, 1, ns), wmap), pl.BlockSpec((None, 1, ns), wmap),
            pl.BlockSpec((None, 1, wc), wmap),
        ],
        out_specs=pl.BlockSpec((None, seq, LANES), lambda lb, b: (lb, b, 0)),
        out_shape=jax.ShapeDtypeStruct((SSM_NLB, t, LANES), F32),
        scratch_shapes=[pltpu.VMEM((nch, 2 * ns), F32), pltpu.VMEM((nch, 2 * ns), F32)],
        compiler_params=pltpu.CompilerParams(dimension_semantics=("arbitrary", "arbitrary"),
                                             vmem_limit_bytes=VMEM_LIMIT),
        name="ssm",
    )(u4, m_intra, m_state, m_out, a_re, a_im, d_l)


def _attn_kernel(q_ref, k_ref, v_ref, qi_ref, ki_ref, wi_ref, tri_ref, o_ref, keys_ref, k16_ref, qh_ref,
                 qm_ref, acc_ref, m_ref, *, tq, tk, seq, topk):
    i = pl.program_id(1)
    s0 = i * tq
    nk = (s0 + tq + tk - 1) // tk
    row = s0 + lax.broadcasted_iota(I32, (tq, 1), 0)
    lim = ((row >> 6) + 1) << 6
    lim_r = (((s0 + lax.broadcasted_iota(I32, (tq, LANES), 0)) >> 6) + 1) << 6
    kvec = jnp.minimum(lim_r, topk).astype(F32)
    col0 = lax.broadcasted_iota(I32, (tq, tk), 1)

    qi = qi_ref[...]
    lane_i = lax.broadcasted_iota(I32, qi.shape, 1)
    for h in range(IDX_HEADS):
        qh_ref[h] = jnp.where((lane_i >> 5) == h, qi, jnp.zeros_like(qi))
    wi = wi_ref[...]
    wcol = [wi[:, h:h + 1] for h in range(IDX_HEADS)]

    def score_tile(j, carry):
        kt = ki_ref[pl.ds(pl.multiple_of(j * tk, tk), tk), :]
        acc = jnp.zeros((tq, tk), F32)
        for h in range(IDX_HEADS):
            acc = acc + wcol[h] * jnp.maximum(_dot_nt(qh_ref[h], kt), 0.0)
        sc = jnp.where(col0 + j * tk < lim, acc, -jnp.inf)
        bits = pltpu.bitcast(sc, I32)
        key = bits ^ ((bits >> 31) & 0x7FFFFFFF)
        keys_ref[j] = key
        k16_ref[j] = (key >> 16).astype(jnp.int16)
        return carry

    lax.fori_loop(0, nk, score_tile, 0)

    def row_total(part):
        return jnp.broadcast_to(jnp.sum(part, axis=1, keepdims=True), (tq, LANES))

    one16 = jnp.ones((tq, LANES), jnp.int16)
    zero16 = jnp.zeros((tq, LANES), jnp.int16)

    def count16(cand16):
        def body(j, part):
            kk = k16_ref[j]
            for c in range(tk // LANES):
                part = part + jnp.where(kk[:, c * LANES:(c + 1) * LANES] >= cand16, one16, zero16)
            return part
        return row_total(lax.fori_loop(0, nk, body, jnp.zeros((tq, LANES), jnp.int16)).astype(F32))

    def bit_step(it, carry):
        u_ans, g_lo, g_hi = carry
        cand_u = u_ans | lax.shift_left(jnp.int32(1), 15 - it)
        cnt = count16((cand_u - 32768).astype(jnp.int16))
        ok = cnt >= kvec
        return jnp.where(ok, cand_u, u_ans), jnp.where(ok, cnt, g_lo), jnp.where(ok, g_hi, cnt)

    u_ans, g_lo, g_hi = lax.fori_loop(
        0, 16, bit_step, (jnp.zeros((tq, LANES), I32), lim_r.astype(F32), jnp.zeros((tq, LANES), F32)))
    b16 = u_ans - 32768

    def rebase(j, carry):
        kk = keys_ref[j]
        for c in range(tk // LANES):
            sl = slice(c * LANES, (c + 1) * LANES)
            hi16 = kk[:, sl] >> 16
            low = (kk[:, sl] & 0xFFFF) - 32768
            rel = jnp.where(hi16 > b16, 32767, jnp.where(hi16 < b16, -32768, low))
            k16_ref[j, :, sl] = rel.astype(jnp.int16)
        return carry

    lax.fori_loop(0, nk, rebase, 0)

    def unfinished(lo, hi, g_lo):
        return jnp.where(g_lo != kvec, jnp.where(hi - lo > 1, 1.0, 0.0), 0.0)

    def refine_cond(carry):
        return (carry[0] < 18) & (carry[1] > 0.0)

    def refine(carry):
        it, _, lo, hi, g_lo, g_hi = carry
        todo = unfinished(lo, hi, g_lo)
        mid = lo + ((hi - lo) >> 1)
        cand = jnp.where(b16 == 0, jnp.where(hi - lo == 65536, 1, mid), mid)
        cnt = count16((cand - 32768).astype(jnp.int16))
        up = jnp.where(cnt >= kvec, todo, 0.0) > 0.0
        dn = jnp.where(cnt < kvec, todo, 0.0) > 0.0
        lo = jnp.where(up, cand, lo)
        g_lo = jnp.where(up, cnt, g_lo)
        hi = jnp.where(dn, cand, hi)
        g_hi = jnp.where(dn, cnt, g_hi)
        return it + 1, jnp.max(unfinished(lo, hi, g_lo)), lo, hi, g_lo, g_hi

    off_lo = jnp.zeros((tq, LANES), I32)
    off_hi = jnp.full((tq, LANES), 65536, I32)
    _, _, off_lo, _, g_lo, g_hi = lax.while_loop(
        refine_cond, refine,
        (jnp.int32(0), jnp.max(unfinished(off_lo, off_hi, g_lo)), off_lo, off_hi, g_lo, g_hi))
    vth = (b16 << 16) + off_lo
    has_tie = jnp.max(g_lo - kvec) > 0.0

    @pl.when(jnp.logical_not(has_tie))
    def _():
        def to_bias(j, carry):
            kk = keys_ref[j]
            for c in range(tk // LANES):
                sl = slice(c * LANES, (c + 1) * LANES)
                keys_ref[j, :, sl] = pltpu.bitcast(jnp.where(kk[:, sl] >= vth, 0.0, NEG_BIG).astype(F32), I32)
            return carry
        lax.fori_loop(0, nk, to_bias, 0)

    @pl.when(has_tie)
    def _():
        need = kvec - g_hi

        def to_bias(j, seen):
            kk = keys_ref[j]
            eq = jnp.concatenate(
                [jnp.where(kk[:, c * LANES:(c + 1) * LANES] == vth, 1.0, 0.0) for c in range(tk // LANES)],
                axis=1)
            rank = _dot(eq.astype(BF16), tri_ref[...])
            for c in range(tk // LANES):
                sl = slice(c * LANES, (c + 1) * LANES)
                take = jnp.where(seen + rank[:, sl] <= need, eq[:, sl], 0.0)
                sel = jnp.where(kk[:, sl] > vth, 1.0, take)
                keys_ref[j, :, sl] = pltpu.bitcast(jnp.where(sel > 0.0, 0.0, NEG_BIG).astype(F32), I32)
            return seen + jnp.broadcast_to(rank[:, tk - 1:tk], (tq, LANES))
        lax.fori_loop(0, nk, to_bias, jnp.zeros((tq, LANES), F32))

    lane_q = lax.broadcasted_iota(I32, (tq, LANES), 1)
    lane_k = lax.broadcasted_iota(I32, (tk, LANES), 1)
    for h in range(N_HEADS):
        qp = q_ref[:, (h // 2) * LANES:(h // 2 + 1) * LANES]
        keep = (lane_q >= HEAD_DIM) if h % 2 else (lane_q < HEAD_DIM)
        qm_ref[h] = jnp.where(keep, qp, jnp.zeros_like(qp))
    acc_ref[...] = jnp.zeros_like(acc_ref)
    m_ref[...] = jnp.full(m_ref.shape, NEG_BIG, F32)
    ones_v = jnp.ones((tk, LANES), BF16)

    def attn_tile(j, carry):
        rows = pl.ds(pl.multiple_of(j * tk, tk), tk)
        bias = pltpu.bitcast(keys_ref[j], F32)
        for h in range(N_HEADS):
            cs = slice((h // 2) * LANES, (h // 2 + 1) * LANES)
            own = (lane_k >= HEAD_DIM) if h % 2 else (lane_k < HEAD_DIM)
            s = _dot_nt(qm_ref[h], k_ref[rows, cs]) + bias
            m_old = m_ref[h]
            m_new = jnp.maximum(m_old, jnp.max(s, axis=1, keepdims=True))
            p = jnp.concatenate(
                [jnp.exp2(s[:, c * LANES:(c + 1) * LANES] - m_new) for c in range(tk // LANES)],
                axis=1).astype(BF16)
            vh = jnp.where(own, v_ref[rows, cs], ones_v)
            acc_ref[h] = jnp.exp2(m_old - m_new) * acc_ref[h] + _dot(p, vh)
            m_ref[h] = m_new
        return carry

    lax.fori_loop(0, nk, attn_tile, 0)

    for hp in range(N_HEADS // 2):
        a0 = acc_ref[2 * hp]
        a1 = acc_ref[2 * hp + 1]
        out = jnp.where(lane_q < HEAD_DIM, a0 / pltpu.roll(a0, HEAD_DIM, 1), a1 / pltpu.roll(a1, HEAD_DIM, 1))
        o_ref[:, hp * LANES:(hp + 1) * LANES] = out.astype(o_ref.dtype)


def _attn(q, k, v, qi, ki8, wi, bsz, seq, tq, tk):
    t = q.shape[0]
    nq = seq // tq
    qrow = lambda b, i: (b * nq + i, 0)
    kv = lambda b, i: (b, 0)
    kern = functools.partial(_attn_kernel, tq=tq, tk=tk, seq=seq, topk=min(TOPK_MAX, seq // 4))
    tri = (jnp.arange(tk)[:, None] <= jnp.arange(tk)[None, :]).astype(BF16)
    return pl.pallas_call(
        kern,
        grid=(bsz, nq),
        in_specs=[
            pl.BlockSpec((tq, ATTN_WIDTH), qrow),
            pl.BlockSpec((seq, ATTN_WIDTH), kv), pl.BlockSpec((seq, ATTN_WIDTH), kv),
            pl.BlockSpec((tq, 256), qrow), pl.BlockSpec((seq, 256), kv),
            pl.BlockSpec((tq, LANES), qrow),
            pl.BlockSpec((tk, tk), lambda b, i: (0, 0)),
        ],
        out_specs=pl.BlockSpec((tq, ATTN_WIDTH), qrow),
        out_shape=jax.ShapeDtypeStruct((t, ATTN_WIDTH), BF16),
        scratch_shapes=[
            pltpu.VMEM((seq // tk, tq, tk), I32),
            pltpu.VMEM((seq // tk, tq, tk), jnp.int16),
            pltpu.VMEM((IDX_HEADS, tq, 256), BF16),
            pltpu.VMEM((N_HEADS, tq, LANES), BF16),
            pltpu.VMEM((N_HEADS, tq, LANES), F32),
            pltpu.VMEM((N_HEADS, tq, LANES), F32),
        ],
        compiler_params=pltpu.CompilerParams(dimension_semantics=("arbitrary", "arbitrary"),
                                             vmem_limit_bytes=VMEM_LIMIT),
        name="attn",
    )(q, k, v, qi, ki8, wi, tri)


def _mix_kernel(y4_ref, ya_ref, gs_ref, ga_ref, x_ref, wglu_ref, bglu_ref, wsb_ref, wab_ref, wo_ref,
                nf_ref, wrh_ref, wrl_ref, br_ref, r1_ref, h2_ref, route_ref):
    ys = jnp.concatenate([y4_ref[lb] for lb in range(SSM_NLB)], axis=1)
    ys = 0.5 * ys * (1.0 + jnp.tanh(math.sqrt(2.0 / math.pi) * (ys + 0.044715 * (ys * ys * ys))))
    ys = ys * jax.nn.sigmoid(_dot(ys.astype(BF16), wglu_ref[...]) + bglu_ref[...])
    br_s = _dot(ys.astype(BF16), wsb_ref[...])
    br_a = _dot(ya_ref[...], wab_ref[...])
    mix = gs_ref[...].astype(F32) * br_s + ga_ref[...].astype(F32) * br_a
    r1 = x_ref[...] + _dot(mix.astype(BF16), wo_ref[...])
    r1_ref[...] = r1
    h2 = _rms(r1, nf_ref[...])
    h2_ref[...] = h2
    hi = h2.astype(BF16)
    lo = (h2 - hi.astype(F32)).astype(BF16)
    logits = _dot(hi, wrh_ref[...]) + (_dot(hi, wrl_ref[...]) + _dot(lo, wrh_ref[...])) + br_ref[...]
    tm = logits.shape[0]
    lane = lax.broadcasted_iota(I32, (tm, LANES), 1).astype(F32)
    vals = logits
    tops, idxs = [], []
    for _ in range(TOP_K):
        m = jnp.max(vals, axis=1, keepdims=True)
        idx = jnp.min(jnp.where(vals == m, lane, float(LANES)), axis=1, keepdims=True)
        tops.append(m)
        idxs.append(idx)
        vals = jnp.where(lane == idx, -jnp.inf, vals)
    es = [jnp.exp(tv - tops[0]) for tv in tops]
    den = es[0] + es[1] + es[2] + es[3]
    route = jnp.zeros((tm, LANES), F32)
    for kk in range(TOP_K):
        route = jnp.where(lane == float(kk), idxs[kk], route)
        route = jnp.where(lane == float(TOP_K + kk), es[kk] / den, route)
    route_ref[...] = route


def _mix(y4, y_attn, gs, ga, x2, w_glu, b_glu, w_ssm_br, w_attn_br, w_o, norm_ffn, w_router, b_router, tm):
    t = x2.shape[0]
    wr = jnp.pad(w_router.astype(F32), ((0, 0), (0, LANES - N_EXPERTS)))
    wr_hi = wr.astype(BF16)
    wr_lo = (wr - wr_hi.astype(F32)).astype(BF16)
    br = jnp.concatenate([b_router.astype(F32), jnp.full((LANES - N_EXPERTS,), NEG_BIG, F32)]).reshape(1, LANES)
    row = lambda i: (i, 0)
    const = lambda i: (0, 0)
    return pl.pallas_call(
        _mix_kernel,
        grid=(t // tm,),
        in_specs=[
            pl.BlockSpec((SSM_NLB, tm, LANES), lambda i: (0, i, 0)),
            pl.BlockSpec((tm, ATTN_WIDTH), row),
            pl.BlockSpec((tm, D_MODEL), row), pl.BlockSpec((tm, D_MODEL), row), pl.BlockSpec((tm, D_MODEL), row),
            pl.BlockSpec((SSM_WIDTH, SSM_WIDTH), const), pl.BlockSpec((1, SSM_WIDTH), const),
            pl.BlockSpec((SSM_WIDTH, D_MODEL), const), pl.BlockSpec((ATTN_WIDTH, D_MODEL), const),
            pl.BlockSpec((D_MODEL, D_MODEL), const), pl.BlockSpec((1, D_MODEL), const),
            pl.BlockSpec((D_MODEL, LANES), const), pl.BlockSpec((D_MODEL, LANES), const),
            pl.BlockSpec((1, LANES), const),
        ],
        out_specs=[pl.BlockSpec((tm, D_MODEL), row), pl.BlockSpec((tm, D_MODEL), row),
                   pl.BlockSpec((tm, LANES), row)],
        out_shape=[jax.ShapeDtypeStruct((t, D_MODEL), F32), jax.ShapeDtypeStruct((t, D_MODEL), F32),
                   jax.ShapeDtypeStruct((t, LANES), F32)],
        compiler_params=pltpu.CompilerParams(dimension_semantics=("arbitrary",),
                                             vmem_limit_bytes=VMEM_LIMIT),
        name="mix",
    )(y4, y_attn, gs, ga, x2, w_glu.astype(BF16), b_glu.reshape(1, SSM_WIDTH).astype(F32),
      w_ssm_br.astype(BF16), w_attn_br.astype(BF16), w_o.astype(BF16), norm_ffn.reshape(1, D_MODEL),
      wr_hi, wr_lo, br)


def _prefix_kernel(route_ref, tri_ref, pre_ref, tot_ref, carry_ref):
    @pl.when(pl.program_id(0) == 0)
    def _():
        carry_ref[...] = jnp.zeros_like(carry_ref)

    route = route_ref[...]
    tm = route.shape[0]
    lane = lax.broadcasted_iota(I32, (tm, LANES), 1).astype(F32)
    mask = jnp.zeros((tm, LANES), F32)
    for kk in range(TOP_K):
        mask = jnp.where(lane == route[:, kk:kk + 1], 1.0, mask)
    incl = _dot(tri_ref[...], mask.astype(BF16))
    carry = carry_ref[...]
    pre_ref[...] = carry + incl - mask
    carry = carry + incl[tm - 1:tm, :]
    carry_ref[...] = carry
    tot_ref[...] = carry


def _prefix(route, tm):
    t = route.shape[0]
    tri = (jnp.arange(tm)[:, None] >= jnp.arange(tm)[None, :]).astype(BF16)
    return pl.pallas_call(
        _prefix_kernel,
        grid=(t // tm,),
        in_specs=[pl.BlockSpec((tm, LANES), lambda i: (i, 0)), pl.BlockSpec((tm, tm), lambda i: (0, 0))],
        out_specs=[pl.BlockSpec((tm, LANES), lambda i: (i, 0)), pl.BlockSpec((1, LANES), lambda i: (0, 0))],
        out_shape=[jax.ShapeDtypeStruct((t, LANES), F32), jax.ShapeDtypeStruct((1, LANES), F32)],
        scratch_shapes=[pltpu.VMEM((1, LANES), F32)],
        compiler_params=pltpu.CompilerParams(dimension_semantics=("arbitrary",)),
        name="prefix",
    )(route, tri)


def _scatter_kernel(zt_ref, dest_ref, h2_ref, xs_ref, zbuf_ref, sem, zsem):
    tm = h2_ref.shape[0]
    tmx = zbuf_ref.shape[0]

    @pl.when(pl.program_id(0) == 0)
    def _():
        zbuf_ref[...] = jnp.zeros_like(zbuf_ref)

        def zero_copy(z):
            return pltpu.make_async_copy(
                zbuf_ref, xs_ref.at[pl.ds(pl.multiple_of(zt_ref[z] * tmx, tmx), tmx)], zsem)

        for z in range(2 * N_EXPERTS):
            @pl.when(zt_ref[z] >= 0)
            def _():
                zero_copy(z).start()
        for z in range(2 * N_EXPERTS):
            @pl.when(zt_ref[z] >= 0)
            def _():
                zero_copy(z).wait()

    def row_copy(r, kk):
        d = dest_ref[0, r * TOP_K + kk]
        return pltpu.make_async_copy(h2_ref.at[pl.ds(r, 1)], xs_ref.at[pl.ds(d, 1)], sem)

    def start(r, carry):
        for kk in range(TOP_K):
            row_copy(r, kk).start(priority=kk % 2)
        return carry

    lax.fori_loop(0, tm, start, 0)

    def wait(r, carry):
        for kk in range(TOP_K):
            row_copy(r, kk).wait()
        return carry

    lax.fori_loop(0, tm, wait, 0)


def _scatter(zero_tiles, dest, h2, n_rows, tm, tmx):
    t = h2.shape[0]
    dest3 = dest.reshape(t // tm, 1, tm * TOP_K)
    grid_spec = pltpu.PrefetchScalarGridSpec(
        num_scalar_prefetch=1,
        grid=(t // tm,),
        in_specs=[
            pl.BlockSpec((None, 1, tm * TOP_K), lambda i, zt: (i, 0, 0), memory_space=pltpu.SMEM),
            pl.BlockSpec((tm, D_MODEL), lambda i, zt: (i, 0)),
        ],
        out_specs=pl.BlockSpec(memory_space=pl.ANY),
        scratch_shapes=[pltpu.VMEM((tmx, D_MODEL), F32), pltpu.SemaphoreType.DMA(()),
                        pltpu.SemaphoreType.DMA(())],
    )
    return pl.pallas_call(
        _scatter_kernel,
        grid_spec=grid_spec,
        out_shape=jax.ShapeDtypeStruct((n_rows, D_MODEL), F32),
        compiler_params=pltpu.CompilerParams(dimension_semantics=("arbitrary",), has_side_effects=True),
        name="scatter",
    )(zero_tiles, dest3, h2)


def _expert_kernel(te_ref, tv_ref, x_ref, wg_ref, bg_ref, wu_ref, bu_ref, wd_ref, bd_ref, y_ref,
                   wgb_ref, wub_ref, wdb_ref):
    i = pl.program_id(0)
    e = te_ref[i]
    prev = te_ref[jnp.maximum(i - 1, 0)]

    @pl.when((i == 0) | (e != prev))
    def _():
        wgb_ref[...] = wg_ref[...].astype(BF16)
        wub_ref[...] = wu_ref[...].astype(BF16)
        wdb_ref[...] = wd_ref[...].astype(BF16)

    @pl.when(tv_ref[i] != 0)
    def _():
        x = x_ref[...].astype(BF16)
        g = jnp.minimum(_dot(x, wgb_ref[...]) + bg_ref[...], SWIGLU_LIMIT)
        u = jnp.clip(_dot(x, wub_ref[...]) + bu_ref[...], -SWIGLU_LIMIT, SWIGLU_LIMIT)
        a = (u + 1.0) * (g * jax.nn.sigmoid(SWIGLU_ALPHA * g))
        y_ref[...] = _dot(a.astype(BF16), wdb_ref[...]) + bd_ref[...]

    @pl.when(tv_ref[i] == 0)
    def _():
        y_ref[...] = jnp.zeros_like(y_ref)


def _experts(tile_expert, tile_valid, xs, w_gate, b_gate, w_up, b_up, w_down, b_down, tmx):
    n_rows = xs.shape[0]
    wmap = lambda i, te, tv: (te[i], 0, 0)
    row = lambda i, te, tv: (i, 0)
    d_ff = w_gate.shape[2]
    grid_spec = pltpu.PrefetchScalarGridSpec(
        num_scalar_prefetch=2,
        grid=(n_rows // tmx,),
        in_specs=[
            pl.BlockSpec((tmx, D_MODEL), row),
            pl.BlockSpec((None, D_MODEL, d_ff), wmap), pl.BlockSpec((None, 1, d_ff), wmap),
            pl.BlockSpec((None, D_MODEL, d_ff), wmap), pl.BlockSpec((None, 1, d_ff), wmap),
            pl.BlockSpec((None, d_ff, D_MODEL), wmap), pl.BlockSpec((None, 1, D_MODEL), wmap),
        ],
        out_specs=pl.BlockSpec((tmx, D_MODEL), row),
        scratch_shapes=[pltpu.VMEM((D_MODEL, d_ff), BF16), pltpu.VMEM((D_MODEL, d_ff), BF16),
                        pltpu.VMEM((d_ff, D_MODEL), BF16)],
    )
    return pl.pallas_call(
        _expert_kernel,
        grid_spec=grid_spec,
        out_shape=jax.ShapeDtypeStruct((n_rows, D_MODEL), F32),
        compiler_params=pltpu.CompilerParams(dimension_semantics=("arbitrary",),
                                             vmem_limit_bytes=VMEM_LIMIT),
        name="experts",
    )(tile_expert, tile_valid, xs, w_gate, b_gate.reshape(N_EXPERTS, 1, d_ff), w_up,
      b_up.reshape(N_EXPERTS, 1, d_ff), w_down, b_down.reshape(N_EXPERTS, 1, D_MODEL))


def _combine_kernel(dest_ref, route_ref, r1_ref, p_ref, ys_ref, wpg_ref, wpp_ref, nfin_ref, o_ref,
                    buf_ref, sem):
    tm = r1_ref.shape[0]

    def row_copy(r, kk):
        d = dest_ref[0, r * TOP_K + kk]
        return pltpu.make_async_copy(ys_ref.at[pl.ds(d, 1)], buf_ref.at[kk, pl.ds(r, 1)], sem)

    def start(r, carry):
        for kk in range(TOP_K):
            row_copy(r, kk).start(priority=kk % 2)
        return carry

    lax.fori_loop(0, tm, start, 0)

    def wait(r, carry):
        for kk in range(TOP_K):
            row_copy(r, kk).wait()
        return carry

    lax.fori_loop(0, tm, wait, 0)

    route = route_ref[...]
    moe = route[:, TOP_K:TOP_K + 1] * buf_ref[0]
    for kk in range(1, TOP_K):
        moe = moe + route[:, TOP_K + kk:TOP_K + kk + 1] * buf_ref[kk]
    r2 = r1_ref[...] + moe
    gate = jax.nn.sigmoid(_dot(r2.astype(BF16), wpg_ref[...]))
    r3 = r2 + gate * _dot(p_ref[...].astype(BF16), wpp_ref[...])
    o_ref[...] = _rms(r3, nfin_ref[...])


def _combine(dest, route, r1, p2, ys, w_ple_gate, w_ple_proj, norm_final, tm):
    t = r1.shape[0]
    dest3 = dest.reshape(t // tm, 1, tm * TOP_K)
    row = lambda i: (i, 0)
    const = lambda i: (0, 0)
    return pl.pallas_call(
        _combine_kernel,
        grid=(t // tm,),
        in_specs=[
            pl.BlockSpec((None, 1, tm * TOP_K), lambda i: (i, 0, 0), memory_space=pltpu.SMEM),
            pl.BlockSpec((tm, LANES), row),
            pl.BlockSpec((tm, D_MODEL), row),
            pl.BlockSpec((tm, PLE_DIM), row),
            pl.BlockSpec(memory_space=pl.ANY),
            pl.BlockSpec((D_MODEL, D_MODEL), const), pl.BlockSpec((PLE_DIM, D_MODEL), const),
            pl.BlockSpec((1, D_MODEL), const),
        ],
        out_specs=pl.BlockSpec((tm, D_MODEL), row),
        out_shape=jax.ShapeDtypeStruct((t, D_MODEL), F32),
        scratch_shapes=[pltpu.VMEM((TOP_K, tm, D_MODEL), F32), pltpu.SemaphoreType.DMA(())],
        compiler_params=pltpu.CompilerParams(dimension_semantics=("arbitrary",),
                                             vmem_limit_bytes=VMEM_LIMIT),
        name="combine",
    )(dest3, route, r1, p2, ys, w_ple_gate.astype(BF16), w_ple_proj.astype(BF16),
      norm_final.reshape(1, D_MODEL))


def _moe(h2, route, r1, p2, w_gate, b_gate, w_up, b_up, w_down, b_down, w_ple_gate, w_ple_proj,
         norm_final, tm_prefix, tm_rows, tmx):
    t = h2.shape[0]
    pre, tot = _prefix(route, tm_prefix)
    counts = tot[0, :N_EXPERTS].astype(I32)
    padded = ((counts + tmx - 1) // tmx) * tmx
    ends = jnp.cumsum(padded)
    off = ends - padded
    n_tiles = (t * TOP_K) // tmx + N_EXPERTS
    tile_start = jnp.arange(n_tiles, dtype=I32) * tmx
    tile_valid = (tile_start < ends[-1]).astype(I32)
    tile_expert = jnp.minimum(jnp.sum((tile_start[:, None] >= ends[None, :]).astype(I32), axis=1), N_EXPERTS - 1)
    last_e = jnp.max(jnp.where(tile_valid != 0, tile_expert, 0))
    tile_expert = jnp.where(tile_valid != 0, tile_expert, last_e)
    top_idx = route[:, :TOP_K].astype(I32)
    dest = jnp.take_along_axis(pre[:, :N_EXPERTS].astype(I32) + off[None, :], top_idx, axis=1)
    used = ends[-1] // tmx
    last_tile = jnp.where(padded > 0, ends // tmx - 1, -1)
    spare = used + jnp.arange(N_EXPERTS, dtype=I32)
    zero_tiles = jnp.concatenate([last_tile, jnp.where(spare < n_tiles, spare, -1)]).astype(I32)
    xs = _scatter(zero_tiles, dest, h2, n_tiles * tmx, tm_rows, tmx)
    ys = _experts(tile_expert, tile_valid, xs, w_gate, b_gate, w_up, b_up, w_down, b_down, tmx)
    return _combine(dest, route, r1, p2, ys, w_ple_gate, w_ple_proj, norm_final, tm_rows)


def kernel(x, p, w_in, b_gates, lam_re, lam_im, log_dt, b_re, b_im, c_re, c_im, d_skip, w_glu, b_glu,
           w_ssm_br, w_attn_br, w_o, norm_mix, norm_ffn, w_router, b_router, w_gate, b_gate, w_up, b_up,
           w_down, b_down, w_ple_gate, w_ple_proj, norm_final):
    bsz, seq, _ = x.shape
    t = bsz * seq
    x2 = x.reshape(t, D_MODEL)
    u4, q, k, v, qi, ki8, wi, gs, ga = _inproj(x2, norm_mix[0], w_in[0], b_gates[0], seq, tm=256)
    mats = _ssm_mats(lam_re[0], lam_im[0], log_dt[0], b_re[0], b_im[0], c_re[0], c_im[0], d_skip[0])
    y4 = _ssm(u4, mats, bsz, seq)
    y_attn = _attn(q, k, v, qi, ki8, wi, bsz, seq, tq=256, tk=512)
    r1, h2, route = _mix(y4, y_attn, gs, ga, x2, w_glu[0], b_glu[0], w_ssm_br[0], w_attn_br[0], w_o[0],
                         norm_ffn[0], w_router[0], b_router[0], tm=256)
    out = _moe(h2, route, r1, p[0].reshape(t, PLE_DIM), w_gate[0], b_gate[0], w_up[0], b_up[0], w_down[0],
               b_down[0], w_ple_gate[0], w_ple_proj[0], norm_final, tm_prefix=512, tm_rows=256, tmx=512)
    return out.reshape(bsz, seq, D_MODEL)
```

```python
import functools
import math

import jax
import jax.numpy as jnp
from jax import lax
from jax.experimental import pallas as pl
from jax.experimental.pallas import tpu as pltpu

F32 = jnp.float32
BF16 = jnp.bfloat16
I32 = jnp.int32

D_MODEL = 1024
CHUNK = 64
PLE_DIM = 256
EPS = 1e-6
SSM_WIDTH = 512
SSM_GROUP = 16
SSM_GROUPS = 32
SSM_STATE = 64
N_HEADS = 8
HEAD_DIM = 64
ATTN_WIDTH = 512
IDX_HEADS = 8
IDX_DIM = 32
TOPK_MAX = 256
ROPE_THETA = 10000.0
N_EXPERTS = 32
TOP_K = 4
SWIGLU_LIMIT = 7.0
SWIGLU_ALPHA = 1.702

LANES = 128
SSM_TC = 16
SSM_GL = LANES // SSM_GROUP
SSM_NLB = SSM_WIDTH // LANES
NEG_BIG = -1e30
LOG2E = 1.4426950408889634
INT_MIN = -2147483648
VMEM_LIMIT = 56 * 1024 * 1024

_C_U, _C_Q, _C_K, _C_V, _C_QI, _C_KI, _C_WI, _C_GS, _C_GA = 0, 512, 1024, 1536, 2048, 2304, 2560, 2688, 3712
_C_END = 4736


def _rms(x, g):
    return x * lax.rsqrt(jnp.mean(x * x, axis=-1, keepdims=True) + EPS) * g


def _dot(a, b):
    return jnp.dot(a, b, preferred_element_type=F32)


def _dot_nt(a, b):
    return lax.dot_general(a, b, (((1,), (1,)), ((), ())), preferred_element_type=F32)


def _inproj_kernel(x_ref, g_ref, w_ref, bg_ref, cq_ref, sq_ref, ci_ref, si_ref,
                   u_ref, q_ref, k_ref, v_ref, qi_ref, ki_ref, wi_ref, gs_ref, ga_ref):
    h = _rms(x_ref[...], g_ref[...]).astype(BF16)

    def mm(c0, n):
        return _dot(h, w_ref[:, c0:c0 + n])

    u = mm(_C_U, 512)
    for lb in range(SSM_NLB):
        u_ref[lb] = u[:, lb * LANES:(lb + 1) * LANES]
    def rope(z, cos, sin, d):
        n = z.shape[1]
        lane = lax.broadcasted_iota(I32, z.shape, 1)
        partner = jnp.where((lane & (d - 1)) < d // 2, pltpu.roll(z, n - d // 2, 1), pltpu.roll(z, d // 2, 1))
        return z * cos + partner * sin

    cq = cq_ref[...]
    sq = sq_ref[...]
    q_ref[...] = (rope(mm(_C_Q, 512), cq, sq, HEAD_DIM) * (HEAD_DIM ** -0.5 * LOG2E)).astype(BF16)
    k_ref[...] = rope(mm(_C_K, 512), cq, sq, HEAD_DIM).astype(BF16)
    v_ref[...] = mm(_C_V, 512).astype(BF16)
    ci = ci_ref[...]
    si = si_ref[...]
    qi_ref[...] = rope(mm(_C_QI, 256), ci, si, IDX_DIM).astype(BF16)
    ki_ref[...] = rope(mm(_C_KI, 256), ci, si, IDX_DIM).astype(BF16)
    wi_ref[...] = mm(_C_WI, 128) * ((IDX_HEADS * IDX_DIM) ** -0.5)
    bg = bg_ref[...]
    gs_ref[...] = jax.nn.sigmoid(mm(_C_GS, 1024) + bg[:, :1024]).astype(BF16)
    ga_ref[...] = jax.nn.sigmoid(mm(_C_GA, 1024) + bg[:, 1024:]).astype(BF16)


def _rope_tables(seq, d, reps):
    half = d // 2
    inv = ROPE_THETA ** (-jnp.arange(half, dtype=F32) * 2.0 / d)
    ang = jnp.arange(seq, dtype=F32)[:, None] * inv[None, :]
    cos, sin = jnp.cos(ang), jnp.sin(ang)
    c = jnp.concatenate([cos, cos], axis=-1)
    s = jnp.concatenate([-sin, sin], axis=-1)
    return jnp.tile(c, (1, reps)), jnp.tile(s, (1, reps))


def _inproj(x2, norm_mix, w_in, b_gates, seq, tm):
    t = x2.shape[0]
    w_u, w_q, w_k, w_v, w_qi, w_ki, w_wi, w_gs, w_ga = jnp.split(
        w_in, [512, 1024, 1536, 2048, 2304, 2336, 2344, 3368], axis=1)
    w_ki8 = jnp.tile(w_ki, (1, IDX_HEADS))
    w_wi_p = jnp.pad(w_wi, ((0, 0), (0, LANES - IDX_HEADS)))
    w_all = jnp.concatenate([w_u, w_q, w_k, w_v, w_qi, w_ki8, w_wi_p, w_gs, w_ga], axis=1).astype(BF16)
    assert w_all.shape[1] == _C_END
    cq, sq = _rope_tables(seq, HEAD_DIM, N_HEADS)
    ci, si = _rope_tables(seq, IDX_DIM, IDX_HEADS)
    nt = seq // tm
    row = lambda i: (i, 0)
    pos = lambda i: (i % nt, 0)
    const = lambda i: (0, 0)
    outs = pl.pallas_call(
        _inproj_kernel,
        grid=(t // tm,),
        in_specs=[
            pl.BlockSpec((tm, D_MODEL), row),
            pl.BlockSpec((1, D_MODEL), const),
            pl.BlockSpec((D_MODEL, _C_END), const),
            pl.BlockSpec((1, 2 * D_MODEL), const),
            pl.BlockSpec((tm, 512), pos), pl.BlockSpec((tm, 512), pos),
            pl.BlockSpec((tm, 256), pos), pl.BlockSpec((tm, 256), pos),
        ],
        out_specs=[
            pl.BlockSpec((SSM_NLB, tm, LANES), lambda i: (0, i, 0)),
            pl.BlockSpec((tm, 512), row), pl.BlockSpec((tm, 512), row), pl.BlockSpec((tm, 512), row),
            pl.BlockSpec((tm, 256), row), pl.BlockSpec((tm, 256), row), pl.BlockSpec((tm, LANES), row),
            pl.BlockSpec((tm, D_MODEL), row), pl.BlockSpec((tm, D_MODEL), row),
        ],
        out_shape=[
            jax.ShapeDtypeStruct((SSM_NLB, t, LANES), F32),
            jax.ShapeDtypeStruct((t, 512), BF16), jax.ShapeDtypeStruct((t, 512), BF16),
            jax.ShapeDtypeStruct((t, 512), BF16),
            jax.ShapeDtypeStruct((t, 256), BF16), jax.ShapeDtypeStruct((t, 256), BF16),
            jax.ShapeDtypeStruct((t, LANES), F32),
            jax.ShapeDtypeStruct((t, D_MODEL), BF16), jax.ShapeDtypeStruct((t, D_MODEL), BF16),
        ],
        compiler_params=pltpu.CompilerParams(dimension_semantics=("arbitrary",),
                                             vmem_limit_bytes=VMEM_LIMIT),
        name="inproj",
    )(x2, norm_mix.reshape(1, D_MODEL), w_all, b_gates.reshape(1, 2 * D_MODEL), cq, sq, ci, si)
    return outs


def _ssm_mats(lam_re, lam_im, log_dt, b_re, b_im, c_re, c_im, d_skip):
    g_, p_, h_, tc, gl, nlb = SSM_GROUPS, SSM_STATE, SSM_GROUP, SSM_TC, SSM_GL, SSM_NLB
    lam = lax.complex(lam_re.astype(F32), lam_im.astype(F32))
    dt = jnp.exp(log_dt.astype(F32))[:, None]
    lam_dt = lam * dt
    lam_bar = jnp.exp(lam_dt)
    b_bar = ((lam_bar - 1.0) / lam)[..., None] * lax.complex(b_re.astype(F32), b_im.astype(F32))
    c = lax.complex(c_re.astype(F32), c_im.astype(F32))
    steps = jnp.arange(tc + 1, dtype=F32)
    pw = jnp.exp(lam_dt[None] * steps[:, None, None])
    hp = lax.Precision.HIGHEST
    lane_g = jnp.arange(LANES) // h_
    st_g = (jnp.arange(2 * gl * p_) % (gl * p_)) // p_
    rep_o = (jnp.arange(h_)[:, None] == (jnp.arange(LANES) % h_)[None, :]).astype(F32)
    st_col = (jnp.arange(2 * gl * p_) // (gl * p_)) * p_ + jnp.arange(2 * gl * p_) % p_
    rep_s = (jnp.arange(2 * p_)[:, None] == st_col[None, :]).astype(F32)
    taps = jnp.einsum('gop,tgp,gpi->gtoi', c, pw[:tc], b_bar).real
    a3 = taps.reshape(nlb, gl, tc, h_, h_).transpose(0, 2, 1, 4, 3).reshape(nlb, tc, LANES, h_)
    d = jnp.einsum('btro,oc->btrc', a3, rep_o, precision=hp)
    d = d * (lane_g[:, None] == lane_g[None, :]).astype(F32)
    dcat = d.transpose(0, 2, 1, 3).reshape(nlb, LANES, tc * LANES)
    m_intra = jnp.stack(
        [jnp.pad(dcat[:, :, :(tc - j) * LANES], ((0, 0), (0, 0), (j * LANES, 0))) for j in range(tc)],
        axis=1).reshape(nlb, tc * LANES, tc * LANES)
    sc = pw[:tc][::-1][:, :, :, None] * b_bar[None]
    sc = jnp.stack([sc.real, sc.imag], axis=0).reshape(2, tc, nlb, gl, p_, h_)
    a_s = sc.transpose(2, 1, 3, 5, 0, 4).reshape(nlb, tc, LANES, 2 * p_)
    m_state = jnp.einsum('bjrq,qc->bjrc', a_s, rep_s, precision=hp)
    m_state = (m_state * (lane_g[:, None] == st_g[None, :]).astype(F32)).reshape(nlb, tc * LANES, 2 * gl * p_)
    oc = c[None] * pw[1:tc + 1][:, :, None, :]
    oc = jnp.stack([oc.real, -oc.imag], axis=0).reshape(2, tc, nlb, gl, h_, p_)
    a_o = oc.transpose(2, 1, 0, 3, 5, 4).reshape(nlb, tc, 2 * gl * p_, h_)
    m_out = jnp.einsum('blro,oc->blrc', a_o, rep_o, precision=hp)
    m_out = m_out * (st_g[:, None] == lane_g[None, :]).astype(F32)
    m_out = m_out.transpose(0, 2, 1, 3).reshape(nlb, 2 * gl * p_, tc * LANES)
    a = pw[tc].reshape(nlb, 1, gl * p_)
    d_l = jnp.tile(d_skip.astype(F32).reshape(nlb, 1, LANES), (1, 1, tc))
    return (m_intra.astype(BF16), m_state.astype(BF16), m_out.astype(BF16),
            a.real.astype(F32), a.imag.astype(F32), d_l)


def _ssm_kernel(u_ref, mi_ref, ms_ref, mo_ref, are_ref, aim_ref, d_ref, y_ref, con_ref, sp_ref):
    nch = u_ref.shape[0] // SSM_TC
    ns = are_ref.shape[1]
    uf = jnp.concatenate([u_ref[pl.ds(j, nch, stride=SSM_TC), :] for j in range(SSM_TC)], axis=1)
    u = uf.astype(BF16)
    con_ref[...] = _dot(u, ms_ref[...])
    a_re = are_ref[...]
    a_im = aim_ref[...]

    def step(c, carry):
        s_re, s_im = carry
        sp_ref[pl.ds(c, 1), :ns] = s_re
        sp_ref[pl.ds(c, 1), ns:] = s_im
        c_re = con_ref[pl.ds(c, 1), :ns]
        c_im = con_ref[pl.ds(c, 1), ns:]
        return (a_re * s_re - a_im * s_im + c_re, a_re * s_im + a_im * s_re + c_im)

    zero = jnp.zeros((1, ns), F32)
    lax.fori_loop(0, nch, step, (zero, zero))
    y = _dot(u, mi_ref[...]) + _dot(sp_ref[...].astype(BF16), mo_ref[...]) + d_ref[...] * uf
    for l in range(SSM_TC):
        y_ref[pl.ds(l, nch, stride=SSM_TC), :] = y[:, l * LANES:(l + 1) * LANES]


def _ssm(u4, mats, bsz, seq):
    m_intra, m_state, m_out, a_re, a_im, d_l = mats
    t = u4.shape[1]
    nch = seq // SSM_TC
    wc = SSM_TC * LANES
    ns = SSM_GL * SSM_STATE
    wmap = lambda lb, b: (lb, 0, 0)
    return pl.pallas_call(
        _ssm_kernel,
        grid=(SSM_NLB, bsz),
        in_specs=[
            pl.BlockSpec((None, seq, LANES), lambda lb, b: (lb, b, 0)),
            pl.BlockSpec((None, wc, wc), wmap),
            pl.BlockSpec((None, wc, 2 * ns), wmap),
            pl.BlockSpec((None, 2 * ns, wc), wmap),
            pl.BlockSpec((None, 1, ns), wmap), pl.BlockSpec((None, 1, ns), wmap),
            pl.BlockSpec((None, 1, wc), wmap),
        ],
        out_specs=pl.BlockSpec((None, seq, LANES), lambda lb, b: (lb, b, 0)),
        out_shape=jax.ShapeDtypeStruct((SSM_NLB, t, LANES), F32),
        scratch_shapes=[pltpu.VMEM((nch, 2 * ns), F32), pltpu.VMEM((nch, 2 * ns), F32)],
        compiler_params=pltpu.CompilerParams(dimension_semantics=("arbitrary", "arbitrary"),
                                             vmem_limit_bytes=VMEM_LIMIT),
        name="ssm",
    )(u4, m_intra, m_state, m_out, a_re, a_im, d_l)


def _attn_kernel(q_ref, k_ref, v_ref, qi_ref, ki_ref, wi_ref, tri_ref, o_ref, keys_ref, kt_ref, k16_ref,
                 qh_ref, qm_ref, acc_ref, m_ref, *, tq, tk, seq, topk):
    i = pl.program_id(1)
    s0 = i * tq
    nk = (s0 + tq + tk - 1) // tk
    row = s0 + lax.broadcasted_iota(I32, (tq, 1), 0)
    lim = ((row >> 6) + 1) << 6
    lim_q = (((s0 + lax.broadcasted_iota(I32, (1, tq), 1)) >> 6) + 1) << 6
    kvec = jnp.minimum(lim_q, topk).astype(F32)
    col0 = lax.broadcasted_iota(I32, (tq, tk), 1)

    qi = qi_ref[...]
    lane_i = lax.broadcasted_iota(I32, qi.shape, 1)
    for h in range(IDX_HEADS):
        qh_ref[h] = jnp.where((lane_i >> 5) == h, qi, jnp.zeros_like(qi))
    wi = wi_ref[...]
    wcol = [wi[:, h:h + 1] for h in range(IDX_HEADS)]

    def score_tile(j, carry):
        kt = ki_ref[pl.ds(pl.multiple_of(j * tk, tk), tk), :]
        acc = jnp.zeros((tq, tk), F32)
        for h in range(IDX_HEADS):
            acc = acc + wcol[h] * jnp.maximum(_dot_nt(qh_ref[h], kt), 0.0)
        sc = jnp.where(col0 + j * tk < lim, acc, -jnp.inf)
        bits = pltpu.bitcast(sc, I32)
        key = bits ^ ((bits >> 31) & 0x7FFFFFFF)
        keys_ref[j] = key
        key_t = key.T
        kt_ref[j] = key_t
        k16_ref[j] = (key_t >> 16).astype(jnp.int16)
        return carry

    lax.fori_loop(0, nk, score_tile, 0)

    rg = 16
    one16 = jnp.ones((rg, tq), jnp.int16)
    zero16 = jnp.zeros((rg, tq), jnp.int16)

    def count16(cand):
        cand16 = jnp.broadcast_to(cand, (rg, tq)).astype(jnp.int16)

        def body(j, part):
            kk = k16_ref[j]
            for r in range(tk // rg):
                part = part + jnp.where(kk[r * rg:(r + 1) * rg, :] >= cand16, one16, zero16)
            return part
        part = lax.fori_loop(0, nk, body, jnp.zeros((rg, tq), jnp.int16))
        return jnp.sum(part.astype(F32), axis=0, keepdims=True)

    def bit_step(it, carry):
        u_ans, g_lo, g_hi = carry
        cand_u = u_ans | lax.shift_left(jnp.int32(1), 15 - it)
        cnt = count16(cand_u - 32768)
        ok = cnt >= kvec
        return jnp.where(ok, cand_u, u_ans), jnp.where(ok, cnt, g_lo), jnp.where(ok, g_hi, cnt)

    u_ans, g_lo, g_hi = lax.fori_loop(
        0, 16, bit_step, (jnp.zeros((1, tq), I32), lim_q.astype(F32), jnp.zeros((1, tq), F32)))
    b16 = u_ans - 32768

    def rebase(j, carry):
        kk = kt_ref[j]
        hi16 = kk >> 16
        low = (kk & 0xFFFF) - 32768
        rel = jnp.where(hi16 > b16, 32767, jnp.where(hi16 < b16, -32768, low))
        k16_ref[j] = rel.astype(jnp.int16)
        return carry

    lax.fori_loop(0, nk, rebase, 0)

    def unfinished(lo, hi, g_lo):
        return jnp.where(g_lo != kvec, jnp.where(hi - lo > 1, 1.0, 0.0), 0.0)

    def refine_cond(carry):
        return (carry[0] < 18) & (carry[1] > 0.0)

    def refine(carry):
        it, _, lo, hi, g_lo, g_hi = carry
        todo = unfinished(lo, hi, g_lo)
        mid = lo + ((hi - lo) >> 1)
        cand = jnp.where(b16 == 0, jnp.where(hi - lo == 65536, 1, mid), mid)
        cnt = count16(cand - 32768)
        up = jnp.where(cnt >= kvec, todo, 0.0) > 0.0
        dn = jnp.where(cnt < kvec, todo, 0.0) > 0.0
        lo = jnp.where(up, cand, lo)
        g_lo = jnp.where(up, cnt, g_lo)
        hi = jnp.where(dn, cand, hi)
        g_hi = jnp.where(dn, cnt, g_hi)
        return it + 1, jnp.max(unfinished(lo, hi, g_lo)), lo, hi, g_lo, g_hi

    off_lo = jnp.zeros((1, tq), I32)
    off_hi = jnp.full((1, tq), 65536, I32)
    _, _, off_lo, _, g_lo, g_hi = lax.while_loop(
        refine_cond, refine,
        (jnp.int32(0), jnp.max(unfinished(off_lo, off_hi, g_lo)), off_lo, off_hi, g_lo, g_hi))
    has_tie = jnp.max(g_lo - kvec) > 0.0

    def to_rows(x):
        return jnp.transpose(jnp.broadcast_to(x, (LANES, tq)))

    vth = to_rows((b16 << 16) + off_lo)

    @pl.when(jnp.logical_not(has_tie))
    def _():
        def to_bias(j, carry):
            kk = keys_ref[j]
            for c in range(tk // LANES):
                sl = slice(c * LANES, (c + 1) * LANES)
                keys_ref[j, :, sl] = pltpu.bitcast(jnp.where(kk[:, sl] >= vth, 0.0, NEG_BIG).astype(F32), I32)
            return carry
        lax.fori_loop(0, nk, to_bias, 0)

    @pl.when(has_tie)
    def _():
        need = to_rows(kvec - g_hi)

        def to_bias(j, seen):
            kk = keys_ref[j]
            eq = jnp.concatenate(
                [jnp.where(kk[:, c * LANES:(c + 1) * LANES] == vth, 1.0, 0.0) for c in range(tk // LANES)],
                axis=1)
            rank = _dot(eq.astype(BF16), tri_ref[...])
            for c in range(tk // LANES):
                sl = slice(c * LANES, (c + 1) * LANES)
                take = jnp.where(seen + rank[:, sl] <= need, eq[:, sl], 0.0)
                sel = jnp.where(kk[:, sl] > vth, 1.0, take)
                keys_ref[j, :, sl] = pltpu.bitcast(jnp.where(sel > 0.0, 0.0, NEG_BIG).astype(F32), I32)
            return seen + jnp.broadcast_to(rank[:, tk - 1:tk], (tq, LANES))
        lax.fori_loop(0, nk, to_bias, jnp.zeros((tq, LANES), F32))

    lane_q = lax.broadcasted_iota(I32, (tq, LANES), 1)
    lane_k = lax.broadcasted_iota(I32, (tk, LANES), 1)
    for h in range(N_HEADS):
        qp = q_ref[:, (h // 2) * LANES:(h // 2 + 1) * LANES]
        keep = (lane_q >= HEAD_DIM) if h % 2 else (lane_q < HEAD_DIM)
        qm_ref[h] = jnp.where(keep, qp, jnp.zeros_like(qp))
    acc_ref[...] = jnp.zeros_like(acc_ref)
    m_ref[...] = jnp.full(m_ref.shape, NEG_BIG, F32)
    ones_v = jnp.ones((tk, LANES), BF16)

    def attn_tile(j, carry):
        rows = pl.ds(pl.multiple_of(j * tk, tk), tk)
        bias = pltpu.bitcast(keys_ref[j], F32)
        for h in range(N_HEADS):
            cs = slice((h // 2) * LANES, (h // 2 + 1) * LANES)
            own = (lane_k >= HEAD_DIM) if h % 2 else (lane_k < HEAD_DIM)
            s = _dot_nt(qm_ref[h], k_ref[rows, cs]) + bias
            m_old = m_ref[h]
            m_new = jnp.maximum(m_old, jnp.max(s, axis=1, keepdims=True))
            p = jnp.concatenate(
                [jnp.exp2(s[:, c * LANES:(c + 1) * LANES] - m_new) for c in range(tk // LANES)],
                axis=1).astype(BF16)
            vh = jnp.where(own, v_ref[rows, cs], ones_v)
            acc_ref[h] = jnp.exp2(m_old - m_new) * acc_ref[h] + _dot(p, vh)
            m_ref[h] = m_new
        return carry

    lax.fori_loop(0, nk, attn_tile, 0)

    for hp in range(N_HEADS // 2):
        a0 = acc_ref[2 * hp]
        a1 = acc_ref[2 * hp + 1]
        out = jnp.where(lane_q < HEAD_DIM, a0 / pltpu.roll(a0, HEAD_DIM, 1), a1 / pltpu.roll(a1, HEAD_DIM, 1))
        o_ref[:, hp * LANES:(hp + 1) * LANES] = out.astype(o_ref.dtype)


def _attn(q, k, v, qi, ki8, wi, bsz, seq, tq, tk):
    t = q.shape[0]
    nq = seq // tq
    qrow = lambda b, i: (b * nq + i, 0)
    kv = lambda b, i: (b, 0)
    kern = functools.partial(_attn_kernel, tq=tq, tk=tk, seq=seq, topk=min(TOPK_MAX, seq // 4))
    tri = (jnp.arange(tk)[:, None] <= jnp.arange(tk)[None, :]).astype(BF16)
    return pl.pallas_call(
        kern,
        grid=(bsz, nq),
        in_specs=[
            pl.BlockSpec((tq, ATTN_WIDTH), qrow),
            pl.BlockSpec((seq, ATTN_WIDTH), kv), pl.BlockSpec((seq, ATTN_WIDTH), kv),
            pl.BlockSpec((tq, 256), qrow), pl.BlockSpec((seq, 256), kv),
            pl.BlockSpec((tq, LANES), qrow),
            pl.BlockSpec((tk, tk), lambda b, i: (0, 0)),
        ],
        out_specs=pl.BlockSpec((tq, ATTN_WIDTH), qrow),
        out_shape=jax.ShapeDtypeStruct((t, ATTN_WIDTH), BF16),
        scratch_shapes=[
            pltpu.VMEM((seq // tk, tq, tk), I32),
            pltpu.VMEM((seq // tk, tk, tq), I32),
            pltpu.VMEM((seq // tk, tk, tq), jnp.int16),
            pltpu.VMEM((IDX_HEADS, tq, 256), BF16),
            pltpu.VMEM((N_HEADS, tq, LANES), BF16),
            pltpu.VMEM((N_HEADS, tq, LANES), F32),
            pltpu.VMEM((N_HEADS, tq, LANES), F32),
        ],
        compiler_params=pltpu.CompilerParams(dimension_semantics=("arbitrary", "arbitrary"),
                                             vmem_limit_bytes=VMEM_LIMIT),
        name="attn",
    )(q, k, v, qi, ki8, wi, tri)


def _mix_kernel(y4_ref, ya_ref, gs_ref, ga_ref, x_ref, wglu_ref, bglu_ref, wsb_ref, wab_ref, wo_ref,
                nf_ref, wrh_ref, wrl_ref, br_ref, r1_ref, h2_ref, route_ref):
    ys = jnp.concatenate([y4_ref[lb] for lb in range(SSM_NLB)], axis=1)
    ys = 0.5 * ys * (1.0 + jnp.tanh(math.sqrt(2.0 / math.pi) * (ys + 0.044715 * (ys * ys * ys))))
    ys = ys * jax.nn.sigmoid(_dot(ys.astype(BF16), wglu_ref[...]) + bglu_ref[...])
    br_s = _dot(ys.astype(BF16), wsb_ref[...])
    br_a = _dot(ya_ref[...], wab_ref[...])
    mix = gs_ref[...].astype(F32) * br_s + ga_ref[...].astype(F32) * br_a
    r1 = x_ref[...] + _dot(mix.astype(BF16), wo_ref[...])
    r1_ref[...] = r1
    h2 = _rms(r1, nf_ref[...])
    h2_ref[...] = h2
    hi = h2.astype(BF16)
    lo = (h2 - hi.astype(F32)).astype(BF16)
    logits = _dot(hi, wrh_ref[...]) + (_dot(hi, wrl_ref[...]) + _dot(lo, wrh_ref[...])) + br_ref[...]
    tm = logits.shape[0]
    lane = lax.broadcasted_iota(I32, (tm, LANES), 1).astype(F32)
    vals = logits
    tops, idxs = [], []
    for _ in range(TOP_K):
        m = jnp.max(vals, axis=1, keepdims=True)
        idx = jnp.min(jnp.where(vals == m, lane, float(LANES)), axis=1, keepdims=True)
        tops.append(m)
        idxs.append(idx)
        vals = jnp.where(lane == idx, -jnp.inf, vals)
    es = [jnp.exp(tv - tops[0]) for tv in tops]
    den = es[0] + es[1] + es[2] + es[3]
    route = jnp.zeros((tm, LANES), F32)
    for kk in range(TOP_K):
        route = jnp.where(lane == float(kk), idxs[kk], route)
        route = jnp.where(lane == float(TOP_K + kk), es[kk] / den, route)
    route_ref[...] = route


def _mix(y4, y_attn, gs, ga, x2, w_glu, b_glu, w_ssm_br, w_attn_br, w_o, norm_ffn, w_router, b_router, tm):
    t = x2.shape[0]
    wr = jnp.pad(w_router.astype(F32), ((0, 0), (0, LANES - N_EXPERTS)))
    wr_hi = wr.astype(BF16)
    wr_lo = (wr - wr_hi.astype(F32)).astype(BF16)
    br = jnp.concatenate([b_router.astype(F32), jnp.full((LANES - N_EXPERTS,), NEG_BIG, F32)]).reshape(1, LANES)
    row = lambda i: (i, 0)
    const = lambda i: (0, 0)
    return pl.pallas_call(
        _mix_kernel,
        grid=(t // tm,),
        in_specs=[
            pl.BlockSpec((SSM_NLB, tm, LANES), lambda i: (0, i, 0)),
            pl.BlockSpec((tm, ATTN_WIDTH), row),
            pl.BlockSpec((tm, D_MODEL), row), pl.BlockSpec((tm, D_MODEL), row), pl.BlockSpec((tm, D_MODEL), row),
            pl.BlockSpec((SSM_WIDTH, SSM_WIDTH), const), pl.BlockSpec((1, SSM_WIDTH), const),
            pl.BlockSpec((SSM_WIDTH, D_MODEL), const), pl.BlockSpec((ATTN_WIDTH, D_MODEL), const),
            pl.BlockSpec((D_MODEL, D_MODEL), const), pl.BlockSpec((1, D_MODEL), const),
            pl.BlockSpec((D_MODEL, LANES), const), pl.BlockSpec((D_MODEL, LANES), const),
            pl.BlockSpec((1, LANES), const),
        ],
        out_specs=[pl.BlockSpec((tm, D_MODEL), row), pl.BlockSpec((tm, D_MODEL), row),
                   pl.BlockSpec((tm, LANES), row)],
        out_shape=[jax.ShapeDtypeStruct((t, D_MODEL), F32), jax.ShapeDtypeStruct((t, D_MODEL), F32),
                   jax.ShapeDtypeStruct((t, LANES), F32)],
        compiler_params=pltpu.CompilerParams(dimension_semantics=("arbitrary",),
                                             vmem_limit_bytes=VMEM_LIMIT),
        name="mix",
    )(y4, y_attn, gs, ga, x2, w_glu.astype(BF16), b_glu.reshape(1, SSM_WIDTH).astype(F32),
      w_ssm_br.astype(BF16), w_attn_br.astype(BF16), w_o.astype(BF16), norm_ffn.reshape(1, D_MODEL),
      wr_hi, wr_lo, br)


def _prefix_kernel(route_ref, tri_ref, pre_ref, tot_ref, carry_ref):
    @pl.when(pl.program_id(0) == 0)
    def _():
        carry_ref[...] = jnp.zeros_like(carry_ref)

    route = route_ref[...]
    tm = route.shape[0]
    lane = lax.broadcasted_iota(I32, (tm, LANES), 1).astype(F32)
    mask = jnp.zeros((tm, LANES), F32)
    for kk in range(TOP_K):
        mask = jnp.where(lane == route[:, kk:kk + 1], 1.0, mask)
    incl = _dot(tri_ref[...], mask.astype(BF16))
    carry = carry_ref[...]
    pre_ref[...] = carry + incl - mask
    carry = carry + incl[tm - 1:tm, :]
    carry_ref[...] = carry
    tot_ref[...] = carry


def _prefix(route, tm):
    t = route.shape[0]
    tri = (jnp.arange(tm)[:, None] >= jnp.arange(tm)[None, :]).astype(BF16)
    return pl.pallas_call(
        _prefix_kernel,
        grid=(t // tm,),
        in_specs=[pl.BlockSpec((tm, LANES), lambda i: (i, 0)), pl.BlockSpec((tm, tm), lambda i: (0, 0))],
        out_specs=[pl.BlockSpec((tm, LANES), lambda i: (i, 0)), pl.BlockSpec((1, LANES), lambda i: (0, 0))],
        out_shape=[jax.ShapeDtypeStruct((t, LANES), F32), jax.ShapeDtypeStruct((1, LANES), F32)],
        scratch_shapes=[pltpu.VMEM((1, LANES), F32)],
        compiler_params=pltpu.CompilerParams(dimension_semantics=("arbitrary",)),
        name="prefix",
    )(route, tri)


def _scatter_kernel(zt_ref, dest_ref, h2_ref, xs_ref, zbuf_ref, sem, zsem):
    tm = h2_ref.shape[0]
    tmx = zbuf_ref.shape[0]

    @pl.when(pl.program_id(0) == 0)
    def _():
        zbuf_ref[...] = jnp.zeros_like(zbuf_ref)

        def zero_copy(z):
            return pltpu.make_async_copy(
                zbuf_ref, xs_ref.at[pl.ds(pl.multiple_of(zt_ref[z] * tmx, tmx), tmx)], zsem)

        for z in range(2 * N_EXPERTS):
            @pl.when(zt_ref[z] >= 0)
            def _():
                zero_copy(z).start()
        for z in range(2 * N_EXPERTS):
            @pl.when(zt_ref[z] >= 0)
            def _():
                zero_copy(z).wait()

    def row_copy(r, kk):
        d = dest_ref[0, r * TOP_K + kk]
        return pltpu.make_async_copy(h2_ref.at[pl.ds(r, 1)], xs_ref.at[pl.ds(d, 1)], sem)

    def start(r, carry):
        for kk in range(TOP_K):
            row_copy(r, kk).start(priority=kk % 2)
        return carry

    lax.fori_loop(0, tm, start, 0)

    def wait(r, carry):
        for kk in range(TOP_K):
            row_copy(r, kk).wait()
        return carry

    lax.fori_loop(0, tm, wait, 0)


def _scatter(zero_tiles, dest, h2, n_rows, tm, tmx):
    t = h2.shape[0]
    dest3 = dest.reshape(t // tm, 1, tm * TOP_K)
    grid_spec = pltpu.PrefetchScalarGridSpec(
        num_scalar_prefetch=1,
        grid=(t // tm,),
        in_specs=[
            pl.BlockSpec((None, 1, tm * TOP_K), lambda i, zt: (i, 0, 0), memory_space=pltpu.SMEM),
            pl.BlockSpec((tm, D_MODEL), lambda i, zt: (i, 0)),
        ],
        out_specs=pl.BlockSpec(memory_space=pl.ANY),
        scratch_shapes=[pltpu.VMEM((tmx, D_MODEL), F32), pltpu.SemaphoreType.DMA(()),
                        pltpu.SemaphoreType.DMA(())],
    )
    return pl.pallas_call(
        _scatter_kernel,
        grid_spec=grid_spec,
        out_shape=jax.ShapeDtypeStruct((n_rows, D_MODEL), F32),
        compiler_params=pltpu.CompilerParams(dimension_semantics=("arbitrary",), has_side_effects=True),
        name="scatter",
    )(zero_tiles, dest3, h2)


def _expert_kernel(te_ref, tv_ref, x_ref, wg_ref, bg_ref, wu_ref, bu_ref, wd_ref, bd_ref, y_ref,
                   wgb_ref, wub_ref, wdb_ref):
    i = pl.program_id(0)
    e = te_ref[i]
    prev = te_ref[jnp.maximum(i - 1, 0)]

    @pl.when((i == 0) | (e != prev))
    def _():
        wgb_ref[...] = wg_ref[...].astype(BF16)
        wub_ref[...] = wu_ref[...].astype(BF16)
        wdb_ref[...] = wd_ref[...].astype(BF16)

    @pl.when(tv_ref[i] != 0)
    def _():
        x = x_ref[...].astype(BF16)
        g = jnp.minimum(_dot(x, wgb_ref[...]) + bg_ref[...], SWIGLU_LIMIT)
        u = jnp.clip(_dot(x, wub_ref[...]) + bu_ref[...], -SWIGLU_LIMIT, SWIGLU_LIMIT)
        a = (u + 1.0) * (g * jax.nn.sigmoid(SWIGLU_ALPHA * g))
        y_ref[...] = _dot(a.astype(BF16), wdb_ref[...]) + bd_ref[...]

    @pl.when(tv_ref[i] == 0)
    def _():
        y_ref[...] = jnp.zeros_like(y_ref)


def _experts(tile_expert, tile_valid, xs, w_gate, b_gate, w_up, b_up, w_down, b_down, tmx):
    n_rows = xs.shape[0]
    wmap = lambda i, te, tv: (te[i], 0, 0)
    row = lambda i, te, tv: (i, 0)
    d_ff = w_gate.shape[2]
    grid_spec = pltpu.PrefetchScalarGridSpec(
        num_scalar_prefetch=2,
        grid=(n_rows // tmx,),
        in_specs=[
            pl.BlockSpec((tmx, D_MODEL), row),
            pl.BlockSpec((None, D_MODEL, d_ff), wmap), pl.BlockSpec((None, 1, d_ff), wmap),
            pl.BlockSpec((None, D_MODEL, d_ff), wmap), pl.BlockSpec((None, 1, d_ff), wmap),
            pl.BlockSpec((None, d_ff, D_MODEL), wmap), pl.BlockSpec((None, 1, D_MODEL), wmap),
        ],
        out_specs=pl.BlockSpec((tmx, D_MODEL), row),
        scratch_shapes=[pltpu.VMEM((D_MODEL, d_ff), BF16), pltpu.VMEM((D_MODEL, d_ff), BF16),
                        pltpu.VMEM((d_ff, D_MODEL), BF16)],
    )
    return pl.pallas_call(
        _expert_kernel,
        grid_spec=grid_spec,
        out_shape=jax.ShapeDtypeStruct((n_rows, D_MODEL), F32),
        compiler_params=pltpu.CompilerParams(dimension_semantics=("arbitrary",),
                                             vmem_limit_bytes=VMEM_LIMIT),
        name="experts",
    )(tile_expert, tile_valid, xs, w_gate, b_gate.reshape(N_EXPERTS, 1, d_ff), w_up,
      b_up.reshape(N_EXPERTS, 1, d_ff), w_down, b_down.reshape(N_EXPERTS, 1, D_MODEL))


def _combine_kernel(dest_ref, route_ref, r1_ref, p_ref, ys_ref, wpg_ref, wpp_ref, nfin_ref, o_ref,
                    buf_ref, sem):
    tm = r1_ref.shape[0]

    def row_copy(r, kk):
        d = dest_ref[0, r * TOP_K + kk]
        return pltpu.make_async_copy(ys_ref.at[pl.ds(d, 1)], buf_ref.at[kk, pl.ds(r, 1)], sem)

    def start(r, carry):
        for kk in range(TOP_K):
            row_copy(r, kk).start(priority=kk % 2)
        return carry

    lax.fori_loop(0, tm, start, 0)

    def wait(r, carry):
        for kk in range(TOP_K):
            row_copy(r, kk).wait()
        return carry

    lax.fori_loop(0, tm, wait, 0)

    route = route_ref[...]
    moe = route[:, TOP_K:TOP_K + 1] * buf_ref[0]
    for kk in range(1, TOP_K):
        moe = moe + route[:, TOP_K + kk:TOP_K + kk + 1] * buf_ref[kk]
    r2 = r1_ref[...] + moe
    gate = jax.nn.sigmoid(_dot(r2.astype(BF16), wpg_ref[...]))
    r3 = r2 + gate * _dot(p_ref[...].astype(BF16), wpp_ref[...])
    o_ref[...] = _rms(r3, nfin_ref[...])


def _combine(dest, route, r1, p2, ys, w_ple_gate, w_ple_proj, norm_final, tm):
    t = r1.shape[0]
    dest3 = dest.reshape(t // tm, 1, tm * TOP_K)
    row = lambda i: (i, 0)
    const = lambda i: (0, 0)
    return pl.pallas_call(
        _combine_kernel,
        grid=(t // tm,),
        in_specs=[
            pl.BlockSpec((None, 1, tm * TOP_K), lambda i: (i, 0, 0), memory_space=pltpu.SMEM),
            pl.BlockSpec((tm, LANES), row),
            pl.BlockSpec((tm, D_MODEL), row),
            pl.BlockSpec((tm, PLE_DIM), row),
            pl.BlockSpec(memory_space=pl.ANY),
            pl.BlockSpec((D_MODEL, D_MODEL), const), pl.BlockSpec((PLE_DIM, D_MODEL), const),
            pl.BlockSpec((1, D_MODEL), const),
        ],
        out_specs=pl.BlockSpec((tm, D_MODEL), row),
        out_shape=jax.ShapeDtypeStruct((t, D_MODEL), F32),
        scratch_shapes=[pltpu.VMEM((TOP_K, tm, D_MODEL), F32), pltpu.SemaphoreType.DMA(())],
        compiler_params=pltpu.CompilerParams(dimension_semantics=("arbitrary",),
                                             vmem_limit_bytes=VMEM_LIMIT),
        name="combine",
    )(dest3, route, r1, p2, ys, w_ple_gate.astype(BF16), w_ple_proj.astype(BF16),
      norm_final.reshape(1, D_MODEL))


def _moe(h2, route, r1, p2, w_gate, b_gate, w_up, b_up, w_down, b_down, w_ple_gate, w_ple_proj,
         norm_final, tm_prefix, tm_rows, tmx):
    t = h2.shape[0]
    pre, tot = _prefix(route, tm_prefix)
    counts = tot[0, :N_EXPERTS].astype(I32)
    padded = ((counts + tmx - 1) // tmx) * tmx
    ends = jnp.cumsum(padded)
    off = ends - padded
    n_tiles = (t * TOP_K) // tmx + N_EXPERTS
    tile_start = jnp.arange(n_tiles, dtype=I32) * tmx
    tile_valid = (tile_start < ends[-1]).astype(I32)
    tile_expert = jnp.minimum(jnp.sum((tile_start[:, None] >= ends[None, :]).astype(I32), axis=1), N_EXPERTS - 1)
    last_e = jnp.max(jnp.where(tile_valid != 0, tile_expert, 0))
    tile_expert = jnp.where(tile_valid != 0, tile_expert, last_e)
    top_idx = route[:, :TOP_K].astype(I32)
    dest = jnp.take_along_axis(pre[:, :N_EXPERTS].astype(I32) + off[None, :], top_idx, axis=1)
    used = ends[-1] // tmx
    last_tile = jnp.where(padded > 0, ends // tmx - 1, -1)
    spare = used + jnp.arange(N_EXPERTS, dtype=I32)
    zero_tiles = jnp.concatenate([last_tile, jnp.where(spare < n_tiles, spare, -1)]).astype(I32)
    xs = _scatter(zero_tiles, dest, h2, n_tiles * tmx, tm_rows, tmx)
    ys = _experts(tile_expert, tile_valid, xs, w_gate, b_gate, w_up, b_up, w_down, b_down, tmx)
    return _combine(dest, route, r1, p2, ys, w_ple_gate, w_ple_proj, norm_final, tm_rows)


def kernel(x, p, w_in, b_gates, lam_re, lam_im, log_dt, b_re, b_im, c_re, c_im, d_skip, w_glu, b_glu,
           w_ssm_br, w_attn_br, w_o, norm_mix, norm_ffn, w_router, b_router, w_gate, b_gate, w_up, b_up,
           w_down, b_down, w_ple_gate, w_ple_proj, norm_final):
    bsz, seq, _ = x.shape
    t = bsz * seq
    x2 = x.reshape(t, D_MODEL)
    u4, q, k, v, qi, ki8, wi, gs, ga = _inproj(x2, norm_mix[0], w_in[0], b_gates[0], seq, tm=256)
    mats = _ssm_mats(lam_re[0], lam_im[0], log_dt[0], b_re[0], b_im[0], c_re[0], c_im[0], d_skip[0])
    y4 = _ssm(u4, mats, bsz, seq)
    y_attn = _attn(q, k, v, qi, ki8, wi, bsz, seq, tq=256, tk=512)
    r1, h2, route = _mix(y4, y_attn, gs, ga, x2, w_glu[0], b_glu[0], w_ssm_br[0], w_attn_br[0], w_o[0],
                         norm_ffn[0], w_router[0], b_router[0], tm=256)
    out = _moe(h2, route, r1, p[0].reshape(t, PLE_DIM), w_gate[0], b_gate[0], w_up[0], b_up[0], w_down[0],
               b_down[0], w_ple_gate[0], w_ple_proj[0], norm_final, tm_prefix=512, tm_rows=256, tmx=512)
    return out.reshape(bsz, seq, D_MODEL)
```

```python
import functools
import math

import jax
import jax.numpy as jnp
from jax import lax
from jax.experimental import pallas as pl
from jax.experimental.pallas import tpu as pltpu

F32 = jnp.float32
BF16 = jnp.bfloat16
I32 = jnp.int32

D_MODEL = 1024
CHUNK = 64
PLE_DIM = 256
EPS = 1e-6
SSM_WIDTH = 512
SSM_GROUP = 16
SSM_GROUPS = 32
SSM_STATE = 64
N_HEADS = 8
HEAD_DIM = 64
ATTN_WIDTH = 512
IDX_HEADS = 8
IDX_DIM = 32
TOPK_MAX = 256
ROPE_THETA = 10000.0
N_EXPERTS = 32
TOP_K = 4
SWIGLU_LIMIT = 7.0
SWIGLU_ALPHA = 1.702

LANES = 128
SSM_TC = 16
SSM_GL = LANES // SSM_GROUP
SSM_NLB = SSM_WIDTH // LANES
NEG_BIG = -1e30
LOG2E = 1.4426950408889634
INT_MIN = -2147483648
VMEM_LIMIT = 56 * 1024 * 1024

_C_U, _C_Q, _C_K, _C_V, _C_QI, _C_KI, _C_WI, _C_GS, _C_GA = 0, 512, 1024, 1536, 2048, 2304, 2560, 2688, 3712
_C_END = 4736


def _rms(x, g):
    return x * lax.rsqrt(jnp.mean(x * x, axis=-1, keepdims=True) + EPS) * g


def _dot(a, b):
    return jnp.dot(a, b, preferred_element_type=F32)


def _dot_nt(a, b):
    return lax.dot_general(a, b, (((1,), (1,)), ((), ())), preferred_element_type=F32)


def _inproj_kernel(x_ref, g_ref, w_ref, bg_ref, cq_ref, sq_ref, ci_ref, si_ref,
                   u_ref, q_ref, k_ref, v_ref, qi_ref, ki_ref, wi_ref, gs_ref, ga_ref):
    h = _rms(x_ref[...], g_ref[...]).astype(BF16)

    def mm(c0, n):
        return _dot(h, w_ref[:, c0:c0 + n])

    u = mm(_C_U, 512)
    for lb in range(SSM_NLB):
        u_ref[lb] = u[:, lb * LANES:(lb + 1) * LANES]
    def rope(z, cos, sin, d):
        n = z.shape[1]
        lane = lax.broadcasted_iota(I32, z.shape, 1)
        partner = jnp.where((lane & (d - 1)) < d // 2, pltpu.roll(z, n - d // 2, 1), pltpu.roll(z, d // 2, 1))
        return z * cos + partner * sin

    cq = cq_ref[...]
    sq = sq_ref[...]
    q_ref[...] = (rope(mm(_C_Q, 512), cq, sq, HEAD_DIM) * (HEAD_DIM ** -0.5 * LOG2E)).astype(BF16)
    k_ref[...] = rope(mm(_C_K, 512), cq, sq, HEAD_DIM).astype(BF16)
    v_ref[...] = mm(_C_V, 512).astype(BF16)
    ci = ci_ref[...]
    si = si_ref[...]
    qi_ref[...] = rope(mm(_C_QI, 256), ci, si, IDX_DIM).astype(BF16)
    ki_ref[...] = rope(mm(_C_KI, 256), ci, si, IDX_DIM).astype(BF16)
    wi_ref[...] = mm(_C_WI, 128) * ((IDX_HEADS * IDX_DIM) ** -0.5)
    bg = bg_ref[...]
    gs_ref[...] = jax.nn.sigmoid(mm(_C_GS, 1024) + bg[:, :1024]).astype(BF16)
    ga_ref[...] = jax.nn.sigmoid(mm(_C_GA, 1024) + bg[:, 1024:]).astype(BF16)


def _rope_tables(seq, d, reps):
    half = d // 2
    inv = ROPE_THETA ** (-jnp.arange(half, dtype=F32) * 2.0 / d)
    ang = jnp.arange(seq, dtype=F32)[:, None] * inv[None, :]
    cos, sin = jnp.cos(ang), jnp.sin(ang)
    c = jnp.concatenate([cos, cos], axis=-1)
    s = jnp.concatenate([-sin, sin], axis=-1)
    return jnp.tile(c, (1, reps)), jnp.tile(s, (1, reps))


def _inproj(x2, norm_mix, w_in, b_gates, seq, tm):
    t = x2.shape[0]
    w_u, w_q, w_k, w_v, w_qi, w_ki, w_wi, w_gs, w_ga = jnp.split(
        w_in, [512, 1024, 1536, 2048, 2304, 2336, 2344, 3368], axis=1)
    w_ki8 = jnp.tile(w_ki, (1, IDX_HEADS))
    w_wi_p = jnp.pad(w_wi, ((0, 0), (0, LANES - IDX_HEADS)))
    w_all = jnp.concatenate([w_u, w_q, w_k, w_v, w_qi, w_ki8, w_wi_p, w_gs, w_ga], axis=1).astype(BF16)
    assert w_all.shape[1] == _C_END
    cq, sq = _rope_tables(seq, HEAD_DIM, N_HEADS)
    ci, si = _rope_tables(seq, IDX_DIM, IDX_HEADS)
    nt = seq // tm
    row = lambda i: (i, 0)
    pos = lambda i: (i % nt, 0)
    const = lambda i: (0, 0)
    outs = pl.pallas_call(
        _inproj_kernel,
        grid=(t // tm,),
        in_specs=[
            pl.BlockSpec((tm, D_MODEL), row),
            pl.BlockSpec((1, D_MODEL), const),
            pl.BlockSpec((D_MODEL, _C_END), const),
            pl.BlockSpec((1, 2 * D_MODEL), const),
            pl.BlockSpec((tm, 512), pos), pl.BlockSpec((tm, 512), pos),
            pl.BlockSpec((tm, 256), pos), pl.BlockSpec((tm, 256), pos),
        ],
        out_specs=[
            pl.BlockSpec((SSM_NLB, tm, LANES), lambda i: (0, i, 0)),
            pl.BlockSpec((tm, 512), row), pl.BlockSpec((tm, 512), row), pl.BlockSpec((tm, 512), row),
            pl.BlockSpec((tm, 256), row), pl.BlockSpec((tm, 256), row), pl.BlockSpec((tm, LANES), row),
            pl.BlockSpec((tm, D_MODEL), row), pl.BlockSpec((tm, D_MODEL), row),
        ],
        out_shape=[
            jax.ShapeDtypeStruct((SSM_NLB, t, LANES), F32),
            jax.ShapeDtypeStruct((t, 512), BF16), jax.ShapeDtypeStruct((t, 512), BF16),
            jax.ShapeDtypeStruct((t, 512), BF16),
            jax.ShapeDtypeStruct((t, 256), BF16), jax.ShapeDtypeStruct((t, 256), BF16),
            jax.ShapeDtypeStruct((t, LANES), F32),
            jax.ShapeDtypeStruct((t, D_MODEL), BF16), jax.ShapeDtypeStruct((t, D_MODEL), BF16),
        ],
        compiler_params=pltpu.CompilerParams(dimension_semantics=("arbitrary",),
                                             vmem_limit_bytes=VMEM_LIMIT),
        name="inproj",
    )(x2, norm_mix.reshape(1, D_MODEL), w_all, b_gates.reshape(1, 2 * D_MODEL), cq, sq, ci, si)
    return outs


def _ssm_mats(lam_re, lam_im, log_dt, b_re, b_im, c_re, c_im, d_skip):
    g_, p_, h_, tc, gl, nlb = SSM_GROUPS, SSM_STATE, SSM_GROUP, SSM_TC, SSM_GL, SSM_NLB
    lam = lax.complex(lam_re.astype(F32), lam_im.astype(F32))
    dt = jnp.exp(log_dt.astype(F32))[:, None]
    lam_dt = lam * dt
    lam_bar = jnp.exp(lam_dt)
    b_bar = ((lam_bar - 1.0) / lam)[..., None] * lax.complex(b_re.astype(F32), b_im.astype(F32))
    c = lax.complex(c_re.astype(F32), c_im.astype(F32))
    steps = jnp.arange(tc + 1, dtype=F32)
    pw = jnp.exp(lam_dt[None] * steps[:, None, None])
    hp = lax.Precision.HIGHEST
    lane_g = jnp.arange(LANES) // h_
    st_g = (jnp.arange(2 * gl * p_) % (gl * p_)) // p_
    rep_o = (jnp.arange(h_)[:, None] == (jnp.arange(LANES) % h_)[None, :]).astype(F32)
    st_col = (jnp.arange(2 * gl * p_) // (gl * p_)) * p_ + jnp.arange(2 * gl * p_) % p_
    rep_s = (jnp.arange(2 * p_)[:, None] == st_col[None, :]).astype(F32)
    taps = jnp.einsum('gop,tgp,gpi->gtoi', c, pw[:tc], b_bar).real
    a3 = taps.reshape(nlb, gl, tc, h_, h_).transpose(0, 2, 1, 4, 3).reshape(nlb, tc, LANES, h_)
    d = jnp.einsum('btro,oc->btrc', a3, rep_o, precision=hp)
    d = d * (lane_g[:, None] == lane_g[None, :]).astype(F32)
    dcat = d.transpose(0, 2, 1, 3).reshape(nlb, LANES, tc * LANES)
    m_intra = jnp.stack(
        [jnp.pad(dcat[:, :, :(tc - j) * LANES], ((0, 0), (0, 0), (j * LANES, 0))) for j in range(tc)],
        axis=1).reshape(nlb, tc * LANES, tc * LANES)
    sc = pw[:tc][::-1][:, :, :, None] * b_bar[None]
    sc = jnp.stack([sc.real, sc.imag], axis=0).reshape(2, tc, nlb, gl, p_, h_)
    a_s = sc.transpose(2, 1, 3, 5, 0, 4).reshape(nlb, tc, LANES, 2 * p_)
    m_state = jnp.einsum('bjrq,qc->bjrc', a_s, rep_s, precision=hp)
    m_state = (m_state * (lane_g[:, None] == st_g[None, :]).astype(F32)).reshape(nlb, tc * LANES, 2 * gl * p_)
    oc = c[None] * pw[1:tc + 1][:, :, None, :]
    oc = jnp.stack([oc.real, -oc.imag], axis=0).reshape(2, tc, nlb, gl, h_, p_)
    a_o = oc.transpose(2, 1, 0, 3, 5, 4).reshape(nlb, tc, 2 * gl * p_, h_)
    m_out = jnp.einsum('blro,oc->blrc', a_o, rep_o, precision=hp)
    m_out = m_out * (st_g[:, None] == lane_g[None, :]).astype(F32)
    m_out = m_out.transpose(0, 2, 1, 3).reshape(nlb, 2 * gl * p_, tc * LANES)
    a = pw[tc].reshape(nlb, 1, gl * p_)
    d_l = jnp.tile(d_skip.astype(F32).reshape(nlb, 1, LANES), (1, 1, tc))
    return (m_intra.astype(BF16), m_state.astype(BF16), m_out.astype(BF16),
            a.real.astype(F32), a.imag.astype(F32), d_l)


def _ssm_kernel(u_ref, mi_ref, ms_ref, mo_ref, are_ref, aim_ref, d_ref, y_ref, con_ref, sp_ref):
    nch = u_ref.shape[0] // SSM_TC
    ns = are_ref.shape[1]
    uf = jnp.concatenate([u_ref[pl.ds(j, nch, stride=SSM_TC), :] for j in range(SSM_TC)], axis=1)
    u = uf.astype(BF16)
    con_ref[...] = _dot(u, ms_ref[...])
    a_re = are_ref[...]
    a_im = aim_ref[...]

    def step(c, carry):
        s_re, s_im = carry
        sp_ref[pl.ds(c, 1), :ns] = s_re
        sp_ref[pl.ds(c, 1), ns:] = s_im
        c_re = con_ref[pl.ds(c, 1), :ns]
        c_im = con_ref[pl.ds(c, 1), ns:]
        return (a_re * s_re - a_im * s_im + c_re, a_re * s_im + a_im * s_re + c_im)

    zero = jnp.zeros((1, ns), F32)
    lax.fori_loop(0, nch, step, (zero, zero))
    y = _dot(u, mi_ref[...]) + _dot(sp_ref[...].astype(BF16), mo_ref[...]) + d_ref[...] * uf
    for l in range(SSM_TC):
        y_ref[pl.ds(l, nch, stride=SSM_TC), :] = y[:, l * LANES:(l + 1) * LANES]


def _ssm(u4, mats, bsz, seq):
    m_intra, m_state, m_out, a_re, a_im, d_l = mats
    t = u4.shape[1]
    nch = seq // SSM_TC
    wc = SSM_TC * LANES
    ns = SSM_GL * SSM_STATE
    wmap = lambda lb, b: (lb, 0, 0)
    return pl.pallas_call(
        _ssm_kernel,
        grid=(SSM_NLB, bsz),
        in_specs=[
            pl.BlockSpec((None, seq, LANES), lambda lb, b: (lb, b, 0)),
            pl.BlockSpec((None, wc, wc), wmap),
            pl.BlockSpec((None, wc, 2 * ns), wmap),
            pl.BlockSpec((None, 2 * ns, wc), wmap),
            pl.BlockSpec((None, 1, ns), wmap), pl.BlockSpec((None, 1, ns), wmap),
            pl.BlockSpec((None, 1, wc), wmap),
        ],
        out_specs=pl.BlockSpec((None, seq, LANES), lambda lb, b: (lb, b, 0)),
        out_shape=jax.ShapeDtypeStruct((SSM_NLB, t, LANES), F32),
        scratch_shapes=[pltpu.VMEM((nch, 2 * ns), F32), pltpu.VMEM((nch, 2 * ns), F32)],
        compiler_params=pltpu.CompilerParams(dimension_semantics=("arbitrary", "arbitrary"),
                                             vmem_limit_bytes=VMEM_LIMIT),
        name="ssm",
    )(u4, m_intra, m_state, m_out, a_re, a_im, d_l)


def _attn_kernel(q_ref, k_ref, v_ref, qi_ref, ki_ref, wi_ref, tri_ref, o_ref, keys_ref, kt_ref, k16_ref,
                 qh_ref, qm_ref, acc_ref, m_ref, *, tq, tk, seq, topk):
    i = pl.program_id(1)
    s0 = i * tq
    nk = (s0 + tq + tk - 1) // tk
    row = s0 + lax.broadcasted_iota(I32, (tq, 1), 0)
    lim = ((row >> 6) + 1) << 6
    lim_q = (((s0 + lax.broadcasted_iota(I32, (1, tq), 1)) >> 6) + 1) << 6
    kvec = jnp.minimum(lim_q, topk).astype(F32)
    col0 = lax.broadcasted_iota(I32, (tq, tk), 1)

    qi = qi_ref[...]
    lane_i = lax.broadcasted_iota(I32, qi.shape, 1)
    for h in range(IDX_HEADS):
        qh_ref[h] = jnp.where((lane_i >> 5) == h, qi, jnp.zeros_like(qi))
    wi = wi_ref[...]
    wcol = [wi[:, h:h + 1] for h in range(IDX_HEADS)]

    def score_tile(j, carry):
        kt = ki_ref[pl.ds(pl.multiple_of(j * tk, tk), tk), :]
        acc = jnp.zeros((tq, tk), F32)
        for h in range(IDX_HEADS):
            acc = acc + wcol[h] * jnp.maximum(_dot_nt(qh_ref[h], kt), 0.0)
        sc = jnp.where(col0 + j * tk < lim, acc, -jnp.inf)
        bits = pltpu.bitcast(sc, I32)
        key = bits ^ ((bits >> 31) & 0x7FFFFFFF)
        keys_ref[j] = key
        key_t = key.T
        kt_ref[j] = key_t
        k16_ref[j] = (key_t >> 16).astype(jnp.int16)
        return carry

    lax.fori_loop(0, nk, score_tile, 0)

    rg = 16
    one16 = jnp.ones((rg, tq), jnp.int16)
    zero16 = jnp.zeros((rg, tq), jnp.int16)

    def count16(cand):
        cand16 = jnp.broadcast_to(cand, (rg, tq)).astype(jnp.int16)

        def body(j, part):
            kk = k16_ref[j]
            for r in range(tk // rg):
                part = part + jnp.where(kk[r * rg:(r + 1) * rg, :] >= cand16, one16, zero16)
            return part
        part = lax.fori_loop(0, nk, body, jnp.zeros((rg, tq), jnp.int16))
        return jnp.sum(part.astype(F32), axis=0, keepdims=True)

    def bit_step(it, carry):
        u_ans, g_lo, g_hi = carry
        cand_u = u_ans | lax.shift_left(jnp.int32(1), 15 - it)
        cnt = count16(cand_u - 32768)
        ok = cnt >= kvec
        return jnp.where(ok, cand_u, u_ans), jnp.where(ok, cnt, g_lo), jnp.where(ok, g_hi, cnt)

    u_ans, g_lo, g_hi = lax.fori_loop(
        0, 16, bit_step, (jnp.zeros((1, tq), I32), lim_q.astype(F32), jnp.zeros((1, tq), F32)))
    b16 = u_ans - 32768

    def rebase(j, carry):
        kk = kt_ref[j]
        hi16 = kk >> 16
        low = (kk & 0xFFFF) - 32768
        rel = jnp.where(hi16 > b16, 32767, jnp.where(hi16 < b16, -32768, low))
        k16_ref[j] = rel.astype(jnp.int16)
        return carry

    lax.fori_loop(0, nk, rebase, 0)

    def unfinished(lo, hi, g_lo):
        return jnp.where(g_lo != kvec, jnp.where(hi - lo > 1, 1.0, 0.0), 0.0)

    def refine_cond(carry):
        return (carry[0] < 18) & (carry[1] > 0.0)

    def refine(carry):
        it, _, lo, hi, g_lo, g_hi = carry
        todo = unfinished(lo, hi, g_lo)
        mid = lo + ((hi - lo) >> 1)
        cand = jnp.where(b16 == 0, jnp.where(hi - lo == 65536, 1, mid), mid)
        cnt = count16(cand - 32768)
        up = jnp.where(cnt >= kvec, todo, 0.0) > 0.0
        dn = jnp.where(cnt < kvec, todo, 0.0) > 0.0
        lo = jnp.where(up, cand, lo)
        g_lo = jnp.where(up, cnt, g_lo)
        hi = jnp.where(dn, cand, hi)
        g_hi = jnp.where(dn, cnt, g_hi)
        return it + 1, jnp.max(unfinished(lo, hi, g_lo)), lo, hi, g_lo, g_hi

    off_lo = jnp.zeros((1, tq), I32)
    off_hi = jnp.full((1, tq), 65536, I32)
    _, _, off_lo, _, g_lo, g_hi = lax.while_loop(
        refine_cond, refine,
        (jnp.int32(0), jnp.max(unfinished(off_lo, off_hi, g_lo)), off_lo, off_hi, g_lo, g_hi))
    has_tie = jnp.max(g_lo - kvec) > 0.0

    def to_rows(x):
        return jnp.transpose(jnp.broadcast_to(x, (LANES, tq)))

    vth = to_rows((b16 << 16) + off_lo)

    @pl.when(jnp.logical_not(has_tie))
    def _():
        def to_bias(j, carry):
            kk = keys_ref[j]
            for c in range(tk // LANES):
                sl = slice(c * LANES, (c + 1) * LANES)
                keys_ref[j, :, sl] = pltpu.bitcast(jnp.where(kk[:, sl] >= vth, 0.0, NEG_BIG).astype(F32), I32)
            return carry
        lax.fori_loop(0, nk, to_bias, 0)

    @pl.when(has_tie)
    def _():
        need = to_rows(kvec - g_hi)

        def to_bias(j, seen):
            kk = keys_ref[j]
            eq = jnp.concatenate(
                [jnp.where(kk[:, c * LANES:(c + 1) * LANES] == vth, 1.0, 0.0) for c in range(tk // LANES)],
                axis=1)
            rank = _dot(eq.astype(BF16), tri_ref[...])
            for c in range(tk // LANES):
                sl = slice(c * LANES, (c + 1) * LANES)
                take = jnp.where(seen + rank[:, sl] <= need, eq[:, sl], 0.0)
                sel = jnp.where(kk[:, sl] > vth, 1.0, take)
                keys_ref[j, :, sl] = pltpu.bitcast(jnp.where(sel > 0.0, 0.0, NEG_BIG).astype(F32), I32)
            return seen + jnp.broadcast_to(rank[:, tk - 1:tk], (tq, LANES))
        lax.fori_loop(0, nk, to_bias, jnp.zeros((tq, LANES), F32))

    lane_q = lax.broadcasted_iota(I32, (tq, LANES), 1)
    lane_k = lax.broadcasted_iota(I32, (tk, LANES), 1)
    for h in range(N_HEADS):
        qp = q_ref[:, (h // 2) * LANES:(h // 2 + 1) * LANES]
        keep = (lane_q >= HEAD_DIM) if h % 2 else (lane_q < HEAD_DIM)
        qm_ref[h] = jnp.where(keep, qp, jnp.zeros_like(qp))
    acc_ref[...] = jnp.zeros_like(acc_ref)
    m_ref[...] = jnp.full(m_ref.shape, NEG_BIG, F32)
    ones_v = jnp.ones((tk, LANES), BF16)

    def attn_tile(j, carry):
        rows = pl.ds(pl.multiple_of(j * tk, tk), tk)
        bias = pltpu.bitcast(keys_ref[j], F32)
        for h in range(N_HEADS):
            cs = slice((h // 2) * LANES, (h // 2 + 1) * LANES)
            own = (lane_k >= HEAD_DIM) if h % 2 else (lane_k < HEAD_DIM)
            s = _dot_nt(qm_ref[h], k_ref[rows, cs]) + bias
            m_old = m_ref[h]
            m_new = jnp.maximum(m_old, jnp.max(s, axis=1, keepdims=True))
            p = jnp.concatenate(
                [jnp.exp2(s[:, c * LANES:(c + 1) * LANES] - m_new) for c in range(tk // LANES)],
                axis=1).astype(BF16)
            vh = jnp.where(own, v_ref[rows, cs], ones_v)
            acc_ref[h] = jnp.exp2(m_old - m_new) * acc_ref[h] + _dot(p, vh)
            m_ref[h] = m_new
        return carry

    lax.fori_loop(0, nk, attn_tile, 0)

    for hp in range(N_HEADS // 2):
        a0 = acc_ref[2 * hp]
        a1 = acc_ref[2 * hp + 1]
        out = jnp.where(lane_q < HEAD_DIM, a0 / pltpu.roll(a0, HEAD_DIM, 1), a1 / pltpu.roll(a1, HEAD_DIM, 1))
        o_ref[:, hp * LANES:(hp + 1) * LANES] = out.astype(o_ref.dtype)


def _attn(q, k, v, qi, ki8, wi, bsz, seq, tq, tk):
    t = q.shape[0]
    nq = seq // tq
    qrow = lambda b, i: (b * nq + i, 0)
    kv = lambda b, i: (b, 0)
    kern = functools.partial(_attn_kernel, tq=tq, tk=tk, seq=seq, topk=min(TOPK_MAX, seq // 4))
    tri = (jnp.arange(tk)[:, None] <= jnp.arange(tk)[None, :]).astype(BF16)
    return pl.pallas_call(
        kern,
        grid=(bsz, nq),
        in_specs=[
            pl.BlockSpec((tq, ATTN_WIDTH), qrow),
            pl.BlockSpec((seq, ATTN_WIDTH), kv), pl.BlockSpec((seq, ATTN_WIDTH), kv),
            pl.BlockSpec((tq, 256), qrow), pl.BlockSpec((seq, 256), kv),
            pl.BlockSpec((tq, LANES), qrow),
            pl.BlockSpec((tk, tk), lambda b, i: (0, 0)),
        ],
        out_specs=pl.BlockSpec((tq, ATTN_WIDTH), qrow),
        out_shape=jax.ShapeDtypeStruct((t, ATTN_WIDTH), BF16),
        scratch_shapes=[
            pltpu.VMEM((seq // tk, tq, tk), I32),
            pltpu.VMEM((seq // tk, tk, tq), I32),
            pltpu.VMEM((seq // tk, tk, tq), jnp.int16),
            pltpu.VMEM((IDX_HEADS, tq, 256), BF16),
            pltpu.VMEM((N_HEADS, tq, LANES), BF16),
            pltpu.VMEM((N_HEADS, tq, LANES), F32),
            pltpu.VMEM((N_HEADS, tq, LANES), F32),
        ],
        compiler_params=pltpu.CompilerParams(dimension_semantics=("arbitrary", "arbitrary"),
                                             vmem_limit_bytes=VMEM_LIMIT),
        name="attn",
    )(q, k, v, qi, ki8, wi, tri)


def _mix_kernel(y4_ref, ya_ref, gs_ref, ga_ref, x_ref, wglu_ref, bglu_ref, wsb_ref, wab_ref, wo_ref,
                nf_ref, wrh_ref, wrl_ref, br_ref, r1_ref, h2_ref, route_ref):
    ys = jnp.concatenate([y4_ref[lb] for lb in range(SSM_NLB)], axis=1)
    ys = 0.5 * ys * (1.0 + jnp.tanh(math.sqrt(2.0 / math.pi) * (ys + 0.044715 * (ys * ys * ys))))
    ys = ys * jax.nn.sigmoid(_dot(ys.astype(BF16), wglu_ref[...]) + bglu_ref[...])
    br_s = _dot(ys.astype(BF16), wsb_ref[...])
    br_a = _dot(ya_ref[...], wab_ref[...])
    mix = gs_ref[...].astype(F32) * br_s + ga_ref[...].astype(F32) * br_a
    r1 = x_ref[...] + _dot(mix.astype(BF16), wo_ref[...])
    r1_ref[...] = r1
    h2 = _rms(r1, nf_ref[...])
    h2_ref[...] = h2
    hi = h2.astype(BF16)
    lo = (h2 - hi.astype(F32)).astype(BF16)
    logits = _dot(hi, wrh_ref[...]) + (_dot(hi, wrl_ref[...]) + _dot(lo, wrh_ref[...])) + br_ref[...]
    tm = logits.shape[0]
    lane = lax.broadcasted_iota(I32, (tm, LANES), 1).astype(F32)
    vals = logits
    tops, idxs = [], []
    for _ in range(TOP_K):
        m = jnp.max(vals, axis=1, keepdims=True)
        idx = jnp.min(jnp.where(vals == m, lane, float(LANES)), axis=1, keepdims=True)
        tops.append(m)
        idxs.append(idx)
        vals = jnp.where(lane == idx, -jnp.inf, vals)
    es = [jnp.exp(tv - tops[0]) for tv in tops]
    den = es[0] + es[1] + es[2] + es[3]
    route = jnp.zeros((tm, LANES), F32)
    for kk in range(TOP_K):
        route = jnp.where(lane == float(kk), idxs[kk], route)
        route = jnp.where(lane == float(TOP_K + kk), es[kk] / den, route)
    route_ref[...] = route


def _mix(y4, y_attn, gs, ga, x2, w_glu, b_glu, w_ssm_br, w_attn_br, w_o, norm_ffn, w_router, b_router, tm):
    t = x2.shape[0]
    wr = jnp.pad(w_router.astype(F32), ((0, 0), (0, LANES - N_EXPERTS)))
    wr_hi = wr.astype(BF16)
    wr_lo = (wr - wr_hi.astype(F32)).astype(BF16)
    br = jnp.concatenate([b_router.astype(F32), jnp.full((LANES - N_EXPERTS,), NEG_BIG, F32)]).reshape(1, LANES)
    row = lambda i: (i, 0)
    const = lambda i: (0, 0)
    return pl.pallas_call(
        _mix_kernel,
        grid=(t // tm,),
        in_specs=[
            pl.BlockSpec((SSM_NLB, tm, LANES), lambda i: (0, i, 0)),
            pl.BlockSpec((tm, ATTN_WIDTH), row),
            pl.BlockSpec((tm, D_MODEL), row), pl.BlockSpec((tm, D_MODEL), row), pl.BlockSpec((tm, D_MODEL), row),
            pl.BlockSpec((SSM_WIDTH, SSM_WIDTH), const), pl.BlockSpec((1, SSM_WIDTH), const),
            pl.BlockSpec((SSM_WIDTH, D_MODEL), const), pl.BlockSpec((ATTN_WIDTH, D_MODEL), const),
            pl.BlockSpec((D_MODEL, D_MODEL), const), pl.BlockSpec((1, D_MODEL), const),
            pl.BlockSpec((D_MODEL, LANES), const), pl.BlockSpec((D_MODEL, LANES), const),
            pl.BlockSpec((1, LANES), const),
        ],
        out_specs=[pl.BlockSpec((tm, D_MODEL), row), pl.BlockSpec((tm, D_MODEL), row),
                   pl.BlockSpec((tm, LANES), row)],
        out_shape=[jax.ShapeDtypeStruct((t, D_MODEL), F32), jax.ShapeDtypeStruct((t, D_MODEL), F32),
                   jax.ShapeDtypeStruct((t, LANES), F32)],
        compiler_params=pltpu.CompilerParams(dimension_semantics=("arbitrary",),
                                             vmem_limit_bytes=VMEM_LIMIT),
        name="mix",
    )(y4, y_attn, gs, ga, x2, w_glu.astype(BF16), b_glu.reshape(1, SSM_WIDTH).astype(F32),
      w_ssm_br.astype(BF16), w_attn_br.astype(BF16), w_o.astype(BF16), norm_ffn.reshape(1, D_MODEL),
      wr_hi, wr_lo, br)


def _prefix_kernel(route_ref, tri_ref, pre_ref, tot_ref, carry_ref):
    @pl.when(pl.program_id(0) == 0)
    def _():
        carry_ref[...] = jnp.zeros_like(carry_ref)

    route = route_ref[...]
    tm = route.shape[0]
    lane = lax.broadcasted_iota(I32, (tm, LANES), 1).astype(F32)
    mask = jnp.zeros((tm, LANES), F32)
    for kk in range(TOP_K):
        mask = jnp.where(lane == route[:, kk:kk + 1], 1.0, mask)
    incl = _dot(tri_ref[...], mask.astype(BF16))
    carry = carry_ref[...]
    pre_ref[...] = carry + incl - mask
    carry = carry + incl[tm - 1:tm, :]
    carry_ref[...] = carry
    tot_ref[...] = carry


def _prefix(route, tm):
    t = route.shape[0]
    tri = (jnp.arange(tm)[:, None] >= jnp.arange(tm)[None, :]).astype(BF16)
    return pl.pallas_call(
        _prefix_kernel,
        grid=(t // tm,),
        in_specs=[pl.BlockSpec((tm, LANES), lambda i: (i, 0)), pl.BlockSpec((tm, tm), lambda i: (0, 0))],
        out_specs=[pl.BlockSpec((tm, LANES), lambda i: (i, 0)), pl.BlockSpec((1, LANES), lambda i: (0, 0))],
        out_shape=[jax.ShapeDtypeStruct((t, LANES), F32), jax.ShapeDtypeStruct((1, LANES), F32)],
        scratch_shapes=[pltpu.VMEM((1, LANES), F32)],
        compiler_params=pltpu.CompilerParams(dimension_semantics=("arbitrary",)),
        name="prefix",
    )(route, tri)


MOE_TT = 512
MOE_SEG = 8
MOE_NL = MOE_TT * TOP_K + N_EXPERTS * MOE_SEG
MOE_W = D_MODEL + LANES
_SEG_SIZES = tuple(MOE_SEG << b for b in range((MOE_TT // MOE_SEG).bit_length() - 1, -1, -1))


def _segment_copies(seg_ref, e, make_copy, fn):
    c = seg_ref[0, e]
    s = seg_ref[0, N_EXPERTS + e]
    d = seg_ref[0, 2 * N_EXPERTS + e]
    for size in _SEG_SIZES:
        units = size // MOE_SEG
        hit = (c & units) != 0

        @pl.when(hit)
        def _():
            fn(make_copy(pl.multiple_of(s * MOE_SEG, MOE_SEG), pl.multiple_of(d * MOE_SEG, MOE_SEG), size))
        s = s + jnp.where(hit, units, 0)
        d = d + jnp.where(hit, units, 0)


def _dispatch_kernel(zt_ref, seg_ref, lpos_ref, h2_ref, wtok_ref, xs_ref, buf_ref, zbuf_ref, sem, zsem):
    tmx = zbuf_ref.shape[0]

    @pl.when(pl.program_id(0) == 0)
    def _():
        zbuf_ref[...] = jnp.zeros_like(zbuf_ref)

        def zero_copy(z):
            return pltpu.make_async_copy(
                zbuf_ref, xs_ref.at[pl.ds(pl.multiple_of(zt_ref[z] * tmx, tmx), tmx)], zsem)

        for z in range(zt_ref.shape[0]):
            @pl.when(zt_ref[z] >= 0)
            def _():
                zero_copy(z).start()
        for z in range(zt_ref.shape[0]):
            @pl.when(zt_ref[z] >= 0)
            def _():
                zero_copy(z).wait()

    rowi = lax.broadcasted_iota(I32, (MOE_NL, MOE_TT), 0)
    sel = jnp.zeros((MOE_NL, MOE_TT), F32)
    for kk in range(TOP_K):
        sel = jnp.where(rowi == lpos_ref[kk:kk + 1, :], 1.0, sel)
    src = jnp.concatenate([h2_ref[...].astype(BF16), wtok_ref[...]], axis=1)
    buf_ref[...] = _dot(sel.astype(BF16), src)

    def make_copy(s, d, size):
        return pltpu.make_async_copy(buf_ref.at[pl.ds(s, size)], xs_ref.at[pl.ds(d, size)], sem)

    def start(e, carry):
        _segment_copies(seg_ref, e, make_copy, lambda cp: cp.start())
        return carry

    def wait(e, carry):
        _segment_copies(seg_ref, e, make_copy, lambda cp: cp.wait())
        return carry

    lax.fori_loop(0, N_EXPERTS, start, 0)
    lax.fori_loop(0, N_EXPERTS, wait, 0)


def _dispatch(zero_tiles, seg, lpos_t, h2, wtok, n_rows, tmx):
    t = h2.shape[0]
    grid_spec = pltpu.PrefetchScalarGridSpec(
        num_scalar_prefetch=1,
        grid=(t // MOE_TT,),
        in_specs=[
            pl.BlockSpec((None, 1, 3 * N_EXPERTS), lambda i, zt: (i, 0, 0), memory_space=pltpu.SMEM),
            pl.BlockSpec((None, 8, MOE_TT), lambda i, zt: (i, 0, 0)),
            pl.BlockSpec((MOE_TT, D_MODEL), lambda i, zt: (i, 0)),
            pl.BlockSpec((MOE_TT, LANES), lambda i, zt: (i, 0)),
        ],
        out_specs=pl.BlockSpec(memory_space=pl.ANY),
        scratch_shapes=[pltpu.VMEM((MOE_NL, MOE_W), F32), pltpu.VMEM((tmx, MOE_W), F32),
                        pltpu.SemaphoreType.DMA(()), pltpu.SemaphoreType.DMA(())],
    )
    return pl.pallas_call(
        _dispatch_kernel,
        grid_spec=grid_spec,
        out_shape=jax.ShapeDtypeStruct((n_rows, MOE_W), F32),
        compiler_params=pltpu.CompilerParams(dimension_semantics=("arbitrary",), has_side_effects=True,
                                             vmem_limit_bytes=VMEM_LIMIT),
        name="dispatch",
    )(zero_tiles, seg, lpos_t, h2, wtok)


def _expert_kernel(te_ref, tv_ref, x_ref, wg_ref, bg_ref, wu_ref, bu_ref, wd_ref, bd_ref, y_ref,
                   wgb_ref, wub_ref, wdb_ref):
    i = pl.program_id(0)
    e = te_ref[i]
    prev = te_ref[jnp.maximum(i - 1, 0)]

    @pl.when((i == 0) | (e != prev))
    def _():
        wgb_ref[...] = wg_ref[...].astype(BF16)
        wub_ref[...] = wu_ref[...].astype(BF16)
        wdb_ref[...] = wd_ref[...].astype(BF16)

    @pl.when(tv_ref[i] != 0)
    def _():
        x = x_ref[:, :D_MODEL].astype(BF16)
        rt = x_ref[:, D_MODEL:]
        ef = e.astype(F32)
        w_row = jnp.zeros((rt.shape[0], 1), F32)
        for kk in range(TOP_K):
            wk = (rt[:, TOP_K + kk:TOP_K + kk + 1] + rt[:, 2 * TOP_K + kk:2 * TOP_K + kk + 1]
                  + rt[:, 3 * TOP_K + kk:3 * TOP_K + kk + 1])
            w_row = w_row + jnp.where(rt[:, kk:kk + 1] == ef, wk, 0.0)
        g = jnp.minimum(_dot(x, wgb_ref[...]) + bg_ref[...], SWIGLU_LIMIT)
        u = jnp.clip(_dot(x, wub_ref[...]) + bu_ref[...], -SWIGLU_LIMIT, SWIGLU_LIMIT)
        a = (u + 1.0) * (g * jax.nn.sigmoid(SWIGLU_ALPHA * g))
        y_ref[...] = (_dot(a.astype(BF16), wdb_ref[...]) + bd_ref[...]) * w_row

    @pl.when(tv_ref[i] == 0)
    def _():
        y_ref[...] = jnp.zeros_like(y_ref)


def _experts(tile_expert, tile_valid, xs, w_gate, b_gate, w_up, b_up, w_down, b_down, tmx):
    n_rows = xs.shape[0]
    wmap = lambda i, te, tv: (te[i], 0, 0)
    row = lambda i, te, tv: (i, 0)
    d_ff = w_gate.shape[2]
    grid_spec = pltpu.PrefetchScalarGridSpec(
        num_scalar_prefetch=2,
        grid=(n_rows // tmx,),
        in_specs=[
            pl.BlockSpec((tmx, MOE_W), row),
            pl.BlockSpec((None, D_MODEL, d_ff), wmap), pl.BlockSpec((None, 1, d_ff), wmap),
            pl.BlockSpec((None, D_MODEL, d_ff), wmap), pl.BlockSpec((None, 1, d_ff), wmap),
            pl.BlockSpec((None, d_ff, D_MODEL), wmap), pl.BlockSpec((None, 1, D_MODEL), wmap),
        ],
        out_specs=pl.BlockSpec((tmx, D_MODEL), row),
        scratch_shapes=[pltpu.VMEM((D_MODEL, d_ff), BF16), pltpu.VMEM((D_MODEL, d_ff), BF16),
                        pltpu.VMEM((d_ff, D_MODEL), BF16)],
    )
    return pl.pallas_call(
        _expert_kernel,
        grid_spec=grid_spec,
        out_shape=jax.ShapeDtypeStruct((n_rows, D_MODEL), F32),
        compiler_params=pltpu.CompilerParams(dimension_semantics=("arbitrary",),
                                             vmem_limit_bytes=VMEM_LIMIT),
        name="experts",
    )(tile_expert, tile_valid, xs, w_gate, b_gate.reshape(N_EXPERTS, 1, d_ff), w_up,
      b_up.reshape(N_EXPERTS, 1, d_ff), w_down, b_down.reshape(N_EXPERTS, 1, D_MODEL))


def _combine_kernel(seg_ref, lpos_ref, r1_ref, p_ref, ys_ref, wpg_ref, wpp_ref, nfin_ref, o_ref,
                    buf_ref, sem):
    @pl.when(pl.program_id(0) == 0)
    def _():
        buf_ref[...] = jnp.zeros_like(buf_ref)

    def make_copy(s, d, size):
        return pltpu.make_async_copy(ys_ref.at[pl.ds(d, size)], buf_ref.at[pl.ds(s, size)], sem)

    def start(e, carry):
        _segment_copies(seg_ref, e, make_copy, lambda cp: cp.start())
        return carry

    def wait(e, carry):
        _segment_copies(seg_ref, e, make_copy, lambda cp: cp.wait())
        return carry

    lax.fori_loop(0, N_EXPERTS, start, 0)
    coli = lax.broadcasted_iota(I32, (MOE_TT, MOE_NL), 1)
    lpos = lpos_ref[...]
    pick = jnp.zeros((MOE_TT, MOE_NL), F32)
    for kk in range(TOP_K):
        pick = jnp.where(coli == lpos[:, kk:kk + 1], 1.0, pick)
    lax.fori_loop(0, N_EXPERTS, wait, 0)
    moe = _dot(pick.astype(BF16), buf_ref[...].astype(BF16))
    r2 = r1_ref[...] + moe
    gate = jax.nn.sigmoid(_dot(r2.astype(BF16), wpg_ref[...]))
    r3 = r2 + gate * _dot(p_ref[...].astype(BF16), wpp_ref[...])
    o_ref[...] = _rms(r3, nfin_ref[...])


def _combine(seg, lpos, r1, p2, ys, w_ple_gate, w_ple_proj, norm_final):
    t = r1.shape[0]
    row = lambda i: (i, 0)
    const = lambda i: (0, 0)
    return pl.pallas_call(
        _combine_kernel,
        grid=(t // MOE_TT,),
        in_specs=[
            pl.BlockSpec((None, 1, 3 * N_EXPERTS), lambda i: (i, 0, 0), memory_space=pltpu.SMEM),
            pl.BlockSpec((MOE_TT, LANES), row),
            pl.BlockSpec((MOE_TT, D_MODEL), row),
            pl.BlockSpec((MOE_TT, PLE_DIM), row),
            pl.BlockSpec(memory_space=pl.ANY),
            pl.BlockSpec((D_MODEL, D_MODEL), const), pl.BlockSpec((PLE_DIM, D_MODEL), const),
            pl.BlockSpec((1, D_MODEL), const),
        ],
        out_specs=pl.BlockSpec((MOE_TT, D_MODEL), row),
        out_shape=jax.ShapeDtypeStruct((t, D_MODEL), F32),
        scratch_shapes=[pltpu.VMEM((MOE_NL, D_MODEL), F32), pltpu.SemaphoreType.DMA(())],
        compiler_params=pltpu.CompilerParams(dimension_semantics=("arbitrary",),
                                             vmem_limit_bytes=VMEM_LIMIT),
        name="combine",
    )(seg, lpos, r1, p2, ys, w_ple_gate.astype(BF16), w_ple_proj.astype(BF16),
      norm_final.reshape(1, D_MODEL))


def _moe(h2, route, r1, p2, w_gate, b_gate, w_up, b_up, w_down, b_down, w_ple_gate, w_ple_proj,
         norm_final, tm_prefix, tmx):
    t = h2.shape[0]
    nt = t // MOE_TT
    pre, tot = _prefix(route, tm_prefix)
    pre_i = pre[:, :N_EXPERTS].astype(I32)
    counts = tot[0, :N_EXPERTS].astype(I32)
    tile_pre = pre_i[::MOE_TT]
    tile_cnt = jnp.concatenate([tile_pre[1:], counts[None]], axis=0) - tile_pre
    cnt_u = (tile_cnt + MOE_SEG - 1) // MOE_SEG
    loc_u = jnp.cumsum(cnt_u, axis=1) - cnt_u
    reg_u = jnp.sum(cnt_u, axis=0)
    upt = tmx // MOE_SEG
    pad_u = ((reg_u + upt - 1) // upt) * upt
    end_u = jnp.cumsum(pad_u)
    glob_u = (end_u - pad_u)[None, :] + jnp.cumsum(cnt_u, axis=0) - cnt_u
    seg = jnp.concatenate([cnt_u, loc_u, glob_u], axis=1).astype(I32).reshape(nt, 1, 3 * N_EXPERTS)
    n_tiles = (t * TOP_K + nt * N_EXPERTS * (MOE_SEG - 1)) // tmx + N_EXPERTS
    ends = end_u * MOE_SEG
    tile_start = jnp.arange(n_tiles, dtype=I32) * tmx
    tile_valid = (tile_start < ends[-1]).astype(I32)
    tile_expert = jnp.minimum(jnp.sum((tile_start[:, None] >= ends[None, :]).astype(I32), axis=1), N_EXPERTS - 1)
    last_e = jnp.max(jnp.where(tile_valid != 0, tile_expert, 0))
    tile_expert = jnp.where(tile_valid != 0, tile_expert, last_e)
    n_spare = n_tiles - (t * TOP_K) // tmx
    used = ends[-1] // tmx
    last_tile = jnp.where(pad_u > 0, ends // tmx - 1, -1)
    spare = used + jnp.arange(n_spare, dtype=I32)
    zero_tiles = jnp.concatenate([last_tile, jnp.where(spare < n_tiles, spare, -1)]).astype(I32)
    top_idx = route[:, :TOP_K].astype(I32)
    tok_tile = jnp.arange(t, dtype=I32) // MOE_TT
    base = (loc_u * MOE_SEG - tile_pre)[tok_tile]
    lpos = jnp.take_along_axis(base + pre_i, top_idx, axis=1)
    lpos_t = jnp.pad(lpos.reshape(nt, MOE_TT, TOP_K).transpose(0, 2, 1), ((0, 0), (0, 8 - TOP_K), (0, 0)),
                     constant_values=-1)
    lpos_l = jnp.pad(lpos, ((0, 0), (0, LANES - TOP_K)), constant_values=-1)
    w = route[:, TOP_K:2 * TOP_K]
    w_hi = w.astype(BF16)
    w_mid = (w - w_hi.astype(F32)).astype(BF16)
    w_lo = (w - w_hi.astype(F32) - w_mid.astype(F32)).astype(BF16)
    wtok = jnp.concatenate([route[:, :TOP_K].astype(BF16), w_hi, w_mid, w_lo,
                            jnp.zeros((t, LANES - 4 * TOP_K), BF16)], axis=1)
    xs = _dispatch(zero_tiles, seg, lpos_t, h2, wtok, n_tiles * tmx, tmx)
    ys = _experts(tile_expert, tile_valid, xs, w_gate, b_gate, w_up, b_up, w_down, b_down, tmx)
    return _combine(seg, lpos_l, r1, p2, ys, w_ple_gate, w_ple_proj, norm_final)


def kernel(x, p, w_in, b_gates, lam_re, lam_im, log_dt, b_re, b_im, c_re, c_im, d_skip, w_glu, b_glu,
           w_ssm_br, w_attn_br, w_o, norm_mix, norm_ffn, w_router, b_router, w_gate, b_gate, w_up, b_up,
           w_down, b_down, w_ple_gate, w_ple_proj, norm_final):
    bsz, seq, _ = x.shape
    t = bsz * seq
    x2 = x.reshape(t, D_MODEL)
    u4, q, k, v, qi, ki8, wi, gs, ga = _inproj(x2, norm_mix[0], w_in[0], b_gates[0], seq, tm=256)
    mats = _ssm_mats(lam_re[0], lam_im[0], log_dt[0], b_re[0], b_im[0], c_re[0], c_im[0], d_skip[0])
    y4 = _ssm(u4, mats, bsz, seq)
    y_attn = _attn(q, k, v, qi, ki8, wi, bsz, seq, tq=256, tk=512)
    r1, h2, route = _mix(y4, y_attn, gs, ga, x2, w_glu[0], b_glu[0], w_ssm_br[0], w_attn_br[0], w_o[0],
                         norm_ffn[0], w_router[0], b_router[0], tm=256)
    out = _moe(h2, route, r1, p[0].reshape(t, PLE_DIM), w_gate[0], b_gate[0], w_up[0], b_up[0], w_down[0],
               b_down[0], w_ple_gate[0], w_ple_proj[0], norm_final, tm_prefix=512, tmx=512)
    return out.reshape(bsz, seq, D_MODEL)
```

```python
import functools
import math

import jax
import jax.numpy as jnp
from jax import lax
from jax.experimental import pallas as pl
from jax.experimental.pallas import tpu as pltpu

F32 = jnp.float32
BF16 = jnp.bfloat16
I32 = jnp.int32

D_MODEL = 1024
CHUNK = 64
PLE_DIM = 256
EPS = 1e-6
SSM_WIDTH = 512
SSM_GROUP = 16
SSM_GROUPS = 32
SSM_STATE = 64
N_HEADS = 8
HEAD_DIM = 64
ATTN_WIDTH = 512
IDX_HEADS = 8
IDX_DIM = 32
TOPK_MAX = 256
ROPE_THETA = 10000.0
N_EXPERTS = 32
TOP_K = 4
SWIGLU_LIMIT = 7.0
SWIGLU_ALPHA = 1.702

LANES = 128
SSM_TC = 16
SSM_GL = LANES // SSM_GROUP
SSM_NLB = SSM_WIDTH // LANES
NEG_BIG = -1e30
LOG2E = 1.4426950408889634
INT_MIN = -2147483648
VMEM_LIMIT = 56 * 1024 * 1024
MOE_TT = 512
MOE_SEG = 8
MOE_NL = MOE_TT * TOP_K + N_EXPERTS * MOE_SEG
MOE_W = D_MODEL + LANES
_SEG_SIZES = tuple(MOE_SEG << b for b in range((MOE_TT // MOE_SEG).bit_length() - 1, -1, -1))

_C_U, _C_Q, _C_K, _C_V, _C_QI, _C_KI, _C_WI, _C_GS, _C_GA = 0, 512, 1024, 1536, 2048, 2304, 2560, 2688, 3712
_C_END = 4736


def _rms(x, g):
    return x * lax.rsqrt(jnp.mean(x * x, axis=-1, keepdims=True) + EPS) * g


def _dot(a, b):
    return jnp.dot(a, b, preferred_element_type=F32)


def _dot_nt(a, b):
    return lax.dot_general(a, b, (((1,), (1,)), ((), ())), preferred_element_type=F32)


def _inproj_kernel(x_ref, g_ref, w_ref, bg_ref, cq_ref, sq_ref, ci_ref, si_ref,
                   u_ref, q_ref, k_ref, v_ref, qi_ref, ki_ref, wi_ref, gs_ref, ga_ref):
    h = _rms(x_ref[...], g_ref[...]).astype(BF16)

    def mm(c0, n):
        return _dot(h, w_ref[:, c0:c0 + n])

    u = mm(_C_U, 512)
    for lb in range(SSM_NLB):
        u_ref[lb] = u[:, lb * LANES:(lb + 1) * LANES]
    def rope(z, cos, sin, d):
        n = z.shape[1]
        lane = lax.broadcasted_iota(I32, z.shape, 1)
        partner = jnp.where((lane & (d - 1)) < d // 2, pltpu.roll(z, n - d // 2, 1), pltpu.roll(z, d // 2, 1))
        return z * cos + partner * sin

    cq = cq_ref[...]
    sq = sq_ref[...]
    q_ref[...] = (rope(mm(_C_Q, 512), cq, sq, HEAD_DIM) * (HEAD_DIM ** -0.5 * LOG2E)).astype(BF16)
    k_ref[...] = rope(mm(_C_K, 512), cq, sq, HEAD_DIM).astype(BF16)
    v_ref[...] = mm(_C_V, 512).astype(BF16)
    ci = ci_ref[...]
    si = si_ref[...]
    qi_ref[...] = rope(mm(_C_QI, 256), ci, si, IDX_DIM).astype(BF16)
    ki_ref[...] = rope(mm(_C_KI, 256), ci, si, IDX_DIM).astype(BF16)
    wi_ref[...] = mm(_C_WI, 128) * ((IDX_HEADS * IDX_DIM) ** -0.5)
    bg = bg_ref[...]
    gs_ref[...] = jax.nn.sigmoid(mm(_C_GS, 1024) + bg[:, :1024]).astype(BF16)
    ga_ref[...] = jax.nn.sigmoid(mm(_C_GA, 1024) + bg[:, 1024:]).astype(BF16)


def _rope_tables(seq, d, reps):
    half = d // 2
    inv = ROPE_THETA ** (-jnp.arange(half, dtype=F32) * 2.0 / d)
    ang = jnp.arange(seq, dtype=F32)[:, None] * inv[None, :]
    cos, sin = jnp.cos(ang), jnp.sin(ang)
    c = jnp.concatenate([cos, cos], axis=-1)
    s = jnp.concatenate([-sin, sin], axis=-1)
    return jnp.tile(c, (1, reps)), jnp.tile(s, (1, reps))


def _inproj(x2, norm_mix, w_in, b_gates, seq, tm):
    t = x2.shape[0]
    w_u, w_q, w_k, w_v, w_qi, w_ki, w_wi, w_gs, w_ga = jnp.split(
        w_in, [512, 1024, 1536, 2048, 2304, 2336, 2344, 3368], axis=1)
    w_ki8 = jnp.tile(w_ki, (1, IDX_HEADS))
    w_wi_p = jnp.pad(w_wi, ((0, 0), (0, LANES - IDX_HEADS)))
    w_all = jnp.concatenate([w_u, w_q, w_k, w_v, w_qi, w_ki8, w_wi_p, w_gs, w_ga], axis=1).astype(BF16)
    assert w_all.shape[1] == _C_END
    cq, sq = _rope_tables(seq, HEAD_DIM, N_HEADS)
    ci, si = _rope_tables(seq, IDX_DIM, IDX_HEADS)
    nt = seq // tm
    row = lambda i: (i, 0)
    pos = lambda i: (i % nt, 0)
    const = lambda i: (0, 0)
    outs = pl.pallas_call(
        _inproj_kernel,
        grid=(t // tm,),
        in_specs=[
            pl.BlockSpec((tm, D_MODEL), row),
            pl.BlockSpec((1, D_MODEL), const),
            pl.BlockSpec((D_MODEL, _C_END), const),
            pl.BlockSpec((1, 2 * D_MODEL), const),
            pl.BlockSpec((tm, 512), pos), pl.BlockSpec((tm, 512), pos),
            pl.BlockSpec((tm, 256), pos), pl.BlockSpec((tm, 256), pos),
        ],
        out_specs=[
            pl.BlockSpec((SSM_NLB, tm, LANES), lambda i: (0, i, 0)),
            pl.BlockSpec((tm, 512), row), pl.BlockSpec((tm, 512), row), pl.BlockSpec((tm, 512), row),
            pl.BlockSpec((tm, 256), row), pl.BlockSpec((tm, 256), row), pl.BlockSpec((tm, LANES), row),
            pl.BlockSpec((tm, D_MODEL), row), pl.BlockSpec((tm, D_MODEL), row),
        ],
        out_shape=[
            jax.ShapeDtypeStruct((SSM_NLB, t, LANES), F32),
            jax.ShapeDtypeStruct((t, 512), BF16), jax.ShapeDtypeStruct((t, 512), BF16),
            jax.ShapeDtypeStruct((t, 512), BF16),
            jax.ShapeDtypeStruct((t, 256), BF16), jax.ShapeDtypeStruct((t, 256), BF16),
            jax.ShapeDtypeStruct((t, LANES), F32),
            jax.ShapeDtypeStruct((t, D_MODEL), BF16), jax.ShapeDtypeStruct((t, D_MODEL), BF16),
        ],
        compiler_params=pltpu.CompilerParams(dimension_semantics=("arbitrary",),
                                             vmem_limit_bytes=VMEM_LIMIT),
        name="inproj",
    )(x2, norm_mix.reshape(1, D_MODEL), w_all, b_gates.reshape(1, 2 * D_MODEL), cq, sq, ci, si)
    return outs


def _ssm_mats(lam_re, lam_im, log_dt, b_re, b_im, c_re, c_im, d_skip):
    g_, p_, h_, tc, gl, nlb = SSM_GROUPS, SSM_STATE, SSM_GROUP, SSM_TC, SSM_GL, SSM_NLB
    lam = lax.complex(lam_re.astype(F32), lam_im.astype(F32))
    dt = jnp.exp(log_dt.astype(F32))[:, None]
    lam_dt = lam * dt
    lam_bar = jnp.exp(lam_dt)
    b_bar = ((lam_bar - 1.0) / lam)[..., None] * lax.complex(b_re.astype(F32), b_im.astype(F32))
    c = lax.complex(c_re.astype(F32), c_im.astype(F32))
    steps = jnp.arange(tc + 1, dtype=F32)
    pw = jnp.exp(lam_dt[None] * steps[:, None, None])
    hp = lax.Precision.HIGHEST
    lane_g = jnp.arange(LANES) // h_
    st_g = (jnp.arange(2 * gl * p_) % (gl * p_)) // p_
    rep_o = (jnp.arange(h_)[:, None] == (jnp.arange(LANES) % h_)[None, :]).astype(F32)
    st_col = (jnp.arange(2 * gl * p_) // (gl * p_)) * p_ + jnp.arange(2 * gl * p_) % p_
    rep_s = (jnp.arange(2 * p_)[:, None] == st_col[None, :]).astype(F32)
    taps = jnp.einsum('gop,tgp,gpi->gtoi', c, pw[:tc], b_bar).real
    a3 = taps.reshape(nlb, gl, tc, h_, h_).transpose(0, 2, 1, 4, 3).reshape(nlb, tc, LANES, h_)
    d = jnp.einsum('btro,oc->btrc', a3, rep_o, precision=hp)
    d = d * (lane_g[:, None] == lane_g[None, :]).astype(F32)
    dcat = d.transpose(0, 2, 1, 3).reshape(nlb, LANES, tc * LANES)
    m_intra = jnp.stack(
        [jnp.pad(dcat[:, :, :(tc - j) * LANES], ((0, 0), (0, 0), (j * LANES, 0))) for j in range(tc)],
        axis=1).reshape(nlb, tc * LANES, tc * LANES)
    sc = pw[:tc][::-1][:, :, :, None] * b_bar[None]
    sc = jnp.stack([sc.real, sc.imag], axis=0).reshape(2, tc, nlb, gl, p_, h_)
    a_s = sc.transpose(2, 1, 3, 5, 0, 4).reshape(nlb, tc, LANES, 2 * p_)
    m_state = jnp.einsum('bjrq,qc->bjrc', a_s, rep_s, precision=hp)
    m_state = (m_state * (lane_g[:, None] == st_g[None, :]).astype(F32)).reshape(nlb, tc * LANES, 2 * gl * p_)
    oc = c[None] * pw[1:tc + 1][:, :, None, :]
    oc = jnp.stack([oc.real, -oc.imag], axis=0).reshape(2, tc, nlb, gl, h_, p_)
    a_o = oc.transpose(2, 1, 0, 3, 5, 4).reshape(nlb, tc, 2 * gl * p_, h_)
    m_out = jnp.einsum('blro,oc->blrc', a_o, rep_o, precision=hp)
    m_out = m_out * (st_g[:, None] == lane_g[None, :]).astype(F32)
    m_out = m_out.transpose(0, 2, 1, 3).reshape(nlb, 2 * gl * p_, tc * LANES)
    a = pw[tc].reshape(nlb, 1, gl * p_)
    d_l = jnp.tile(d_skip.astype(F32).reshape(nlb, 1, LANES), (1, 1, tc))
    return (m_intra.astype(BF16), m_state.astype(BF16), m_out.astype(BF16),
            a.real.astype(F32), a.imag.astype(F32), d_l)


def _ssm_kernel(u_ref, mi_ref, ms_ref, mo_ref, are_ref, aim_ref, d_ref, y_ref, con_ref, sp_ref):
    nch = u_ref.shape[0] // SSM_TC
    ns = are_ref.shape[1]
    uf = jnp.concatenate([u_ref[pl.ds(j, nch, stride=SSM_TC), :] for j in range(SSM_TC)], axis=1)
    u = uf.astype(BF16)
    con_ref[...] = _dot(u, ms_ref[...])
    a_re = are_ref[...]
    a_im = aim_ref[...]

    def step(c, carry):
        s_re, s_im = carry
        sp_ref[pl.ds(c, 1), :ns] = s_re
        sp_ref[pl.ds(c, 1), ns:] = s_im
        c_re = con_ref[pl.ds(c, 1), :ns]
        c_im = con_ref[pl.ds(c, 1), ns:]
        return (a_re * s_re - a_im * s_im + c_re, a_re * s_im + a_im * s_re + c_im)

    zero = jnp.zeros((1, ns), F32)
    lax.fori_loop(0, nch, step, (zero, zero))
    y = _dot(u, mi_ref[...]) + _dot(sp_ref[...].astype(BF16), mo_ref[...]) + d_ref[...] * uf
    for l in range(SSM_TC):
        y_ref[pl.ds(l, nch, stride=SSM_TC), :] = y[:, l * LANES:(l + 1) * LANES]


def _ssm(u4, mats, bsz, seq):
    m_intra, m_state, m_out, a_re, a_im, d_l = mats
    t = u4.shape[1]
    nch = seq // SSM_TC
    wc = SSM_TC * LANES
    ns = SSM_GL * SSM_STATE
    wmap = lambda lb, b: (lb, 0, 0)
    return pl.pallas_call(
        _ssm_kernel,
        grid=(SSM_NLB, bsz),
        in_specs=[
            pl.BlockSpec((None, seq, LANES), lambda lb, b: (lb, b, 0)),
            pl.BlockSpec((None, wc, wc), wmap),
            pl.BlockSpec((None, wc, 2 * ns), wmap),
            pl.BlockSpec((None, 2 * ns, wc), wmap),
            pl.BlockSpec((None, 1, ns), wmap), pl.BlockSpec((None, 1, ns), wmap),
            pl.BlockSpec((None, 1, wc), wmap),
        ],
        out_specs=pl.BlockSpec((None, seq, LANES), lambda lb, b: (lb, b, 0)),
        out_shape=jax.ShapeDtypeStruct((SSM_NLB, t, LANES), F32),
        scratch_shapes=[pltpu.VMEM((nch, 2 * ns), F32), pltpu.VMEM((nch, 2 * ns), F32)],
        compiler_params=pltpu.CompilerParams(dimension_semantics=("arbitrary", "arbitrary"),
                                             vmem_limit_bytes=VMEM_LIMIT),
        name="ssm",
    )(u4, m_intra, m_state, m_out, a_re, a_im, d_l)


def _attn_kernel(q_ref, k_ref, v_ref, qi_ref, ki_ref, wi_ref, tri_ref, o_ref, keys_ref, kt_ref, k16_ref,
                 qh_ref, qm_ref, acc_ref, m_ref, *, tq, tk, seq, topk):
    i = pl.program_id(1)
    s0 = i * tq
    nk = (s0 + tq + tk - 1) // tk
    row = s0 + lax.broadcasted_iota(I32, (tq, 1), 0)
    lim = ((row >> 6) + 1) << 6
    lim_q = (((s0 + lax.broadcasted_iota(I32, (1, tq), 1)) >> 6) + 1) << 6
    kvec = jnp.minimum(lim_q, topk).astype(F32)
    col0 = lax.broadcasted_iota(I32, (tq, tk), 1)

    qi = qi_ref[...]
    lane_i = lax.broadcasted_iota(I32, qi.shape, 1)
    for h in range(IDX_HEADS):
        qh_ref[h] = jnp.where((lane_i >> 5) == h, qi, jnp.zeros_like(qi))
    wi = wi_ref[...]
    wcol = [wi[:, h:h + 1] for h in range(IDX_HEADS)]

    def score_tile(j, carry):
        kt = ki_ref[pl.ds(pl.multiple_of(j * tk, tk), tk), :]
        acc = jnp.zeros((tq, tk), F32)
        for h in range(IDX_HEADS):
            acc = acc + wcol[h] * jnp.maximum(_dot_nt(qh_ref[h], kt), 0.0)
        sc = jnp.where(col0 + j * tk < lim, acc, -jnp.inf)
        bits = pltpu.bitcast(sc, I32)
        key = bits ^ ((bits >> 31) & 0x7FFFFFFF)
        keys_ref[j] = key
        key_t = key.T
        kt_ref[j] = key_t
        k16_ref[j] = (key_t >> 16).astype(jnp.int16)
        return carry

    lax.fori_loop(0, nk, score_tile, 0)

    rg = 16
    one16 = jnp.ones((rg, tq), jnp.int16)
    zero16 = jnp.zeros((rg, tq), jnp.int16)

    def count16(cand):
        cand16 = jnp.broadcast_to(cand, (rg, tq)).astype(jnp.int16)

        def body(j, part):
            kk = k16_ref[j]
            for r in range(tk // rg):
                part = part + jnp.where(kk[r * rg:(r + 1) * rg, :] >= cand16, one16, zero16)
            return part
        part = lax.fori_loop(0, nk, body, jnp.zeros((rg, tq), jnp.int16))
        return jnp.sum(part.astype(F32), axis=0, keepdims=True)

    def bit_step(it, carry):
        u_ans, g_lo, g_hi = carry
        cand_u = u_ans | lax.shift_left(jnp.int32(1), 15 - it)
        cnt = count16(cand_u - 32768)
        ok = cnt >= kvec
        return jnp.where(ok, cand_u, u_ans), jnp.where(ok, cnt, g_lo), jnp.where(ok, g_hi, cnt)

    u_ans, g_lo, g_hi = lax.fori_loop(
        0, 16, bit_step, (jnp.zeros((1, tq), I32), lim_q.astype(F32), jnp.zeros((1, tq), F32)))
    b16 = u_ans - 32768

    def rebase(j, carry):
        kk = kt_ref[j]
        hi16 = kk >> 16
        low = (kk & 0xFFFF) - 32768
        rel = jnp.where(hi16 > b16, 32767, jnp.where(hi16 < b16, -32768, low))
        k16_ref[j] = rel.astype(jnp.int16)
        return carry

    lax.fori_loop(0, nk, rebase, 0)

    def unfinished(lo, hi, g_lo):
        return jnp.where(g_lo != kvec, jnp.where(hi - lo > 1, 1.0, 0.0), 0.0)

    def refine_cond(carry):
        return (carry[0] < 18) & (carry[1] > 0.0)

    def refine(carry):
        it, _, lo, hi, g_lo, g_hi = carry
        todo = unfinished(lo, hi, g_lo)
        mid = lo + ((hi - lo) >> 1)
        cand = jnp.where(b16 == 0, jnp.where(hi - lo == 65536, 1, mid), mid)
        cnt = count16(cand - 32768)
        up = jnp.where(cnt >= kvec, todo, 0.0) > 0.0
        dn = jnp.where(cnt < kvec, todo, 0.0) > 0.0
        lo = jnp.where(up, cand, lo)
        g_lo = jnp.where(up, cnt, g_lo)
        hi = jnp.where(dn, cand, hi)
        g_hi = jnp.where(dn, cnt, g_hi)
        return it + 1, jnp.max(unfinished(lo, hi, g_lo)), lo, hi, g_lo, g_hi

    off_lo = jnp.zeros((1, tq), I32)
    off_hi = jnp.full((1, tq), 65536, I32)
    _, _, off_lo, _, g_lo, g_hi = lax.while_loop(
        refine_cond, refine,
        (jnp.int32(0), jnp.max(unfinished(off_lo, off_hi, g_lo)), off_lo, off_hi, g_lo, g_hi))
    has_tie = jnp.max(g_lo - kvec) > 0.0

    def to_rows(x):
        return jnp.transpose(jnp.broadcast_to(x, (LANES, tq)))

    vth = to_rows((b16 << 16) + off_lo)

    @pl.when(jnp.logical_not(has_tie))
    def _():
        def to_bias(j, carry):
            kk = keys_ref[j]
            for c in range(tk // LANES):
                sl = slice(c * LANES, (c + 1) * LANES)
                keys_ref[j, :, sl] = pltpu.bitcast(jnp.where(kk[:, sl] >= vth, 0.0, NEG_BIG).astype(F32), I32)
            return carry
        lax.fori_loop(0, nk, to_bias, 0)

    @pl.when(has_tie)
    def _():
        need = to_rows(kvec - g_hi)

        def to_bias(j, seen):
            kk = keys_ref[j]
            eq = jnp.concatenate(
                [jnp.where(kk[:, c * LANES:(c + 1) * LANES] == vth, 1.0, 0.0) for c in range(tk // LANES)],
                axis=1)
            rank = _dot(eq.astype(BF16), tri_ref[...])
            for c in range(tk // LANES):
                sl = slice(c * LANES, (c + 1) * LANES)
                take = jnp.where(seen + rank[:, sl] <= need, eq[:, sl], 0.0)
                sel = jnp.where(kk[:, sl] > vth, 1.0, take)
                keys_ref[j, :, sl] = pltpu.bitcast(jnp.where(sel > 0.0, 0.0, NEG_BIG).astype(F32), I32)
            return seen + jnp.broadcast_to(rank[:, tk - 1:tk], (tq, LANES))
        lax.fori_loop(0, nk, to_bias, jnp.zeros((tq, LANES), F32))

    lane_q = lax.broadcasted_iota(I32, (tq, LANES), 1)
    lane_k = lax.broadcasted_iota(I32, (tk, LANES), 1)
    for h in range(N_HEADS):
        qp = q_ref[:, (h // 2) * LANES:(h // 2 + 1) * LANES]
        keep = (lane_q >= HEAD_DIM) if h % 2 else (lane_q < HEAD_DIM)
        qm_ref[h] = jnp.where(keep, qp, jnp.zeros_like(qp))
    acc_ref[...] = jnp.zeros_like(acc_ref)
    m_ref[...] = jnp.full(m_ref.shape, NEG_BIG, F32)
    ones_v = jnp.ones((tk, LANES), BF16)

    def attn_tile(j, carry):
        rows = pl.ds(pl.multiple_of(j * tk, tk), tk)
        bias = pltpu.bitcast(keys_ref[j], F32)
        for h in range(N_HEADS):
            cs = slice((h // 2) * LANES, (h // 2 + 1) * LANES)
            own = (lane_k >= HEAD_DIM) if h % 2 else (lane_k < HEAD_DIM)
            s = _dot_nt(qm_ref[h], k_ref[rows, cs]) + bias
            m_old = m_ref[h]
            m_new = jnp.maximum(m_old, jnp.max(s, axis=1, keepdims=True))
            p = jnp.concatenate(
                [jnp.exp2(s[:, c * LANES:(c + 1) * LANES] - m_new) for c in range(tk // LANES)],
                axis=1).astype(BF16)
            vh = jnp.where(own, v_ref[rows, cs], ones_v)
            acc_ref[h] = jnp.exp2(m_old - m_new) * acc_ref[h] + _dot(p, vh)
            m_ref[h] = m_new
        return carry

    lax.fori_loop(0, nk, attn_tile, 0)

    for hp in range(N_HEADS // 2):
        a0 = acc_ref[2 * hp]
        a1 = acc_ref[2 * hp + 1]
        out = jnp.where(lane_q < HEAD_DIM, a0 / pltpu.roll(a0, HEAD_DIM, 1), a1 / pltpu.roll(a1, HEAD_DIM, 1))
        o_ref[:, hp * LANES:(hp + 1) * LANES] = out.astype(o_ref.dtype)


def _attn(q, k, v, qi, ki8, wi, bsz, seq, tq, tk):
    t = q.shape[0]
    nq = seq // tq
    qrow = lambda b, i: (b * nq + i, 0)
    kv = lambda b, i: (b, 0)
    kern = functools.partial(_attn_kernel, tq=tq, tk=tk, seq=seq, topk=min(TOPK_MAX, seq // 4))
    tri = (jnp.arange(tk)[:, None] <= jnp.arange(tk)[None, :]).astype(BF16)
    return pl.pallas_call(
        kern,
        grid=(bsz, nq),
        in_specs=[
            pl.BlockSpec((tq, ATTN_WIDTH), qrow),
            pl.BlockSpec((seq, ATTN_WIDTH), kv), pl.BlockSpec((seq, ATTN_WIDTH), kv),
            pl.BlockSpec((tq, 256), qrow), pl.BlockSpec((seq, 256), kv),
            pl.BlockSpec((tq, LANES), qrow),
            pl.BlockSpec((tk, tk), lambda b, i: (0, 0)),
        ],
        out_specs=pl.BlockSpec((tq, ATTN_WIDTH), qrow),
        out_shape=jax.ShapeDtypeStruct((t, ATTN_WIDTH), BF16),
        scratch_shapes=[
            pltpu.VMEM((seq // tk, tq, tk), I32),
            pltpu.VMEM((seq // tk, tk, tq), I32),
            pltpu.VMEM((seq // tk, tk, tq), jnp.int16),
            pltpu.VMEM((IDX_HEADS, tq, 256), BF16),
            pltpu.VMEM((N_HEADS, tq, LANES), BF16),
            pltpu.VMEM((N_HEADS, tq, LANES), F32),
            pltpu.VMEM((N_HEADS, tq, LANES), F32),
        ],
        compiler_params=pltpu.CompilerParams(dimension_semantics=("arbitrary", "arbitrary"),
                                             vmem_limit_bytes=VMEM_LIMIT),
        name="attn",
    )(q, k, v, qi, ki8, wi, tri)


def _mix_kernel(y4_ref, ya_ref, gs_ref, ga_ref, x_ref, wglu_ref, bglu_ref, wsb_ref, wab_ref, wo_ref,
                nf_ref, wrh_ref, wrl_ref, br_ref, r1_ref, h2_ref, route_ref):
    ys = jnp.concatenate([y4_ref[lb] for lb in range(SSM_NLB)], axis=1)
    ys = 0.5 * ys * (1.0 + jnp.tanh(math.sqrt(2.0 / math.pi) * (ys + 0.044715 * (ys * ys * ys))))
    ys = ys * jax.nn.sigmoid(_dot(ys.astype(BF16), wglu_ref[...]) + bglu_ref[...])
    br_s = _dot(ys.astype(BF16), wsb_ref[...])
    br_a = _dot(ya_ref[...], wab_ref[...])
    mix = gs_ref[...].astype(F32) * br_s + ga_ref[...].astype(F32) * br_a
    r1 = x_ref[...] + _dot(mix.astype(BF16), wo_ref[...])
    r1_ref[...] = r1
    h2 = _rms(r1, nf_ref[...])
    h2_ref[...] = h2
    hi = h2.astype(BF16)
    lo = (h2 - hi.astype(F32)).astype(BF16)
    logits = _dot(hi, wrh_ref[...]) + (_dot(hi, wrl_ref[...]) + _dot(lo, wrh_ref[...])) + br_ref[...]
    tm = logits.shape[0]
    lane = lax.broadcasted_iota(I32, (tm, LANES), 1).astype(F32)
    vals = logits
    tops, idxs = [], []
    for _ in range(TOP_K):
        m = jnp.max(vals, axis=1, keepdims=True)
        idx = jnp.min(jnp.where(vals == m, lane, float(LANES)), axis=1, keepdims=True)
        tops.append(m)
        idxs.append(idx)
        vals = jnp.where(lane == idx, -jnp.inf, vals)
    es = [jnp.exp(tv - tops[0]) for tv in tops]
    den = es[0] + es[1] + es[2] + es[3]
    route = jnp.zeros((tm, LANES), F32)
    for kk in range(TOP_K):
        w = es[kk] / den
        w_hi = w.astype(BF16).astype(F32)
        w_mid = (w - w_hi).astype(BF16).astype(F32)
        w_lo = (w - w_hi) - w_mid
        route = jnp.where(lane == float(kk), idxs[kk], route)
        route = jnp.where(lane == float(TOP_K + kk), w_hi, route)
        route = jnp.where(lane == float(2 * TOP_K + kk), w_mid, route)
        route = jnp.where(lane == float(3 * TOP_K + kk), w_lo, route)
    route_ref[...] = route.astype(BF16)


def _mix(y4, y_attn, gs, ga, x2, w_glu, b_glu, w_ssm_br, w_attn_br, w_o, norm_ffn, w_router, b_router, tm):
    t = x2.shape[0]
    wr = jnp.pad(w_router.astype(F32), ((0, 0), (0, LANES - N_EXPERTS)))
    wr_hi = wr.astype(BF16)
    wr_lo = (wr - wr_hi.astype(F32)).astype(BF16)
    br = jnp.concatenate([b_router.astype(F32), jnp.full((LANES - N_EXPERTS,), NEG_BIG, F32)]).reshape(1, LANES)
    row = lambda i: (i, 0)
    const = lambda i: (0, 0)
    return pl.pallas_call(
        _mix_kernel,
        grid=(t // tm,),
        in_specs=[
            pl.BlockSpec((SSM_NLB, tm, LANES), lambda i: (0, i, 0)),
            pl.BlockSpec((tm, ATTN_WIDTH), row),
            pl.BlockSpec((tm, D_MODEL), row), pl.BlockSpec((tm, D_MODEL), row), pl.BlockSpec((tm, D_MODEL), row),
            pl.BlockSpec((SSM_WIDTH, SSM_WIDTH), const), pl.BlockSpec((1, SSM_WIDTH), const),
            pl.BlockSpec((SSM_WIDTH, D_MODEL), const), pl.BlockSpec((ATTN_WIDTH, D_MODEL), const),
            pl.BlockSpec((D_MODEL, D_MODEL), const), pl.BlockSpec((1, D_MODEL), const),
            pl.BlockSpec((D_MODEL, LANES), const), pl.BlockSpec((D_MODEL, LANES), const),
            pl.BlockSpec((1, LANES), const),
        ],
        out_specs=[pl.BlockSpec((tm, D_MODEL), row), pl.BlockSpec((tm, D_MODEL), row),
                   pl.BlockSpec((tm, LANES), row)],
        out_shape=[jax.ShapeDtypeStruct((t, D_MODEL), F32), jax.ShapeDtypeStruct((t, D_MODEL), F32),
                   jax.ShapeDtypeStruct((t, LANES), BF16)],
        compiler_params=pltpu.CompilerParams(dimension_semantics=("arbitrary",),
                                             vmem_limit_bytes=VMEM_LIMIT),
        name="mix",
    )(y4, y_attn, gs, ga, x2, w_glu.astype(BF16), b_glu.reshape(1, SSM_WIDTH).astype(F32),
      w_ssm_br.astype(BF16), w_attn_br.astype(BF16), w_o.astype(BF16), norm_ffn.reshape(1, D_MODEL),
      wr_hi, wr_lo, br)


def _prefix_kernel(route_ref, tri_ref, utri_ref, lpos_ref, lpos_t_ref, cnt_ref):
    route = route_ref[...].astype(F32)
    tm = route.shape[0]
    lane = lax.broadcasted_iota(I32, (tm, LANES), 1).astype(F32)
    mask = jnp.zeros((tm, LANES), F32)
    for kk in range(TOP_K):
        mask = jnp.where(lane == route[:, kk:kk + 1], 1.0, mask)
    incl = _dot(tri_ref[...], mask.astype(BF16))
    cnt = incl[tm - 1:tm, :]
    cnt_u = jnp.floor((cnt + (MOE_SEG - 1)) * (1.0 / MOE_SEG))
    loc = _dot(jnp.broadcast_to(cnt_u, (8, LANES)).astype(BF16), utri_ref[...])[:1, :] * float(MOE_SEG)
    where_to = loc + incl - mask
    lpos = jnp.full((tm, LANES), -1.0, F32)
    for kk in range(TOP_K):
        pos = jnp.sum(jnp.where(lane == route[:, kk:kk + 1], where_to, 0.0), axis=1, keepdims=True)
        lpos = jnp.where(lane == float(kk), pos, lpos)
    lpos_ref[...] = lpos.astype(I32)
    lpos_t_ref[...] = jnp.transpose(lpos)[:8, :].astype(I32)
    cnt_ref[...] = jnp.broadcast_to(cnt, (8, LANES)).astype(I32)


def _prefix(route):
    t = route.shape[0]
    tm = MOE_TT
    nt = t // tm
    tri = (jnp.arange(tm)[:, None] >= jnp.arange(tm)[None, :]).astype(BF16)
    utri = (jnp.arange(LANES)[:, None] < jnp.arange(LANES)[None, :]).astype(BF16)
    return pl.pallas_call(
        _prefix_kernel,
        grid=(nt,),
        in_specs=[pl.BlockSpec((tm, LANES), lambda i: (i, 0)), pl.BlockSpec((tm, tm), lambda i: (0, 0)),
                  pl.BlockSpec((LANES, LANES), lambda i: (0, 0))],
        out_specs=[pl.BlockSpec((tm, LANES), lambda i: (i, 0)), pl.BlockSpec((None, 8, tm), lambda i: (i, 0, 0)),
                   pl.BlockSpec((None, 8, LANES), lambda i: (i, 0, 0))],
        out_shape=[jax.ShapeDtypeStruct((t, LANES), I32), jax.ShapeDtypeStruct((nt, 8, tm), I32),
                   jax.ShapeDtypeStruct((nt, 8, LANES), I32)],
        compiler_params=pltpu.CompilerParams(dimension_semantics=("arbitrary",)),
        name="prefix",
    )(route, tri, utri)


def _segment_copies(seg_ref, e, make_copy, fn):
    c = seg_ref[0, e]
    s = seg_ref[0, N_EXPERTS + e]
    d = seg_ref[0, 2 * N_EXPERTS + e]
    for size in _SEG_SIZES:
        units = size // MOE_SEG
        hit = (c & units) != 0

        @pl.when(hit)
        def _():
            fn(make_copy(pl.multiple_of(s * MOE_SEG, MOE_SEG), pl.multiple_of(d * MOE_SEG, MOE_SEG), size))
        s = s + jnp.where(hit, units, 0)
        d = d + jnp.where(hit, units, 0)


def _dispatch_kernel(zt_ref, seg_ref, lpos_ref, h2_ref, wtok_ref, xs_ref, buf_ref, zbuf_ref, sem, zsem):
    tmx = zbuf_ref.shape[0]

    @pl.when(pl.program_id(0) == 0)
    def _():
        zbuf_ref[...] = jnp.zeros_like(zbuf_ref)

        def zero_copy(z):
            return pltpu.make_async_copy(
                zbuf_ref, xs_ref.at[pl.ds(pl.multiple_of(zt_ref[z] * tmx, tmx), tmx)], zsem)

        for z in range(zt_ref.shape[0]):
            @pl.when(zt_ref[z] >= 0)
            def _():
                zero_copy(z).start()
        for z in range(zt_ref.shape[0]):
            @pl.when(zt_ref[z] >= 0)
            def _():
                zero_copy(z).wait()

    rowi = lax.broadcasted_iota(I32, (MOE_NL, MOE_TT), 0)
    sel = jnp.zeros((MOE_NL, MOE_TT), F32)
    for kk in range(TOP_K):
        sel = jnp.where(rowi == lpos_ref[kk:kk + 1, :], 1.0, sel)
    src = jnp.concatenate([h2_ref[...].astype(BF16), wtok_ref[...]], axis=1)
    buf_ref[...] = _dot(sel.astype(BF16), src)

    def make_copy(s, d, size):
        return pltpu.make_async_copy(buf_ref.at[pl.ds(s, size)], xs_ref.at[pl.ds(d, size)], sem)

    def start(e, carry):
        _segment_copies(seg_ref, e, make_copy, lambda cp: cp.start())
        return carry

    def wait(e, carry):
        _segment_copies(seg_ref, e, make_copy, lambda cp: cp.wait())
        return carry

    lax.fori_loop(0, N_EXPERTS, start, 0)
    lax.fori_loop(0, N_EXPERTS, wait, 0)


def _dispatch(zero_tiles, seg, lpos_t, h2, wtok, n_rows, tmx):
    t = h2.shape[0]
    grid_spec = pltpu.PrefetchScalarGridSpec(
        num_scalar_prefetch=1,
        grid=(t // MOE_TT,),
        in_specs=[
            pl.BlockSpec((None, 1, 3 * N_EXPERTS), lambda i, zt: (i, 0, 0), memory_space=pltpu.SMEM),
            pl.BlockSpec((None, 8, MOE_TT), lambda i, zt: (i, 0, 0)),
            pl.BlockSpec((MOE_TT, D_MODEL), lambda i, zt: (i, 0)),
            pl.BlockSpec((MOE_TT, LANES), lambda i, zt: (i, 0)),
        ],
        out_specs=pl.BlockSpec(memory_space=pl.ANY),
        scratch_shapes=[pltpu.VMEM((MOE_NL, MOE_W), F32), pltpu.VMEM((tmx, MOE_W), F32),
                        pltpu.SemaphoreType.DMA(()), pltpu.SemaphoreType.DMA(())],
    )
    return pl.pallas_call(
        _dispatch_kernel,
        grid_spec=grid_spec,
        out_shape=jax.ShapeDtypeStruct((n_rows, MOE_W), F32),
        compiler_params=pltpu.CompilerParams(dimension_semantics=("arbitrary",), has_side_effects=True,
                                             vmem_limit_bytes=VMEM_LIMIT),
        name="dispatch",
    )(zero_tiles, seg, lpos_t, h2, wtok)


def _expert_kernel(te_ref, tv_ref, x_ref, wg_ref, bg_ref, wu_ref, bu_ref, wd_ref, bd_ref, y_ref,
                   wgb_ref, wub_ref, wdb_ref):
    i = pl.program_id(0)
    e = te_ref[i]
    prev = te_ref[jnp.maximum(i - 1, 0)]

    @pl.when((i == 0) | (e != prev))
    def _():
        wgb_ref[...] = wg_ref[...].astype(BF16)
        wub_ref[...] = wu_ref[...].astype(BF16)
        wdb_ref[...] = wd_ref[...].astype(BF16)

    @pl.when(tv_ref[i] != 0)
    def _():
        x = x_ref[:, :D_MODEL].astype(BF16)
        rt = x_ref[:, D_MODEL:]
        ef = e.astype(F32)
        w_row = jnp.zeros((rt.shape[0], 1), F32)
        for kk in range(TOP_K):
            wk = (rt[:, TOP_K + kk:TOP_K + kk + 1] + rt[:, 2 * TOP_K + kk:2 * TOP_K + kk + 1]
                  + rt[:, 3 * TOP_K + kk:3 * TOP_K + kk + 1])
            w_row = w_row + jnp.where(rt[:, kk:kk + 1] == ef, wk, 0.0)
        g = jnp.minimum(_dot(x, wgb_ref[...]) + bg_ref[...], SWIGLU_LIMIT)
        u = jnp.clip(_dot(x, wub_ref[...]) + bu_ref[...], -SWIGLU_LIMIT, SWIGLU_LIMIT)
        a = (u + 1.0) * (g * jax.nn.sigmoid(SWIGLU_ALPHA * g))
        y_ref[...] = (_dot(a.astype(BF16), wdb_ref[...]) + bd_ref[...]) * w_row

    @pl.when(tv_ref[i] == 0)
    def _():
        y_ref[...] = jnp.zeros_like(y_ref)


def _experts(tile_expert, tile_valid, xs, w_gate, b_gate, w_up, b_up, w_down, b_down, tmx):
    n_rows = xs.shape[0]
    wmap = lambda i, te, tv: (te[i], 0, 0)
    row = lambda i, te, tv: (i, 0)
    d_ff = w_gate.shape[2]
    grid_spec = pltpu.PrefetchScalarGridSpec(
        num_scalar_prefetch=2,
        grid=(n_rows // tmx,),
        in_specs=[
            pl.BlockSpec((tmx, MOE_W), row),
            pl.BlockSpec((None, D_MODEL, d_ff), wmap), pl.BlockSpec((None, 1, d_ff), wmap),
            pl.BlockSpec((None, D_MODEL, d_ff), wmap), pl.BlockSpec((None, 1, d_ff), wmap),
            pl.BlockSpec((None, d_ff, D_MODEL), wmap), pl.BlockSpec((None, 1, D_MODEL), wmap),
        ],
        out_specs=pl.BlockSpec((tmx, D_MODEL), row),
        scratch_shapes=[pltpu.VMEM((D_MODEL, d_ff), BF16), pltpu.VMEM((D_MODEL, d_ff), BF16),
                        pltpu.VMEM((d_ff, D_MODEL), BF16)],
    )
    return pl.pallas_call(
        _expert_kernel,
        grid_spec=grid_spec,
        out_shape=jax.ShapeDtypeStruct((n_rows, D_MODEL), F32),
        compiler_params=pltpu.CompilerParams(dimension_semantics=("arbitrary",),
                                             vmem_limit_bytes=VMEM_LIMIT),
        name="experts",
    )(tile_expert, tile_valid, xs, w_gate, b_gate.reshape(N_EXPERTS, 1, d_ff), w_up,
      b_up.reshape(N_EXPERTS, 1, d_ff), w_down, b_down.reshape(N_EXPERTS, 1, D_MODEL))


def _combine_kernel(seg_ref, lpos_ref, r1_ref, p_ref, ys_ref, wpg_ref, wpp_ref, nfin_ref, o_ref,
                    buf_ref, sem):
    @pl.when(pl.program_id(0) == 0)
    def _():
        buf_ref[...] = jnp.zeros_like(buf_ref)

    def make_copy(s, d, size):
        return pltpu.make_async_copy(ys_ref.at[pl.ds(d, size)], buf_ref.at[pl.ds(s, size)], sem)

    def start(e, carry):
        _segment_copies(seg_ref, e, make_copy, lambda cp: cp.start())
        return carry

    def wait(e, carry):
        _segment_copies(seg_ref, e, make_copy, lambda cp: cp.wait())
        return carry

    lax.fori_loop(0, N_EXPERTS, start, 0)
    coli = lax.broadcasted_iota(I32, (MOE_TT, MOE_NL), 1)
    lpos = lpos_ref[...]
    pick = jnp.zeros((MOE_TT, MOE_NL), F32)
    for kk in range(TOP_K):
        pick = jnp.where(coli == lpos[:, kk:kk + 1], 1.0, pick)
    lax.fori_loop(0, N_EXPERTS, wait, 0)
    moe = _dot(pick.astype(BF16), buf_ref[...].astype(BF16))
    r2 = r1_ref[...] + moe
    gate = jax.nn.sigmoid(_dot(r2.astype(BF16), wpg_ref[...]))
    r3 = r2 + gate * _dot(p_ref[...].astype(BF16), wpp_ref[...])
    o_ref[...] = _rms(r3, nfin_ref[...])


def _combine(seg, lpos, r1, p2, ys, w_ple_gate, w_ple_proj, norm_final):
    t = r1.shape[0]
    row = lambda i: (i, 0)
    const = lambda i: (0, 0)
    return pl.pallas_call(
        _combine_kernel,
        grid=(t // MOE_TT,),
        in_specs=[
            pl.BlockSpec((None, 1, 3 * N_EXPERTS), lambda i: (i, 0, 0), memory_space=pltpu.SMEM),
            pl.BlockSpec((MOE_TT, LANES), row),
            pl.BlockSpec((MOE_TT, D_MODEL), row),
            pl.BlockSpec((MOE_TT, PLE_DIM), row),
            pl.BlockSpec(memory_space=pl.ANY),
            pl.BlockSpec((D_MODEL, D_MODEL), const), pl.BlockSpec((PLE_DIM, D_MODEL), const),
            pl.BlockSpec((1, D_MODEL), const),
        ],
        out_specs=pl.BlockSpec((MOE_TT, D_MODEL), row),
        out_shape=jax.ShapeDtypeStruct((t, D_MODEL), F32),
        scratch_shapes=[pltpu.VMEM((MOE_NL, D_MODEL), F32), pltpu.SemaphoreType.DMA(())],
        compiler_params=pltpu.CompilerParams(dimension_semantics=("arbitrary",),
                                             vmem_limit_bytes=VMEM_LIMIT),
        name="combine",
    )(seg, lpos, r1, p2, ys, w_ple_gate.astype(BF16), w_ple_proj.astype(BF16),
      norm_final.reshape(1, D_MODEL))


def _moe(h2, route, r1, p2, w_gate, b_gate, w_up, b_up, w_down, b_down, w_ple_gate, w_ple_proj,
         norm_final, tmx):
    t = h2.shape[0]
    nt = t // MOE_TT
    lpos_l, lpos_t, cnt8 = _prefix(route)
    tile_cnt = cnt8[:, 0, :N_EXPERTS]
    cnt_u = (tile_cnt + MOE_SEG - 1) // MOE_SEG
    loc_u = jnp.cumsum(cnt_u, axis=1) - cnt_u
    reg_u = jnp.sum(cnt_u, axis=0)
    upt = tmx // MOE_SEG
    pad_u = ((reg_u + upt - 1) // upt) * upt
    end_u = jnp.cumsum(pad_u)
    glob_u = (end_u - pad_u)[None, :] + jnp.cumsum(cnt_u, axis=0) - cnt_u
    seg = jnp.concatenate([cnt_u, loc_u, glob_u], axis=1).astype(I32).reshape(nt, 1, 3 * N_EXPERTS)
    n_tiles = (t * TOP_K + nt * N_EXPERTS * (MOE_SEG - 1)) // tmx + N_EXPERTS
    ends = end_u * MOE_SEG
    tile_start = jnp.arange(n_tiles, dtype=I32) * tmx
    tile_valid = (tile_start < ends[-1]).astype(I32)
    tile_expert = jnp.minimum(jnp.sum((tile_start[:, None] >= ends[None, :]).astype(I32), axis=1), N_EXPERTS - 1)
    last_e = jnp.max(jnp.where(tile_valid != 0, tile_expert, 0))
    tile_expert = jnp.where(tile_valid != 0, tile_expert, last_e)
    n_spare = n_tiles - (t * TOP_K) // tmx
    used = ends[-1] // tmx
    last_tile = jnp.where(pad_u > 0, ends // tmx - 1, -1)
    spare = used + jnp.arange(n_spare, dtype=I32)
    zero_tiles = jnp.concatenate([last_tile, jnp.where(spare < n_tiles, spare, -1)]).astype(I32)
    xs = _dispatch(zero_tiles, seg, lpos_t, h2, route, n_tiles * tmx, tmx)
    ys = _experts(tile_expert, tile_valid, xs, w_gate, b_gate, w_up, b_up, w_down, b_down, tmx)
    return _combine(seg, lpos_l, r1, p2, ys, w_ple_gate, w_ple_proj, norm_final)


def kernel(x, p, w_in, b_gates, lam_re, lam_im, log_dt, b_re, b_im, c_re, c_im, d_skip, w_glu, b_glu,
           w_ssm_br, w_attn_br, w_o, norm_mix, norm_ffn, w_router, b_router, w_gate, b_gate, w_up, b_up,
           w_down, b_down, w_ple_gate, w_ple_proj, norm_final):
    bsz, seq, _ = x.shape
    t = bsz * seq
    x2 = x.reshape(t, D_MODEL)
    u4, q, k, v, qi, ki8, wi, gs, ga = _inproj(x2, norm_mix[0], w_in[0], b_gates[0], seq, tm=256)
    mats = _ssm_mats(lam_re[0], lam_im[0], log_dt[0], b_re[0], b_im[0], c_re[0], c_im[0], d_skip[0])
    y4 = _ssm(u4, mats, bsz, seq)
    y_attn = _attn(q, k, v, qi, ki8, wi, bsz, seq, tq=256, tk=min(1024, seq))
    r1, h2, route = _mix(y4, y_attn, gs, ga, x2, w_glu[0], b_glu[0], w_ssm_br[0], w_attn_br[0], w_o[0],
                         norm_ffn[0], w_router[0], b_router[0], tm=256)
    out = _moe(h2, route, r1, p[0].reshape(t, PLE_DIM), w_gate[0], b_gate[0], w_up[0], b_up[0], w_down[0],
               b_down[0], w_ple_gate[0], w_ple_proj[0], norm_final, tmx=512)
    return out.reshape(bsz, seq, D_MODEL)
```

```python
import functools
import math

import jax
import jax.numpy as jnp
from jax import lax
from jax.experimental import pallas as pl
from jax.experimental.pallas import tpu as pltpu

F32 = jnp.float32
BF16 = jnp.bfloat16
I32 = jnp.int32

D_MODEL = 1024
CHUNK = 64
PLE_DIM = 256
EPS = 1e-6
SSM_WIDTH = 512
SSM_GROUP = 16
SSM_GROUPS = 32
SSM_STATE = 64
N_HEADS = 8
HEAD_DIM = 64
ATTN_WIDTH = 512
IDX_HEADS = 8
IDX_DIM = 32
TOPK_MAX = 256
ROPE_THETA = 10000.0
N_EXPERTS = 32
TOP_K = 4
SWIGLU_LIMIT = 7.0
SWIGLU_ALPHA = 1.702

LANES = 128
SSM_TC = 16
SSM_GL = LANES // SSM_GROUP
SSM_NLB = SSM_WIDTH // LANES
NEG_BIG = -1e30
LOG2E = 1.4426950408889634
CHUNK_SHIFT = CHUNK.bit_length() - 1
VMEM_LIMIT = 56 * 1024 * 1024
MOE_TT = 512
MOE_SEG = 8
MOE_NL = MOE_TT * TOP_K + N_EXPERTS * MOE_SEG
MOE_W = D_MODEL + LANES
_SEG_SIZES = tuple(MOE_SEG << b for b in range((MOE_TT // MOE_SEG).bit_length() - 1, -1, -1))

_C_U, _C_Q, _C_K, _C_V, _C_QI, _C_KI, _C_WI, _C_GS, _C_GA = 0, 512, 1024, 1536, 2048, 2304, 2560, 2688, 3712
_C_END = 4736


def _rms(x, g):
    return x * lax.rsqrt(jnp.mean(x * x, axis=-1, keepdims=True) + EPS) * g


def _dot(a, b):
    return jnp.dot(a, b, preferred_element_type=F32)


def _dot_nt(a, b):
    return lax.dot_general(a, b, (((1,), (1,)), ((), ())), preferred_element_type=F32)


def _inproj_kernel(x_ref, g_ref, w_ref, bg_ref, cq_ref, sq_ref, ci_ref, si_ref,
                   u_ref, q_ref, k_ref, v_ref, qi_ref, ki_ref, wi_ref, gs_ref, ga_ref):
    h = _rms(x_ref[...], g_ref[...]).astype(BF16)

    def mm(c0, n):
        return _dot(h, w_ref[:, c0:c0 + n])

    u = mm(_C_U, 512)
    for lb in range(SSM_NLB):
        u_ref[lb] = u[:, lb * LANES:(lb + 1) * LANES]

    def rope(z, cos, sin, d):
        n = z.shape[1]
        lane = lax.broadcasted_iota(I32, z.shape, 1)
        partner = jnp.where((lane & (d - 1)) < d // 2, pltpu.roll(z, n - d // 2, 1), pltpu.roll(z, d // 2, 1))
        return z * cos + partner * sin

    cq = cq_ref[...]
    sq = sq_ref[...]
    q_ref[...] = (rope(mm(_C_Q, 512), cq, sq, HEAD_DIM) * (HEAD_DIM ** -0.5 * LOG2E)).astype(BF16)
    k_ref[...] = rope(mm(_C_K, 512), cq, sq, HEAD_DIM).astype(BF16)
    v_ref[...] = mm(_C_V, 512).astype(BF16)
    ci = ci_ref[...]
    si = si_ref[...]
    qi_ref[...] = rope(mm(_C_QI, 256), ci, si, IDX_DIM).astype(BF16)
    ki_ref[...] = rope(mm(_C_KI, 256), ci, si, IDX_DIM).astype(BF16)
    wi_ref[...] = mm(_C_WI, 128) * ((IDX_HEADS * IDX_DIM) ** -0.5)
    bg = bg_ref[...]
    gs_ref[...] = jax.nn.sigmoid(mm(_C_GS, 1024) + bg[:, :1024]).astype(BF16)
    ga_ref[...] = jax.nn.sigmoid(mm(_C_GA, 1024) + bg[:, 1024:]).astype(BF16)


def _rope_tables(seq, d, reps):
    half = d // 2
    inv = ROPE_THETA ** (-jnp.arange(half, dtype=F32) * 2.0 / d)
    ang = jnp.arange(seq, dtype=F32)[:, None] * inv[None, :]
    cos, sin = jnp.cos(ang), jnp.sin(ang)
    c = jnp.concatenate([cos, cos], axis=-1)
    s = jnp.concatenate([-sin, sin], axis=-1)
    return jnp.tile(c, (1, reps)), jnp.tile(s, (1, reps))


def _inproj(x2, norm_mix, w_in, b_gates, seq, tm):
    t = x2.shape[0]
    w_u, w_q, w_k, w_v, w_qi, w_ki, w_wi, w_gs, w_ga = jnp.split(
        w_in, [512, 1024, 1536, 2048, 2304, 2336, 2344, 3368], axis=1)
    w_ki8 = jnp.tile(w_ki, (1, IDX_HEADS))
    w_wi_p = jnp.pad(w_wi, ((0, 0), (0, LANES - IDX_HEADS)))
    w_all = jnp.concatenate([w_u, w_q, w_k, w_v, w_qi, w_ki8, w_wi_p, w_gs, w_ga], axis=1).astype(BF16)
    assert w_all.shape[1] == _C_END
    cq, sq = _rope_tables(seq, HEAD_DIM, N_HEADS)
    ci, si = _rope_tables(seq, IDX_DIM, IDX_HEADS)
    nt = seq // tm
    row = lambda i: (i, 0)
    pos = lambda i: (i % nt, 0)
    const = lambda i: (0, 0)
    outs = pl.pallas_call(
        _inproj_kernel,
        grid=(t // tm,),
        in_specs=[
            pl.BlockSpec((tm, D_MODEL), row),
            pl.BlockSpec((1, D_MODEL), const),
            pl.BlockSpec((D_MODEL, _C_END), const),
            pl.BlockSpec((1, 2 * D_MODEL), const),
            pl.BlockSpec((tm, 512), pos), pl.BlockSpec((tm, 512), pos),
            pl.BlockSpec((tm, 256), pos), pl.BlockSpec((tm, 256), pos),
        ],
        out_specs=[
            pl.BlockSpec((SSM_NLB, tm, LANES), lambda i: (0, i, 0)),
            pl.BlockSpec((tm, 512), row), pl.BlockSpec((tm, 512), row), pl.BlockSpec((tm, 512), row),
            pl.BlockSpec((tm, 256), row), pl.BlockSpec((tm, 256), row), pl.BlockSpec((tm, LANES), row),
            pl.BlockSpec((tm, D_MODEL), row), pl.BlockSpec((tm, D_MODEL), row),
        ],
        out_shape=[
            jax.ShapeDtypeStruct((SSM_NLB, t, LANES), F32),
            jax.ShapeDtypeStruct((t, 512), BF16), jax.ShapeDtypeStruct((t, 512), BF16),
            jax.ShapeDtypeStruct((t, 512), BF16),
            jax.ShapeDtypeStruct((t, 256), BF16), jax.ShapeDtypeStruct((t, 256), BF16),
            jax.ShapeDtypeStruct((t, LANES), F32),
            jax.ShapeDtypeStruct((t, D_MODEL), BF16), jax.ShapeDtypeStruct((t, D_MODEL), BF16),
        ],
        compiler_params=pltpu.CompilerParams(dimension_semantics=("arbitrary",),
                                             vmem_limit_bytes=VMEM_LIMIT),
        name="inproj",
    )(x2, norm_mix.reshape(1, D_MODEL), w_all, b_gates.reshape(1, 2 * D_MODEL), cq, sq, ci, si)
    return outs


def _ssm_mats(lam_re, lam_im, log_dt, b_re, b_im, c_re, c_im, d_skip):
    g_, p_, h_, tc, gl, nlb = SSM_GROUPS, SSM_STATE, SSM_GROUP, SSM_TC, SSM_GL, SSM_NLB
    lam = lax.complex(lam_re.astype(F32), lam_im.astype(F32))
    dt = jnp.exp(log_dt.astype(F32))[:, None]
    lam_dt = lam * dt
    lam_bar = jnp.exp(lam_dt)
    b_bar = ((lam_bar - 1.0) / lam)[..., None] * lax.complex(b_re.astype(F32), b_im.astype(F32))
    c = lax.complex(c_re.astype(F32), c_im.astype(F32))
    steps = jnp.arange(tc + 1, dtype=F32)
    pw = jnp.exp(lam_dt[None] * steps[:, None, None])
    hp = lax.Precision.HIGHEST
    lane_g = jnp.arange(LANES) // h_
    st_g = (jnp.arange(2 * gl * p_) % (gl * p_)) // p_
    rep_o = (jnp.arange(h_)[:, None] == (jnp.arange(LANES) % h_)[None, :]).astype(F32)
    st_col = (jnp.arange(2 * gl * p_) // (gl * p_)) * p_ + jnp.arange(2 * gl * p_) % p_
    rep_s = (jnp.arange(2 * p_)[:, None] == st_col[None, :]).astype(F32)
    taps = jnp.einsum('gop,tgp,gpi->gtoi', c, pw[:tc], b_bar).real
    a3 = taps.reshape(nlb, gl, tc, h_, h_).transpose(0, 2, 1, 4, 3).reshape(nlb, tc, LANES, h_)
    d = jnp.einsum('btro,oc->btrc', a3, rep_o, precision=hp)
    d = d * (lane_g[:, None] == lane_g[None, :]).astype(F32)
    dcat = d.transpose(0, 2, 1, 3).reshape(nlb, LANES, tc * LANES)
    m_intra = jnp.stack(
        [jnp.pad(dcat[:, :, :(tc - j) * LANES], ((0, 0), (0, 0), (j * LANES, 0))) for j in range(tc)],
        axis=1).reshape(nlb, tc * LANES, tc * LANES)
    sc = pw[:tc][::-1][:, :, :, None] * b_bar[None]
    sc = jnp.stack([sc.real, sc.imag], axis=0).reshape(2, tc, nlb, gl, p_, h_)
    a_s = sc.transpose(2, 1, 3, 5, 0, 4).reshape(nlb, tc, LANES, 2 * p_)
    m_state = jnp.einsum('bjrq,qc->bjrc', a_s, rep_s, precision=hp)
    m_state = (m_state * (lane_g[:, None] == st_g[None, :]).astype(F32)).reshape(nlb, tc * LANES, 2 * gl * p_)
    oc = c[None] * pw[1:tc + 1][:, :, None, :]
    oc = jnp.stack([oc.real, -oc.imag], axis=0).reshape(2, tc, nlb, gl, h_, p_)
    a_o = oc.transpose(2, 1, 0, 3, 5, 4).reshape(nlb, tc, 2 * gl * p_, h_)
    m_out = jnp.einsum('blro,oc->blrc', a_o, rep_o, precision=hp)
    m_out = m_out * (st_g[:, None] == lane_g[None, :]).astype(F32)
    m_out = m_out.transpose(0, 2, 1, 3).reshape(nlb, 2 * gl * p_, tc * LANES)
    a = pw[tc].reshape(nlb, 1, gl * p_)
    d_l = jnp.tile(d_skip.astype(F32).reshape(nlb, 1, LANES), (1, 1, tc))
    return (m_intra.astype(BF16), m_state.astype(BF16), m_out.astype(BF16),
            a.real.astype(F32), a.imag.astype(F32), d_l)


def _ssm_kernel(u_ref, mi_ref, ms_ref, mo_ref, are_ref, aim_ref, d_ref, y_ref, con_ref, sp_ref):
    nch = u_ref.shape[0] // SSM_TC
    ns = are_ref.shape[1]
    uf = jnp.concatenate([u_ref[pl.ds(j, nch, stride=SSM_TC), :] for j in range(SSM_TC)], axis=1)
    u = uf.astype(BF16)
    con_ref[...] = _dot(u, ms_ref[...])
    a_re = are_ref[...]
    a_im = aim_ref[...]

    def step(c, carry):
        s_re, s_im = carry
        sp_ref[pl.ds(c, 1), :ns] = s_re
        sp_ref[pl.ds(c, 1), ns:] = s_im
        c_re = con_ref[pl.ds(c, 1), :ns]
        c_im = con_ref[pl.ds(c, 1), ns:]
        return (a_re * s_re - a_im * s_im + c_re, a_re * s_im + a_im * s_re + c_im)

    zero = jnp.zeros((1, ns), F32)
    lax.fori_loop(0, nch, step, (zero, zero))
    y = _dot(u, mi_ref[...]) + _dot(sp_ref[...].astype(BF16), mo_ref[...]) + d_ref[...] * uf
    for l in range(SSM_TC):
        y_ref[pl.ds(l, nch, stride=SSM_TC), :] = y[:, l * LANES:(l + 1) * LANES]


def _ssm(u4, mats, bsz, seq):
    m_intra, m_state, m_out, a_re, a_im, d_l = mats
    t = u4.shape[1]
    nch = seq // SSM_TC
    wc = SSM_TC * LANES
    ns = SSM_GL * SSM_STATE
    wmap = lambda lb, b: (lb, 0, 0)
    return pl.pallas_call(
        _ssm_kernel,
        grid=(SSM_NLB, bsz),
        in_specs=[
            pl.BlockSpec((None, seq, LANES), lambda lb, b: (lb, b, 0)),
            pl.BlockSpec((None, wc, wc), wmap),
            pl.BlockSpec((None, wc, 2 * ns), wmap),
            pl.BlockSpec((None, 2 * ns, wc), wmap),
            pl.BlockSpec((None, 1, ns), wmap), pl.BlockSpec((None, 1, ns), wmap),
            pl.BlockSpec((None, 1, wc), wmap),
        ],
        out_specs=pl.BlockSpec((None, seq, LANES), lambda lb, b: (lb, b, 0)),
        out_shape=jax.ShapeDtypeStruct((SSM_NLB, t, LANES), F32),
        scratch_shapes=[pltpu.VMEM((nch, 2 * ns), F32), pltpu.VMEM((nch, 2 * ns), F32)],
        compiler_params=pltpu.CompilerParams(dimension_semantics=("arbitrary", "arbitrary"),
                                             vmem_limit_bytes=VMEM_LIMIT),
        name="ssm",
    )(u4, m_intra, m_state, m_out, a_re, a_im, d_l)


def _attn_kernel(q_ref, k_ref, v_ref, qi_ref, ki_ref, wi_ref, tri_ref, o_ref, keys_ref, kt_ref, k16_ref,
                 qh_ref, qm_ref, acc_ref, m_ref, *, tq, tk, seq, topk):
    i = pl.program_id(1)
    s0 = i * tq
    nk = (s0 + tq + tk - 1) // tk
    row = s0 + lax.broadcasted_iota(I32, (tq, 1), 0)
    lim = ((row >> CHUNK_SHIFT) + 1) << CHUNK_SHIFT
    lim_q = (((s0 + lax.broadcasted_iota(I32, (1, tq), 1)) >> CHUNK_SHIFT) + 1) << CHUNK_SHIFT
    kvec = jnp.minimum(lim_q, topk).astype(F32)
    col0 = lax.broadcasted_iota(I32, (tq, tk), 1)

    qi = qi_ref[...]
    lane_i = lax.broadcasted_iota(I32, qi.shape, 1)
    for h in range(IDX_HEADS):
        qh_ref[h] = jnp.where((lane_i >> 5) == h, qi, jnp.zeros_like(qi))
    wi = wi_ref[...]
    wcol = [wi[:, h:h + 1] for h in range(IDX_HEADS)]

    def score_tile(j, carry):
        kt = ki_ref[pl.ds(pl.multiple_of(j * tk, tk), tk), :]
        acc = jnp.zeros((tq, tk), F32)
        for h in range(IDX_HEADS):
            acc = acc + wcol[h] * jnp.maximum(_dot_nt(qh_ref[h], kt), 0.0)
        sc = jnp.where(col0 + j * tk < lim, acc, -jnp.inf)
        bits = pltpu.bitcast(sc, I32)
        key = bits ^ ((bits >> 31) & 0x7FFFFFFF)
        keys_ref[j] = key
        key_t = key.T
        kt_ref[j] = key_t
        k16_ref[j] = (key_t >> 16).astype(jnp.int16)
        return carry

    lax.fori_loop(0, nk, score_tile, 0)

    rg = 16
    one16 = jnp.ones((rg, tq), jnp.int16)
    zero16 = jnp.zeros((rg, tq), jnp.int16)

    def count16(cand):
        cand16 = jnp.broadcast_to(cand, (rg, tq)).astype(jnp.int16)

        def body(j, part):
            kk = k16_ref[j]
            for r in range(tk // rg):
                part = part + jnp.where(kk[r * rg:(r + 1) * rg, :] >= cand16, one16, zero16)
            return part
        part = lax.fori_loop(0, nk, body, jnp.zeros((rg, tq), jnp.int16))
        return jnp.sum(part.astype(F32), axis=0, keepdims=True)

    def bit_step(it, carry):
        u_ans, g_lo, g_hi = carry
        cand_u = u_ans | lax.shift_left(jnp.int32(1), 15 - it)
        cnt = count16(cand_u - 32768)
        ok = cnt >= kvec
        return jnp.where(ok, cand_u, u_ans), jnp.where(ok, cnt, g_lo), jnp.where(ok, g_hi, cnt)

    u_ans, g_lo, g_hi = lax.fori_loop(
        0, 16, bit_step, (jnp.zeros((1, tq), I32), lim_q.astype(F32), jnp.zeros((1, tq), F32)))
    b16 = u_ans - 32768

    def rebase(j, carry):
        kk = kt_ref[j]
        hi16 = kk >> 16
        low = (kk & 0xFFFF) - 32768
        rel = jnp.where(hi16 > b16, 32767, jnp.where(hi16 < b16, -32768, low))
        k16_ref[j] = rel.astype(jnp.int16)
        return carry

    lax.fori_loop(0, nk, rebase, 0)

    def unfinished(lo, hi, g_lo):
        return jnp.where(g_lo != kvec, jnp.where(hi - lo > 1, 1.0, 0.0), 0.0)

    def refine_cond(carry):
        return (carry[0] < 18) & (carry[1] > 0.0)

    def refine(carry):
        it, _, lo, hi, g_lo, g_hi = carry
        todo = unfinished(lo, hi, g_lo)
        mid = lo + ((hi - lo) >> 1)
        cand = jnp.where(b16 == 0, jnp.where(hi - lo == 65536, 1, mid), mid)
        cnt = count16(cand - 32768)
        up = jnp.where(cnt >= kvec, todo, 0.0) > 0.0
        dn = jnp.where(cnt < kvec, todo, 0.0) > 0.0
        lo = jnp.where(up, cand, lo)
        g_lo = jnp.where(up, cnt, g_lo)
        hi = jnp.where(dn, cand, hi)
        g_hi = jnp.where(dn, cnt, g_hi)
        return it + 1, jnp.max(unfinished(lo, hi, g_lo)), lo, hi, g_lo, g_hi

    off_lo = jnp.zeros((1, tq), I32)
    off_hi = jnp.full((1, tq), 65536, I32)
    _, _, off_lo, _, g_lo, g_hi = lax.while_loop(
        refine_cond, refine,
        (jnp.int32(0), jnp.max(unfinished(off_lo, off_hi, g_lo)), off_lo, off_hi, g_lo, g_hi))
    has_tie = jnp.max(g_lo - kvec) > 0.0

    def to_rows(x):
        return jnp.transpose(jnp.broadcast_to(x, (LANES, tq)))

    vth = to_rows((b16 << 16) + off_lo)

    @pl.when(jnp.logical_not(has_tie))
    def _():
        def to_bias(j, carry):
            kk = keys_ref[j]
            for c in range(tk // LANES):
                sl = slice(c * LANES, (c + 1) * LANES)
                keys_ref[j, :, sl] = pltpu.bitcast(jnp.where(kk[:, sl] >= vth, 0.0, NEG_BIG).astype(F32), I32)
            return carry
        lax.fori_loop(0, nk, to_bias, 0)

    @pl.when(has_tie)
    def _():
        need = to_rows(kvec - g_hi)

        def to_bias(j, seen):
            kk = keys_ref[j]
            eq = jnp.concatenate(
                [jnp.where(kk[:, c * LANES:(c + 1) * LANES] == vth, 1.0, 0.0) for c in range(tk // LANES)],
                axis=1)
            rank = _dot(eq.astype(BF16), tri_ref[...])
            for c in range(tk // LANES):
                sl = slice(c * LANES, (c + 1) * LANES)
                take = jnp.where(seen + rank[:, sl] <= need, eq[:, sl], 0.0)
                sel = jnp.where(kk[:, sl] > vth, 1.0, take)
                keys_ref[j, :, sl] = pltpu.bitcast(jnp.where(sel > 0.0, 0.0, NEG_BIG).astype(F32), I32)
            return seen + jnp.broadcast_to(rank[:, tk - 1:tk], (tq, LANES))
        lax.fori_loop(0, nk, to_bias, jnp.zeros((tq, LANES), F32))

    lane_q = lax.broadcasted_iota(I32, (tq, LANES), 1)
    lane_k = lax.broadcasted_iota(I32, (tk, LANES), 1)
    for h in range(N_HEADS):
        qp = q_ref[:, (h // 2) * LANES:(h // 2 + 1) * LANES]
        keep = (lane_q >= HEAD_DIM) if h % 2 else (lane_q < HEAD_DIM)
        qm_ref[h] = jnp.where(keep, qp, jnp.zeros_like(qp))
    acc_ref[...] = jnp.zeros_like(acc_ref)
    m_ref[...] = jnp.full(m_ref.shape, NEG_BIG, F32)
    ones_v = jnp.ones((tk, LANES), BF16)

    def attn_tile(j, carry):
        rows = pl.ds(pl.multiple_of(j * tk, tk), tk)
        bias = pltpu.bitcast(keys_ref[j], F32)
        for h in range(N_HEADS):
            cs = slice((h // 2) * LANES, (h // 2 + 1) * LANES)
            own = (lane_k >= HEAD_DIM) if h % 2 else (lane_k < HEAD_DIM)
            s = _dot_nt(qm_ref[h], k_ref[rows, cs]) + bias
            m_old = m_ref[h]
            m_new = jnp.maximum(m_old, jnp.max(s, axis=1, keepdims=True))
            p = jnp.concatenate(
                [jnp.exp2(s[:, c * LANES:(c + 1) * LANES] - m_new) for c in range(tk // LANES)],
                axis=1).astype(BF16)
            vh = jnp.where(own, v_ref[rows, cs], ones_v)
            acc_ref[h] = jnp.exp2(m_old - m_new) * acc_ref[h] + _dot(p, vh)
            m_ref[h] = m_new
        return carry

    lax.fori_loop(0, nk, attn_tile, 0)

    for hp in range(N_HEADS // 2):
        a0 = acc_ref[2 * hp]
        a1 = acc_ref[2 * hp + 1]
        out = jnp.where(lane_q < HEAD_DIM, a0 / pltpu.roll(a0, HEAD_DIM, 1), a1 / pltpu.roll(a1, HEAD_DIM, 1))
        o_ref[:, hp * LANES:(hp + 1) * LANES] = out.astype(o_ref.dtype)


def _attn(q, k, v, qi, ki8, wi, bsz, seq, tq, tk):
    t = q.shape[0]
    nq = seq // tq
    qrow = lambda b, i: (b * nq + i, 0)
    kv = lambda b, i: (b, 0)
    kern = functools.partial(_attn_kernel, tq=tq, tk=tk, seq=seq, topk=min(TOPK_MAX, seq // 4))
    tri = (jnp.arange(tk)[:, None] <= jnp.arange(tk)[None, :]).astype(BF16)
    return pl.pallas_call(
        kern,
        grid=(bsz, nq),
        in_specs=[
            pl.BlockSpec((tq, ATTN_WIDTH), qrow),
            pl.BlockSpec((seq, ATTN_WIDTH), kv), pl.BlockSpec((seq, ATTN_WIDTH), kv),
            pl.BlockSpec((tq, 256), qrow), pl.BlockSpec((seq, 256), kv),
            pl.BlockSpec((tq, LANES), qrow),
            pl.BlockSpec((tk, tk), lambda b, i: (0, 0)),
        ],
        out_specs=pl.BlockSpec((tq, ATTN_WIDTH), qrow),
        out_shape=jax.ShapeDtypeStruct((t, ATTN_WIDTH), BF16),
        scratch_shapes=[
            pltpu.VMEM((seq // tk, tq, tk), I32),
            pltpu.VMEM((seq // tk, tk, tq), I32),
            pltpu.VMEM((seq // tk, tk, tq), jnp.int16),
            pltpu.VMEM((IDX_HEADS, tq, 256), BF16),
            pltpu.VMEM((N_HEADS, tq, LANES), BF16),
            pltpu.VMEM((N_HEADS, tq, LANES), F32),
            pltpu.VMEM((N_HEADS, tq, LANES), F32),
        ],
        compiler_params=pltpu.CompilerParams(dimension_semantics=("arbitrary", "arbitrary"),
                                             vmem_limit_bytes=VMEM_LIMIT),
        name="attn",
    )(q, k, v, qi, ki8, wi, tri)


def _mix_kernel(y4_ref, ya_ref, gs_ref, ga_ref, x_ref, wglu_ref, bglu_ref, wsb_ref, wab_ref, wo_ref,
                nf_ref, wr_ref, br_ref, r1_ref, h2_ref, route_ref):
    ys = jnp.concatenate([y4_ref[lb] for lb in range(SSM_NLB)], axis=1)
    ys = 0.5 * ys * (1.0 + jnp.tanh(math.sqrt(2.0 / math.pi) * (ys + 0.044715 * (ys * ys * ys))))
    ys = ys * jax.nn.sigmoid(_dot(ys.astype(BF16), wglu_ref[...]) + bglu_ref[...])
    br_s = _dot(ys.astype(BF16), wsb_ref[...])
    br_a = _dot(ya_ref[...], wab_ref[...])
    mix = gs_ref[...].astype(F32) * br_s + ga_ref[...].astype(F32) * br_a
    r1 = x_ref[...] + _dot(mix.astype(BF16), wo_ref[...])
    r1_ref[...] = r1
    h2 = _rms(r1, nf_ref[...])
    h2_ref[...] = h2
    hi = h2.astype(BF16)
    lo = (h2 - hi.astype(F32)).astype(BF16)
    hw = _dot(hi, wr_ref[...])
    logits = hw[:, :LANES] + (hw[:, LANES:] + _dot(lo, wr_ref[:, :LANES])) + br_ref[...]
    tm = logits.shape[0]
    lane = lax.broadcasted_iota(I32, (tm, LANES), 1).astype(F32)
    vals = logits
    tops, idxs = [], []
    for _ in range(TOP_K):
        m = jnp.max(vals, axis=1, keepdims=True)
        idx = jnp.min(jnp.where(vals == m, lane, float(LANES)), axis=1, keepdims=True)
        tops.append(m)
        idxs.append(idx)
        vals = jnp.where(lane == idx, -jnp.inf, vals)
    es = [jnp.exp(tv - tops[0]) for tv in tops]
    den = es[0] + es[1] + es[2] + es[3]
    route = jnp.zeros((tm, LANES), F32)
    for kk in range(TOP_K):
        w = es[kk] / den
        w_hi = w.astype(BF16).astype(F32)
        w_mid = (w - w_hi).astype(BF16).astype(F32)
        w_lo = (w - w_hi) - w_mid
        route = jnp.where(lane == float(kk), idxs[kk], route)
        route = jnp.where(lane == float(TOP_K + kk), w_hi, route)
        route = jnp.where(lane == float(2 * TOP_K + kk), w_mid, route)
        route = jnp.where(lane == float(3 * TOP_K + kk), w_lo, route)
    route_ref[...] = route.astype(BF16)


def _mix(y4, y_attn, gs, ga, x2, w_glu, b_glu, w_ssm_br, w_attn_br, w_o, norm_ffn, w_router, b_router, tm):
    t = x2.shape[0]
    wr = jnp.pad(w_router.astype(F32), ((0, 0), (0, LANES - N_EXPERTS)))
    wr_hi = wr.astype(BF16)
    wr_split = jnp.concatenate([wr_hi, (wr - wr_hi.astype(F32)).astype(BF16)], axis=1)
    br = jnp.concatenate([b_router.astype(F32), jnp.full((LANES - N_EXPERTS,), NEG_BIG, F32)]).reshape(1, LANES)
    row = lambda i: (i, 0)
    const = lambda i: (0, 0)
    return pl.pallas_call(
        _mix_kernel,
        grid=(t // tm,),
        in_specs=[
            pl.BlockSpec((SSM_NLB, tm, LANES), lambda i: (0, i, 0)),
            pl.BlockSpec((tm, ATTN_WIDTH), row),
            pl.BlockSpec((tm, D_MODEL), row), pl.BlockSpec((tm, D_MODEL), row), pl.BlockSpec((tm, D_MODEL), row),
            pl.BlockSpec((SSM_WIDTH, SSM_WIDTH), const), pl.BlockSpec((1, SSM_WIDTH), const),
            pl.BlockSpec((SSM_WIDTH, D_MODEL), const), pl.BlockSpec((ATTN_WIDTH, D_MODEL), const),
            pl.BlockSpec((D_MODEL, D_MODEL), const), pl.BlockSpec((1, D_MODEL), const),
            pl.BlockSpec((D_MODEL, 2 * LANES), const),
            pl.BlockSpec((1, LANES), const),
        ],
        out_specs=[pl.BlockSpec((tm, D_MODEL), row), pl.BlockSpec((tm, D_MODEL), row),
                   pl.BlockSpec((tm, LANES), row)],
        out_shape=[jax.ShapeDtypeStruct((t, D_MODEL), F32), jax.ShapeDtypeStruct((t, D_MODEL), F32),
                   jax.ShapeDtypeStruct((t, LANES), BF16)],
        compiler_params=pltpu.CompilerParams(dimension_semantics=("arbitrary",),
                                             vmem_limit_bytes=VMEM_LIMIT),
        name="mix",
    )(y4, y_attn, gs, ga, x2, w_glu.astype(BF16), b_glu.reshape(1, SSM_WIDTH).astype(F32),
      w_ssm_br.astype(BF16), w_attn_br.astype(BF16), w_o.astype(BF16), norm_ffn.reshape(1, D_MODEL),
      wr_split, br)


def _prefix_kernel(route_ref, tri_ref, utri_ref, lpos_ref, lpos_t_ref, cnt_ref):
    route = route_ref[...].astype(F32)
    tm = route.shape[0]
    lane = lax.broadcasted_iota(I32, (tm, LANES), 1).astype(F32)
    mask = jnp.zeros((tm, LANES), F32)
    for kk in range(TOP_K):
        mask = jnp.where(lane == route[:, kk:kk + 1], 1.0, mask)
    incl = _dot(tri_ref[...], mask.astype(BF16))
    cnt = incl[tm - 1:tm, :]
    cnt_u = jnp.floor((cnt + (MOE_SEG - 1)) * (1.0 / MOE_SEG))
    loc = _dot(jnp.broadcast_to(cnt_u, (8, LANES)).astype(BF16), utri_ref[...])[:1, :] * float(MOE_SEG)
    where_to = loc + incl - mask
    lpos = jnp.full((tm, LANES), -1.0, F32)
    for kk in range(TOP_K):
        pos = jnp.sum(jnp.where(lane == route[:, kk:kk + 1], where_to, 0.0), axis=1, keepdims=True)
        lpos = jnp.where(lane == float(kk), pos, lpos)
    lpos_ref[...] = lpos.astype(I32)
    lpos_t_ref[...] = jnp.transpose(lpos)[:8, :].astype(I32)
    cnt_ref[...] = jnp.broadcast_to(cnt, (8, LANES)).astype(I32)


def _prefix(route):
    t = route.shape[0]
    tm = MOE_TT
    nt = t // tm
    tri = (jnp.arange(tm)[:, None] >= jnp.arange(tm)[None, :]).astype(BF16)
    utri = (jnp.arange(LANES)[:, None] < jnp.arange(LANES)[None, :]).astype(BF16)
    return pl.pallas_call(
        _prefix_kernel,
        grid=(nt,),
        in_specs=[pl.BlockSpec((tm, LANES), lambda i: (i, 0)), pl.BlockSpec((tm, tm), lambda i: (0, 0)),
                  pl.BlockSpec((LANES, LANES), lambda i: (0, 0))],
        out_specs=[pl.BlockSpec((tm, LANES), lambda i: (i, 0)), pl.BlockSpec((None, 8, tm), lambda i: (i, 0, 0)),
                   pl.BlockSpec((None, 8, LANES), lambda i: (i, 0, 0))],
        out_shape=[jax.ShapeDtypeStruct((t, LANES), I32), jax.ShapeDtypeStruct((nt, 8, tm), I32),
                   jax.ShapeDtypeStruct((nt, 8, LANES), I32)],
        compiler_params=pltpu.CompilerParams(dimension_semantics=("arbitrary",)),
        name="prefix",
    )(route, tri, utri)


def _segment_copies(seg_ref, e, make_copy, fn):
    c = seg_ref[0, e]
    s = seg_ref[0, N_EXPERTS + e]
    d = seg_ref[0, 2 * N_EXPERTS + e]
    for size in _SEG_SIZES:
        units = size // MOE_SEG
        hit = (c & units) != 0

        @pl.when(hit)
        def _():
            fn(make_copy(pl.multiple_of(s * MOE_SEG, MOE_SEG), pl.multiple_of(d * MOE_SEG, MOE_SEG), size))
        s = s + jnp.where(hit, units, 0)
        d = d + jnp.where(hit, units, 0)


def _dispatch_kernel(zt_ref, seg_ref, lpos_ref, h2_ref, wtok_ref, xs_ref, buf_ref, zbuf_ref, sem, zsem):
    tmx = zbuf_ref.shape[0]

    @pl.when(pl.program_id(0) == 0)
    def _():
        zbuf_ref[...] = jnp.zeros_like(zbuf_ref)

        def zero_copy(z):
            return pltpu.make_async_copy(
                zbuf_ref, xs_ref.at[pl.ds(pl.multiple_of(zt_ref[z] * tmx, tmx), tmx)], zsem)

        for z in range(zt_ref.shape[0]):
            @pl.when(zt_ref[z] >= 0)
            def _():
                zero_copy(z).start()
        for z in range(zt_ref.shape[0]):
            @pl.when(zt_ref[z] >= 0)
            def _():
                zero_copy(z).wait()

    rowi = lax.broadcasted_iota(I32, (MOE_NL, MOE_TT), 0)
    sel = jnp.zeros((MOE_NL, MOE_TT), F32)
    for kk in range(TOP_K):
        sel = jnp.where(rowi == lpos_ref[kk:kk + 1, :], 1.0, sel)
    src = jnp.concatenate([h2_ref[...].astype(BF16), wtok_ref[...]], axis=1)
    buf_ref[...] = _dot(sel.astype(BF16), src)

    def make_copy(s, d, size):
        return pltpu.make_async_copy(buf_ref.at[pl.ds(s, size)], xs_ref.at[pl.ds(d, size)], sem)

    def start(e, carry):
        _segment_copies(seg_ref, e, make_copy, lambda cp: cp.start())
        return carry

    def wait(e, carry):
        _segment_copies(seg_ref, e, make_copy, lambda cp: cp.wait())
        return carry

    lax.fori_loop(0, N_EXPERTS, start, 0)
    lax.fori_loop(0, N_EXPERTS, wait, 0)


def _dispatch(zero_tiles, seg, lpos_t, h2, wtok, n_rows, tmx):
    t = h2.shape[0]
    grid_spec = pltpu.PrefetchScalarGridSpec(
        num_scalar_prefetch=1,
        grid=(t // MOE_TT,),
        in_specs=[
            pl.BlockSpec((None, 1, 3 * N_EXPERTS), lambda i, zt: (i, 0, 0), memory_space=pltpu.SMEM),
            pl.BlockSpec((None, 8, MOE_TT), lambda i, zt: (i, 0, 0)),
            pl.BlockSpec((MOE_TT, D_MODEL), lambda i, zt: (i, 0)),
            pl.BlockSpec((MOE_TT, LANES), lambda i, zt: (i, 0)),
        ],
        out_specs=pl.BlockSpec(memory_space=pl.ANY),
        scratch_shapes=[pltpu.VMEM((MOE_NL, MOE_W), F32), pltpu.VMEM((tmx, MOE_W), F32),
                        pltpu.SemaphoreType.DMA(()), pltpu.SemaphoreType.DMA(())],
    )
    return pl.pallas_call(
        _dispatch_kernel,
        grid_spec=grid_spec,
        out_shape=jax.ShapeDtypeStruct((n_rows, MOE_W), F32),
        compiler_params=pltpu.CompilerParams(dimension_semantics=("arbitrary",), has_side_effects=True,
                                             vmem_limit_bytes=VMEM_LIMIT),
        name="dispatch",
    )(zero_tiles, seg, lpos_t, h2, wtok)


def _expert_kernel(te_ref, tv_ref, x_ref, wg_ref, bg_ref, wu_ref, bu_ref, wd_ref, bd_ref, y_ref,
                   wgb_ref, wub_ref, wdb_ref):
    i = pl.program_id(0)
    e = te_ref[i]
    prev = te_ref[jnp.maximum(i - 1, 0)]

    @pl.when((i == 0) | (e != prev))
    def _():
        wgb_ref[...] = wg_ref[...].astype(BF16)
        wub_ref[...] = wu_ref[...].astype(BF16)
        wdb_ref[...] = wd_ref[...].astype(BF16)

    @pl.when(tv_ref[i] != 0)
    def _():
        x = x_ref[:, :D_MODEL].astype(BF16)
        rt = x_ref[:, D_MODEL:]
        ef = e.astype(F32)
        w_row = jnp.zeros((rt.shape[0], 1), F32)
        for kk in range(TOP_K):
            wk = (rt[:, TOP_K + kk:TOP_K + kk + 1] + rt[:, 2 * TOP_K + kk:2 * TOP_K + kk + 1]
                  + rt[:, 3 * TOP_K + kk:3 * TOP_K + kk + 1])
            w_row = w_row + jnp.where(rt[:, kk:kk + 1] == ef, wk, 0.0)
        g = jnp.minimum(_dot(x, wgb_ref[...]) + bg_ref[...], SWIGLU_LIMIT)
        u = jnp.clip(_dot(x, wub_ref[...]) + bu_ref[...], -SWIGLU_LIMIT, SWIGLU_LIMIT)
        a = (u + 1.0) * (g * jax.nn.sigmoid(SWIGLU_ALPHA * g))
        y_ref[...] = (_dot(a.astype(BF16), wdb_ref[...]) + bd_ref[...]) * w_row

    @pl.when(tv_ref[i] == 0)
    def _():
        y_ref[...] = jnp.zeros_like(y_ref)


def _experts(tile_expert, tile_valid, xs, w_gate, b_gate, w_up, b_up, w_down, b_down, tmx):
    n_rows = xs.shape[0]
    wmap = lambda i, te, tv: (te[i], 0, 0)
    row = lambda i, te, tv: (i, 0)
    d_ff = w_gate.shape[2]
    grid_spec = pltpu.PrefetchScalarGridSpec(
        num_scalar_prefetch=2,
        grid=(n_rows // tmx,),
        in_specs=[
            pl.BlockSpec((tmx, MOE_W), row),
            pl.BlockSpec((None, D_MODEL, d_ff), wmap), pl.BlockSpec((None, 1, d_ff), wmap),
            pl.BlockSpec((None, D_MODEL, d_ff), wmap), pl.BlockSpec((None, 1, d_ff), wmap),
            pl.BlockSpec((None, d_ff, D_MODEL), wmap), pl.BlockSpec((None, 1, D_MODEL), wmap),
        ],
        out_specs=pl.BlockSpec((tmx, D_MODEL), row),
        scratch_shapes=[pltpu.VMEM((D_MODEL, d_ff), BF16), pltpu.VMEM((D_MODEL, d_ff), BF16),
                        pltpu.VMEM((d_ff, D_MODEL), BF16)],
    )
    return pl.pallas_call(
        _expert_kernel,
        grid_spec=grid_spec,
        out_shape=jax.ShapeDtypeStruct((n_rows, D_MODEL), F32),
        compiler_params=pltpu.CompilerParams(dimension_semantics=("arbitrary",),
                                             vmem_limit_bytes=VMEM_LIMIT),
        name="experts",
    )(tile_expert, tile_valid, xs, w_gate, b_gate.reshape(N_EXPERTS, 1, d_ff), w_up,
      b_up.reshape(N_EXPERTS, 1, d_ff), w_down, b_down.reshape(N_EXPERTS, 1, D_MODEL))


def _combine_kernel(seg_ref, lpos_ref, r1_ref, p_ref, ys_ref, wpg_ref, wpp_ref, nfin_ref, o_ref,
                    buf_ref, sem):
    @pl.when(pl.program_id(0) == 0)
    def _():
        buf_ref[...] = jnp.zeros_like(buf_ref)

    def make_copy(s, d, size):
        return pltpu.make_async_copy(ys_ref.at[pl.ds(d, size)], buf_ref.at[pl.ds(s, size)], sem)

    def start(e, carry):
        _segment_copies(seg_ref, e, make_copy, lambda cp: cp.start())
        return carry

    def wait(e, carry):
        _segment_copies(seg_ref, e, make_copy, lambda cp: cp.wait())
        return carry

    lax.fori_loop(0, N_EXPERTS, start, 0)
    coli = lax.broadcasted_iota(I32, (MOE_TT, MOE_NL), 1)
    lpos = lpos_ref[...]
    pick = jnp.zeros((MOE_TT, MOE_NL), F32)
    for kk in range(TOP_K):
        pick = jnp.where(coli == lpos[:, kk:kk + 1], 1.0, pick)
    lax.fori_loop(0, N_EXPERTS, wait, 0)
    moe = _dot(pick.astype(BF16), buf_ref[...].astype(BF16))
    r2 = r1_ref[...] + moe
    gate = jax.nn.sigmoid(_dot(r2.astype(BF16), wpg_ref[...]))
    r3 = r2 + gate * _dot(p_ref[...].astype(BF16), wpp_ref[...])
    o_ref[...] = _rms(r3, nfin_ref[...])


def _combine(seg, lpos, r1, p2, ys, w_ple_gate, w_ple_proj, norm_final):
    t = r1.shape[0]
    row = lambda i: (i, 0)
    const = lambda i: (0, 0)
    return pl.pallas_call(
        _combine_kernel,
        grid=(t // MOE_TT,),
        in_specs=[
            pl.BlockSpec((None, 1, 3 * N_EXPERTS), lambda i: (i, 0, 0), memory_space=pltpu.SMEM),
            pl.BlockSpec((MOE_TT, LANES), row),
            pl.BlockSpec((MOE_TT, D_MODEL), row),
            pl.BlockSpec((MOE_TT, PLE_DIM), row),
            pl.BlockSpec(memory_space=pl.ANY),
            pl.BlockSpec((D_MODEL, D_MODEL), const), pl.BlockSpec((PLE_DIM, D_MODEL), const),
            pl.BlockSpec((1, D_MODEL), const),
        ],
        out_specs=pl.BlockSpec((MOE_TT, D_MODEL), row),
        out_shape=jax.ShapeDtypeStruct((t, D_MODEL), F32),
        scratch_shapes=[pltpu.VMEM((MOE_NL, D_MODEL), F32), pltpu.SemaphoreType.DMA(())],
        compiler_params=pltpu.CompilerParams(dimension_semantics=("arbitrary",),
                                             vmem_limit_bytes=VMEM_LIMIT),
        name="combine",
    )(seg, lpos, r1, p2, ys, w_ple_gate.astype(BF16), w_ple_proj.astype(BF16),
      norm_final.reshape(1, D_MODEL))


def _moe(h2, route, r1, p2, w_gate, b_gate, w_up, b_up, w_down, b_down, w_ple_gate, w_ple_proj,
         norm_final, tmx):
    t = h2.shape[0]
    nt = t // MOE_TT
    lpos_l, lpos_t, cnt8 = _prefix(route)
    tile_cnt = cnt8[:, 0, :N_EXPERTS]
    cnt_u = (tile_cnt + MOE_SEG - 1) // MOE_SEG
    loc_u = jnp.cumsum(cnt_u, axis=1) - cnt_u
    reg_u = jnp.sum(cnt_u, axis=0)
    upt = tmx // MOE_SEG
    pad_u = ((reg_u + upt - 1) // upt) * upt
    end_u = jnp.cumsum(pad_u)
    glob_u = (end_u - pad_u)[None, :] + jnp.cumsum(cnt_u, axis=0) - cnt_u
    seg = jnp.concatenate([cnt_u, loc_u, glob_u], axis=1).astype(I32).reshape(nt, 1, 3 * N_EXPERTS)
    n_tiles = (t * TOP_K + nt * N_EXPERTS * (MOE_SEG - 1)) // tmx + N_EXPERTS
    ends = end_u * MOE_SEG
    tile_start = jnp.arange(n_tiles, dtype=I32) * tmx
    tile_valid = (tile_start < ends[-1]).astype(I32)
    tile_expert = jnp.minimum(jnp.sum((tile_start[:, None] >= ends[None, :]).astype(I32), axis=1), N_EXPERTS - 1)
    last_e = jnp.max(jnp.where(tile_valid != 0, tile_expert, 0))
    tile_expert = jnp.where(tile_valid != 0, tile_expert, last_e)
    n_spare = n_tiles - (t * TOP_K) // tmx
    used = ends[-1] // tmx
    last_tile = jnp.where(pad_u > 0, ends // tmx - 1, -1)
    spare = used + jnp.arange(n_spare, dtype=I32)
    zero_tiles = jnp.concatenate([last_tile, jnp.where(spare < n_tiles, spare, -1)]).astype(I32)
    xs = _dispatch(zero_tiles, seg, lpos_t, h2, route, n_tiles * tmx, tmx)
    ys = _experts(tile_expert, tile_valid, xs, w_gate, b_gate, w_up, b_up, w_down, b_down, tmx)
    return _combine(seg, lpos_l, r1, p2, ys, w_ple_gate, w_ple_proj, norm_final)


def _tile_plan(seq):
    return {
        "inproj_rows": 256,
        "attn_q": 256,
        "attn_k": min(1024, seq),
        "mix_rows": min(512, seq),
        "expert_rows": 512,
    }


def kernel(x, p, w_in, b_gates, lam_re, lam_im, log_dt, b_re, b_im, c_re, c_im, d_skip, w_glu, b_glu,
           w_ssm_br, w_attn_br, w_o, norm_mix, norm_ffn, w_router, b_router, w_gate, b_gate, w_up, b_up,
           w_down, b_down, w_ple_gate, w_ple_proj, norm_final):
    bsz, seq, _ = x.shape
    t = bsz * seq
    tiles = _tile_plan(seq)
    x2 = x.reshape(t, D_MODEL)
    u4, q, k, v, qi, ki8, wi, gs, ga = _inproj(x2, norm_mix[0], w_in[0], b_gates[0], seq, tm=tiles["inproj_rows"])
    mats = _ssm_mats(lam_re[0], lam_im[0], log_dt[0], b_re[0], b_im[0], c_re[0], c_im[0], d_skip[0])
    y4 = _ssm(u4, mats, bsz, seq)
    y_attn = _attn(q, k, v, qi, ki8, wi, bsz, seq, tq=tiles["attn_q"], tk=tiles["attn_k"])
    r1, h2, route = _mix(y4, y_attn, gs, ga, x2, w_glu[0], b_glu[0], w_ssm_br[0], w_attn_br[0], w_o[0],
                         norm_ffn[0], w_router[0], b_router[0], tm=tiles["mix_rows"])
    out = _moe(h2, route, r1, p[0].reshape(t, PLE_DIM), w_gate[0], b_gate[0], w_up[0], b_up[0], w_down[0],
               b_down[0], w_ple_gate[0], w_ple_proj[0], norm_final, tmx=tiles["expert_rows"])
    return out.reshape(bsz, seq, D_MODEL)
```

```python
import functools
import math

import jax
import jax.numpy as jnp
from jax import lax
from jax.experimental import pallas as pl
from jax.experimental.pallas import tpu as pltpu

F32 = jnp.float32
BF16 = jnp.bfloat16
I32 = jnp.int32

D_MODEL = 1024
CHUNK = 64
PLE_DIM = 256
EPS = 1e-6
SSM_WIDTH = 512
SSM_GROUP = 16
SSM_GROUPS = 32
SSM_STATE = 64
N_HEADS = 8
HEAD_DIM = 64
ATTN_WIDTH = 512
IDX_HEADS = 8
IDX_DIM = 32
TOPK_MAX = 256
ROPE_THETA = 10000.0
N_EXPERTS = 32
TOP_K = 4
SWIGLU_LIMIT = 7.0
SWIGLU_ALPHA = 1.702

LANES = 128
SSM_TC = 16
SSM_GL = LANES // SSM_GROUP
SSM_NLB = SSM_WIDTH // LANES
NEG_BIG = -1e30
LOG2E = 1.4426950408889634
CHUNK_SHIFT = CHUNK.bit_length() - 1
VMEM_LIMIT = 56 * 1024 * 1024
MOE_TT = 512
MOE_SEG = 8
MOE_NL = MOE_TT * TOP_K + N_EXPERTS * MOE_SEG
MOE_W = D_MODEL + LANES
_SEG_SIZES = tuple(MOE_SEG << b for b in range((MOE_TT // MOE_SEG).bit_length() - 1, -1, -1))

_C_U, _C_Q, _C_K, _C_V, _C_QI, _C_KI, _C_WI, _C_GS, _C_GA = 0, 512, 1024, 1536, 2048, 2304, 2560, 2688, 3712
_C_END = 4736


def _rms(x, g):
    return x * lax.rsqrt(jnp.mean(x * x, axis=-1, keepdims=True) + EPS) * g


def _dot(a, b):
    return jnp.dot(a, b, preferred_element_type=F32)


def _dot_nt(a, b):
    return lax.dot_general(a, b, (((1,), (1,)), ((), ())), preferred_element_type=F32)


def _inproj_kernel(x_ref, g_ref, w_ref, bg_ref, cq_ref, sq_ref, ci_ref, si_ref,
                   u_ref, q_ref, k_ref, v_ref, qi_ref, ki_ref, wi_ref, gs_ref, ga_ref):
    h = _rms(x_ref[...], g_ref[...]).astype(BF16)

    def mm(c0, n):
        return _dot(h, w_ref[:, c0:c0 + n])

    u = mm(_C_U, 512)
    for lb in range(SSM_NLB):
        u_ref[lb] = u[:, lb * LANES:(lb + 1) * LANES]

    def rope(z, cos, sin, d):
        n = z.shape[1]
        lane = lax.broadcasted_iota(I32, z.shape, 1)
        partner = jnp.where((lane & (d - 1)) < d // 2, pltpu.roll(z, n - d // 2, 1), pltpu.roll(z, d // 2, 1))
        return z * cos + partner * sin

    cq = cq_ref[...]
    sq = sq_ref[...]
    q_ref[...] = (rope(mm(_C_Q, 512), cq, sq, HEAD_DIM) * (HEAD_DIM ** -0.5 * LOG2E)).astype(BF16)
    k_ref[...] = rope(mm(_C_K, 512), cq, sq, HEAD_DIM).astype(BF16)
    v_ref[...] = mm(_C_V, 512).astype(BF16)
    ci = ci_ref[...]
    si = si_ref[...]
    qi_ref[...] = rope(mm(_C_QI, 256), ci, si, IDX_DIM).astype(BF16)
    ki_ref[...] = rope(mm(_C_KI, 256), ci, si, IDX_DIM).astype(BF16)
    wi_ref[...] = mm(_C_WI, 128) * ((IDX_HEADS * IDX_DIM) ** -0.5)
    bg = bg_ref[...]
    gs_ref[...] = jax.nn.sigmoid(mm(_C_GS, 1024) + bg[:, :1024]).astype(BF16)
    ga_ref[...] = jax.nn.sigmoid(mm(_C_GA, 1024) + bg[:, 1024:]).astype(BF16)


def _rope_tables(seq, d, reps):
    half = d // 2
    inv = ROPE_THETA ** (-jnp.arange(half, dtype=F32) * 2.0 / d)
    ang = jnp.arange(seq, dtype=F32)[:, None] * inv[None, :]
    cos, sin = jnp.cos(ang), jnp.sin(ang)
    c = jnp.concatenate([cos, cos], axis=-1)
    s = jnp.concatenate([-sin, sin], axis=-1)
    return jnp.tile(c, (1, reps)), jnp.tile(s, (1, reps))


def _inproj(x2, norm_mix, w_in, b_gates, seq, tm):
    t = x2.shape[0]
    w_u, w_q, w_k, w_v, w_qi, w_ki, w_wi, w_gs, w_ga = jnp.split(
        w_in, [512, 1024, 1536, 2048, 2304, 2336, 2344, 3368], axis=1)
    w_ki8 = jnp.tile(w_ki, (1, IDX_HEADS))
    w_wi_p = jnp.pad(w_wi, ((0, 0), (0, LANES - IDX_HEADS)))
    w_all = jnp.concatenate([w_u, w_q, w_k, w_v, w_qi, w_ki8, w_wi_p, w_gs, w_ga], axis=1).astype(BF16)
    assert w_all.shape[1] == _C_END
    cq, sq = _rope_tables(seq, HEAD_DIM, N_HEADS)
    ci, si = _rope_tables(seq, IDX_DIM, IDX_HEADS)
    nt = seq // tm
    row = lambda i: (i, 0)
    pos = lambda i: (i % nt, 0)
    const = lambda i: (0, 0)
    outs = pl.pallas_call(
        _inproj_kernel,
        grid=(t // tm,),
        in_specs=[
            pl.BlockSpec((tm, D_MODEL), row),
            pl.BlockSpec((1, D_MODEL), const),
            pl.BlockSpec((D_MODEL, _C_END), const),
            pl.BlockSpec((1, 2 * D_MODEL), const),
            pl.BlockSpec((tm, 512), pos), pl.BlockSpec((tm, 512), pos),
            pl.BlockSpec((tm, 256), pos), pl.BlockSpec((tm, 256), pos),
        ],
        out_specs=[
            pl.BlockSpec((SSM_NLB, tm, LANES), lambda i: (0, i, 0)),
            pl.BlockSpec((tm, 512), row), pl.BlockSpec((tm, 512), row), pl.BlockSpec((tm, 512), row),
            pl.BlockSpec((tm, 256), row), pl.BlockSpec((tm, 256), row), pl.BlockSpec((tm, LANES), row),
            pl.BlockSpec((tm, D_MODEL), row), pl.BlockSpec((tm, D_MODEL), row),
        ],
        out_shape=[
            jax.ShapeDtypeStruct((SSM_NLB, t, LANES), F32),
            jax.ShapeDtypeStruct((t, 512), BF16), jax.ShapeDtypeStruct((t, 512), BF16),
            jax.ShapeDtypeStruct((t, 512), BF16),
            jax.ShapeDtypeStruct((t, 256), BF16), jax.ShapeDtypeStruct((t, 256), BF16),
            jax.ShapeDtypeStruct((t, LANES), F32),
            jax.ShapeDtypeStruct((t, D_MODEL), BF16), jax.ShapeDtypeStruct((t, D_MODEL), BF16),
        ],
        compiler_params=pltpu.CompilerParams(dimension_semantics=("arbitrary",),
                                             vmem_limit_bytes=VMEM_LIMIT),
        name="inproj",
    )(x2, norm_mix.reshape(1, D_MODEL), w_all, b_gates.reshape(1, 2 * D_MODEL), cq, sq, ci, si)
    return outs


def _ssm_mats(lam_re, lam_im, log_dt, b_re, b_im, c_re, c_im, d_skip):
    g_, p_, h_, tc, gl, nlb = SSM_GROUPS, SSM_STATE, SSM_GROUP, SSM_TC, SSM_GL, SSM_NLB
    lam = lax.complex(lam_re.astype(F32), lam_im.astype(F32))
    dt = jnp.exp(log_dt.astype(F32))[:, None]
    lam_dt = lam * dt
    lam_bar = jnp.exp(lam_dt)
    b_bar = ((lam_bar - 1.0) / lam)[..., None] * lax.complex(b_re.astype(F32), b_im.astype(F32))
    c = lax.complex(c_re.astype(F32), c_im.astype(F32))
    steps = jnp.arange(tc + 1, dtype=F32)
    pw = jnp.exp(lam_dt[None] * steps[:, None, None])
    hp = lax.Precision.HIGHEST
    lane_g = jnp.arange(LANES) // h_
    st_g = (jnp.arange(2 * gl * p_) % (gl * p_)) // p_
    rep_o = (jnp.arange(h_)[:, None] == (jnp.arange(LANES) % h_)[None, :]).astype(F32)
    st_col = (jnp.arange(2 * gl * p_) // (gl * p_)) * p_ + jnp.arange(2 * gl * p_) % p_
    rep_s = (jnp.arange(2 * p_)[:, None] == st_col[None, :]).astype(F32)
    taps = jnp.einsum('gop,tgp,gpi->gtoi', c, pw[:tc], b_bar).real
    a3 = taps.reshape(nlb, gl, tc, h_, h_).transpose(0, 2, 1, 4, 3).reshape(nlb, tc, LANES, h_)
    d = jnp.einsum('btro,oc->btrc', a3, rep_o, precision=hp)
    d = d * (lane_g[:, None] == lane_g[None, :]).astype(F32)
    dcat = d.transpose(0, 2, 1, 3).reshape(nlb, LANES, tc * LANES)
    m_intra = jnp.stack(
        [jnp.pad(dcat[:, :, :(tc - j) * LANES], ((0, 0), (0, 0), (j * LANES, 0))) for j in range(tc)],
        axis=1).reshape(nlb, tc * LANES, tc * LANES)
    sc = pw[:tc][::-1][:, :, :, None] * b_bar[None]
    sc = jnp.stack([sc.real, sc.imag], axis=0).reshape(2, tc, nlb, gl, p_, h_)
    a_s = sc.transpose(2, 1, 3, 5, 0, 4).reshape(nlb, tc, LANES, 2 * p_)
    m_state = jnp.einsum('bjrq,qc->bjrc', a_s, rep_s, precision=hp)
    m_state = (m_state * (lane_g[:, None] == st_g[None, :]).astype(F32)).reshape(nlb, tc * LANES, 2 * gl * p_)
    oc = c[None] * pw[1:tc + 1][:, :, None, :]
    oc = jnp.stack([oc.real, -oc.imag], axis=0).reshape(2, tc, nlb, gl, h_, p_)
    a_o = oc.transpose(2, 1, 0, 3, 5, 4).reshape(nlb, tc, 2 * gl * p_, h_)
    m_out = jnp.einsum('blro,oc->blrc', a_o, rep_o, precision=hp)
    m_out = m_out * (st_g[:, None] == lane_g[None, :]).astype(F32)
    m_out = m_out.transpose(0, 2, 1, 3).reshape(nlb, 2 * gl * p_, tc * LANES)
    a = pw[tc].reshape(nlb, 1, gl * p_)
    d_l = jnp.tile(d_skip.astype(F32).reshape(nlb, 1, LANES), (1, 1, tc))
    return (m_intra.astype(BF16), m_state.astype(BF16), m_out.astype(BF16),
            a.real.astype(F32), a.imag.astype(F32), d_l)


def _ssm_kernel(u_ref, mi_ref, ms_ref, mo_ref, are_ref, aim_ref, d_ref, y_ref, con_ref, sp_ref):
    nch = u_ref.shape[0] // SSM_TC
    ns = are_ref.shape[1]
    uf = jnp.concatenate([u_ref[pl.ds(j, nch, stride=SSM_TC), :] for j in range(SSM_TC)], axis=1)
    u = uf.astype(BF16)
    con_ref[...] = _dot(u, ms_ref[...])
    a_re = are_ref[...]
    a_im = aim_ref[...]

    def step(c, carry):
        s_re, s_im = carry
        sp_ref[pl.ds(c, 1), :ns] = s_re
        sp_ref[pl.ds(c, 1), ns:] = s_im
        c_re = con_ref[pl.ds(c, 1), :ns]
        c_im = con_ref[pl.ds(c, 1), ns:]
        return (a_re * s_re - a_im * s_im + c_re, a_re * s_im + a_im * s_re + c_im)

    zero = jnp.zeros((1, ns), F32)
    lax.fori_loop(0, nch, step, (zero, zero))
    y = _dot(u, mi_ref[...]) + _dot(sp_ref[...].astype(BF16), mo_ref[...]) + d_ref[...] * uf
    for l in range(SSM_TC):
        y_ref[pl.ds(l, nch, stride=SSM_TC), :] = y[:, l * LANES:(l + 1) * LANES]


def _ssm(u4, mats, bsz, seq):
    m_intra, m_state, m_out, a_re, a_im, d_l = mats
    t = u4.shape[1]
    nch = seq // SSM_TC
    wc = SSM_TC * LANES
    ns = SSM_GL * SSM_STATE
    wmap = lambda lb, b: (lb, 0, 0)
    return pl.pallas_call(
        _ssm_kernel,
        grid=(SSM_NLB, bsz),
        in_specs=[
            pl.BlockSpec((None, seq, LANES), lambda lb, b: (lb, b, 0)),
            pl.BlockSpec((None, wc, wc), wmap),
            pl.BlockSpec((None, wc, 2 * ns), wmap),
            pl.BlockSpec((None, 2 * ns, wc), wmap),
            pl.BlockSpec((None, 1, ns), wmap), pl.BlockSpec((None, 1, ns), wmap),
            pl.BlockSpec((None, 1, wc), wmap),
        ],
        out_specs=pl.BlockSpec((None, seq, LANES), lambda lb, b: (lb, b, 0)),
        out_shape=jax.ShapeDtypeStruct((SSM_NLB, t, LANES), F32),
        scratch_shapes=[pltpu.VMEM((nch, 2 * ns), F32), pltpu.VMEM((nch, 2 * ns), F32)],
        compiler_params=pltpu.CompilerParams(dimension_semantics=("arbitrary", "arbitrary"),
                                             vmem_limit_bytes=VMEM_LIMIT),
        name="ssm",
    )(u4, m_intra, m_state, m_out, a_re, a_im, d_l)


def _attn_kernel(q_ref, k_ref, v_ref, qi_ref, ki_ref, wi_ref, tri_ref, o_ref, keys_ref, kt_ref, k16_ref,
                 qh_ref, qm_ref, acc_ref, m_ref, *, tq, tk, seq, topk):
    i = pl.program_id(1)
    s0 = i * tq
    nk = (s0 + tq + tk - 1) // tk
    row = s0 + lax.broadcasted_iota(I32, (tq, 1), 0)
    lim = ((row >> CHUNK_SHIFT) + 1) << CHUNK_SHIFT
    lim_q = (((s0 + lax.broadcasted_iota(I32, (1, tq), 1)) >> CHUNK_SHIFT) + 1) << CHUNK_SHIFT
    kvec = jnp.minimum(lim_q, topk).astype(F32)
    col0 = lax.broadcasted_iota(I32, (tq, tk), 1)

    qi = qi_ref[...]
    lane_i = lax.broadcasted_iota(I32, qi.shape, 1)
    for h in range(IDX_HEADS):
        qh_ref[h] = jnp.where((lane_i >> 5) == h, qi, jnp.zeros_like(qi))
    wi = wi_ref[...]
    wcol = [wi[:, h:h + 1] for h in range(IDX_HEADS)]

    def score_tile(j, carry):
        kt = ki_ref[pl.ds(pl.multiple_of(j * tk, tk), tk), :]
        acc = jnp.zeros((tq, tk), F32)
        for h in range(IDX_HEADS):
            acc = acc + wcol[h] * jnp.maximum(_dot_nt(qh_ref[h], kt), 0.0)
        sc = jnp.where(col0 + j * tk < lim, acc, -jnp.inf)
        bits = pltpu.bitcast(sc, I32)
        key = bits ^ ((bits >> 31) & 0x7FFFFFFF)
        keys_ref[j] = key
        key_t = key.T
        kt_ref[j] = key_t
        k16_ref[j] = (key_t >> 16).astype(jnp.int16)
        return carry

    lax.fori_loop(0, nk, score_tile, 0)

    rg = 16
    one16 = jnp.ones((rg, tq), jnp.int16)
    zero16 = jnp.zeros((rg, tq), jnp.int16)

    def count16(cand):
        cand16 = jnp.broadcast_to(cand, (rg, tq)).astype(jnp.int16)

        def body(j, part):
            kk = k16_ref[j]
            for r in range(tk // rg):
                part = part + jnp.where(kk[r * rg:(r + 1) * rg, :] >= cand16, one16, zero16)
            return part
        part = lax.fori_loop(0, nk, body, jnp.zeros((rg, tq), jnp.int16))
        return jnp.sum(part.astype(F32), axis=0, keepdims=True)

    def bit_step(it, carry):
        u_ans, g_lo, g_hi = carry
        cand_u = u_ans | lax.shift_left(jnp.int32(1), 15 - it)
        cnt = count16(cand_u - 32768)
        ok = cnt >= kvec
        return jnp.where(ok, cand_u, u_ans), jnp.where(ok, cnt, g_lo), jnp.where(ok, g_hi, cnt)

    u_ans, g_lo, g_hi = lax.fori_loop(
        0, 16, bit_step, (jnp.zeros((1, tq), I32), lim_q.astype(F32), jnp.zeros((1, tq), F32)))
    b16 = u_ans - 32768

    def rebase(j, carry):
        kk = kt_ref[j]
        hi16 = kk >> 16
        low = (kk & 0xFFFF) - 32768
        rel = jnp.where(hi16 > b16, 32767, jnp.where(hi16 < b16, -32768, low))
        k16_ref[j] = rel.astype(jnp.int16)
        return carry

    lax.fori_loop(0, nk, rebase, 0)

    def unfinished(lo, hi, g_lo):
        return jnp.where(g_lo != kvec, jnp.where(hi - lo > 1, 1.0, 0.0), 0.0)

    def refine_cond(carry):
        return (carry[0] < 18) & (carry[1] > 0.0)

    def refine(carry):
        it, _, lo, hi, g_lo, g_hi = carry
        todo = unfinished(lo, hi, g_lo)
        mid = lo + ((hi - lo) >> 1)
        cand = jnp.where(b16 == 0, jnp.where(hi - lo == 65536, 1, mid), mid)
        cnt = count16(cand - 32768)
        up = jnp.where(cnt >= kvec, todo, 0.0) > 0.0
        dn = jnp.where(cnt < kvec, todo, 0.0) > 0.0
        lo = jnp.where(up, cand, lo)
        g_lo = jnp.where(up, cnt, g_lo)
        hi = jnp.where(dn, cand, hi)
        g_hi = jnp.where(dn, cnt, g_hi)
        return it + 1, jnp.max(unfinished(lo, hi, g_lo)), lo, hi, g_lo, g_hi

    off_lo = jnp.zeros((1, tq), I32)
    off_hi = jnp.full((1, tq), 65536, I32)
    _, _, off_lo, _, g_lo, g_hi = lax.while_loop(
        refine_cond, refine,
        (jnp.int32(0), jnp.max(unfinished(off_lo, off_hi, g_lo)), off_lo, off_hi, g_lo, g_hi))
    has_tie = jnp.max(g_lo - kvec) > 0.0

    def to_rows(x):
        return jnp.transpose(jnp.broadcast_to(x, (LANES, tq)))

    vth = to_rows((b16 << 16) + off_lo)

    @pl.when(jnp.logical_not(has_tie))
    def _():
        def to_bias(j, carry):
            kk = keys_ref[j]
            for c in range(tk // LANES):
                sl = slice(c * LANES, (c + 1) * LANES)
                keys_ref[j, :, sl] = pltpu.bitcast(jnp.where(kk[:, sl] >= vth, 0.0, NEG_BIG).astype(F32), I32)
            return carry
        lax.fori_loop(0, nk, to_bias, 0)

    @pl.when(has_tie)
    def _():
        need = to_rows(kvec - g_hi)

        def to_bias(j, seen):
            kk = keys_ref[j]
            eq = jnp.concatenate(
                [jnp.where(kk[:, c * LANES:(c + 1) * LANES] == vth, 1.0, 0.0) for c in range(tk // LANES)],
                axis=1)
            rank = _dot(eq.astype(BF16), tri_ref[...])
            for c in range(tk // LANES):
                sl = slice(c * LANES, (c + 1) * LANES)
                take = jnp.where(seen + rank[:, sl] <= need, eq[:, sl], 0.0)
                sel = jnp.where(kk[:, sl] > vth, 1.0, take)
                keys_ref[j, :, sl] = pltpu.bitcast(jnp.where(sel > 0.0, 0.0, NEG_BIG).astype(F32), I32)
            return seen + jnp.broadcast_to(rank[:, tk - 1:tk], (tq, LANES))
        lax.fori_loop(0, nk, to_bias, jnp.zeros((tq, LANES), F32))

    lane_q = lax.broadcasted_iota(I32, (tq, LANES), 1)
    lane_k = lax.broadcasted_iota(I32, (tk, LANES), 1)
    for h in range(N_HEADS):
        qp = q_ref[:, (h // 2) * LANES:(h // 2 + 1) * LANES]
        keep = (lane_q >= HEAD_DIM) if h % 2 else (lane_q < HEAD_DIM)
        qm_ref[h] = jnp.where(keep, qp, jnp.zeros_like(qp))
    acc_ref[...] = jnp.zeros_like(acc_ref)
    m_ref[...] = jnp.full(m_ref.shape, NEG_BIG, F32)
    ones_v = jnp.ones((tk, LANES), BF16)

    def attn_tile(j, carry):
        rows = pl.ds(pl.multiple_of(j * tk, tk), tk)
        bias = pltpu.bitcast(keys_ref[j], F32)
        for h in range(N_HEADS):
            cs = slice((h // 2) * LANES, (h // 2 + 1) * LANES)
            own = (lane_k >= HEAD_DIM) if h % 2 else (lane_k < HEAD_DIM)
            s = _dot_nt(qm_ref[h], k_ref[rows, cs]) + bias
            m_old = m_ref[h]
            m_new = jnp.maximum(m_old, jnp.max(s, axis=1, keepdims=True))
            p = jnp.concatenate(
                [jnp.exp2(s[:, c * LANES:(c + 1) * LANES] - m_new) for c in range(tk // LANES)],
                axis=1).astype(BF16)
            vh = jnp.where(own, v_ref[rows, cs], ones_v)
            acc_ref[h] = jnp.exp2(m_old - m_new) * acc_ref[h] + _dot(p, vh)
            m_ref[h] = m_new
        return carry

    lax.fori_loop(0, nk, attn_tile, 0)

    for hp in range(N_HEADS // 2):
        a0 = acc_ref[2 * hp]
        a1 = acc_ref[2 * hp + 1]
        out = jnp.where(lane_q < HEAD_DIM, a0 / pltpu.roll(a0, HEAD_DIM, 1), a1 / pltpu.roll(a1, HEAD_DIM, 1))
        o_ref[:, hp * LANES:(hp + 1) * LANES] = out.astype(o_ref.dtype)


def _attn(q, k, v, qi, ki8, wi, bsz, seq, tq, tk):
    t = q.shape[0]
    nq = seq // tq
    qrow = lambda b, i: (b * nq + i, 0)
    kv = lambda b, i: (b, 0)
    kern = functools.partial(_attn_kernel, tq=tq, tk=tk, seq=seq, topk=min(TOPK_MAX, seq // 4))
    tri = (jnp.arange(tk)[:, None] <= jnp.arange(tk)[None, :]).astype(BF16)
    return pl.pallas_call(
        kern,
        grid=(bsz, nq),
        in_specs=[
            pl.BlockSpec((tq, ATTN_WIDTH), qrow),
            pl.BlockSpec((seq, ATTN_WIDTH), kv), pl.BlockSpec((seq, ATTN_WIDTH), kv),
            pl.BlockSpec((tq, 256), qrow), pl.BlockSpec((seq, 256), kv),
            pl.BlockSpec((tq, LANES), qrow),
            pl.BlockSpec((tk, tk), lambda b, i: (0, 0)),
        ],
        out_specs=pl.BlockSpec((tq, ATTN_WIDTH), qrow),
        out_shape=jax.ShapeDtypeStruct((t, ATTN_WIDTH), BF16),
        scratch_shapes=[
            pltpu.VMEM((seq // tk, tq, tk), I32),
            pltpu.VMEM((seq // tk, tk, tq), I32),
            pltpu.VMEM((seq // tk, tk, tq), jnp.int16),
            pltpu.VMEM((IDX_HEADS, tq, 256), BF16),
            pltpu.VMEM((N_HEADS, tq, LANES), BF16),
            pltpu.VMEM((N_HEADS, tq, LANES), F32),
            pltpu.VMEM((N_HEADS, tq, LANES), F32),
        ],
        compiler_params=pltpu.CompilerParams(dimension_semantics=("arbitrary", "arbitrary"),
                                             vmem_limit_bytes=VMEM_LIMIT),
        name="attn",
    )(q, k, v, qi, ki8, wi, tri)


def _mix_kernel(y4_ref, ya_ref, gs_ref, ga_ref, x_ref, wglu_ref, bglu_ref, wsb_ref, wab_ref, wo_ref,
                nf_ref, wr_ref, br_ref, r1_ref, h2_ref, route_ref):
    ys = jnp.concatenate([y4_ref[lb] for lb in range(SSM_NLB)], axis=1)
    ys = 0.5 * ys * (1.0 + jnp.tanh(math.sqrt(2.0 / math.pi) * (ys + 0.044715 * (ys * ys * ys))))
    ys = ys * jax.nn.sigmoid(_dot(ys.astype(BF16), wglu_ref[...]) + bglu_ref[...])
    br_s = _dot(ys.astype(BF16), wsb_ref[...])
    br_a = _dot(ya_ref[...], wab_ref[...])
    mix = gs_ref[...].astype(F32) * br_s + ga_ref[...].astype(F32) * br_a
    r1 = x_ref[...] + _dot(mix.astype(BF16), wo_ref[...])
    r1_ref[...] = r1
    h2 = _rms(r1, nf_ref[...])
    h2_ref[...] = h2.astype(BF16)
    hi = h2.astype(BF16)
    lo = (h2 - hi.astype(F32)).astype(BF16)
    hw = _dot(hi, wr_ref[...])
    logits = hw[:, :LANES] + (hw[:, LANES:] + _dot(lo, wr_ref[:, :LANES])) + br_ref[...]
    tm = logits.shape[0]
    lane = lax.broadcasted_iota(I32, (tm, LANES), 1).astype(F32)
    vals = logits
    tops, idxs = [], []
    for _ in range(TOP_K):
        m = jnp.max(vals, axis=1, keepdims=True)
        idx = jnp.min(jnp.where(vals == m, lane, float(LANES)), axis=1, keepdims=True)
        tops.append(m)
        idxs.append(idx)
        vals = jnp.where(lane == idx, -jnp.inf, vals)
    es = [jnp.exp(tv - tops[0]) for tv in tops]
    den = es[0] + es[1] + es[2] + es[3]
    route = jnp.zeros((tm, LANES), F32)
    for kk in range(TOP_K):
        w = es[kk] / den
        w_hi = w.astype(BF16).astype(F32)
        w_mid = (w - w_hi).astype(BF16).astype(F32)
        w_lo = (w - w_hi) - w_mid
        route = jnp.where(lane == float(kk), idxs[kk], route)
        route = jnp.where(lane == float(TOP_K + kk), w_hi, route)
        route = jnp.where(lane == float(2 * TOP_K + kk), w_mid, route)
        route = jnp.where(lane == float(3 * TOP_K + kk), w_lo, route)
    route_ref[...] = route.astype(BF16)


def _mix(y4, y_attn, gs, ga, x2, w_glu, b_glu, w_ssm_br, w_attn_br, w_o, norm_ffn, w_router, b_router, tm):
    t = x2.shape[0]
    wr = jnp.pad(w_router.astype(F32), ((0, 0), (0, LANES - N_EXPERTS)))
    wr_hi = wr.astype(BF16)
    wr_split = jnp.concatenate([wr_hi, (wr - wr_hi.astype(F32)).astype(BF16)], axis=1)
    br = jnp.concatenate([b_router.astype(F32), jnp.full((LANES - N_EXPERTS,), NEG_BIG, F32)]).reshape(1, LANES)
    row = lambda i: (i, 0)
    const = lambda i: (0, 0)
    return pl.pallas_call(
        _mix_kernel,
        grid=(t // tm,),
        in_specs=[
            pl.BlockSpec((SSM_NLB, tm, LANES), lambda i: (0, i, 0)),
            pl.BlockSpec((tm, ATTN_WIDTH), row),
            pl.BlockSpec((tm, D_MODEL), row), pl.BlockSpec((tm, D_MODEL), row), pl.BlockSpec((tm, D_MODEL), row),
            pl.BlockSpec((SSM_WIDTH, SSM_WIDTH), const), pl.BlockSpec((1, SSM_WIDTH), const),
            pl.BlockSpec((SSM_WIDTH, D_MODEL), const), pl.BlockSpec((ATTN_WIDTH, D_MODEL), const),
            pl.BlockSpec((D_MODEL, D_MODEL), const), pl.BlockSpec((1, D_MODEL), const),
            pl.BlockSpec((D_MODEL, 2 * LANES), const),
            pl.BlockSpec((1, LANES), const),
        ],
        out_specs=[pl.BlockSpec((tm, D_MODEL), row), pl.BlockSpec((tm, D_MODEL), row),
                   pl.BlockSpec((tm, LANES), row)],
        out_shape=[jax.ShapeDtypeStruct((t, D_MODEL), F32), jax.ShapeDtypeStruct((t, D_MODEL), BF16),
                   jax.ShapeDtypeStruct((t, LANES), BF16)],
        compiler_params=pltpu.CompilerParams(dimension_semantics=("arbitrary",),
                                             vmem_limit_bytes=VMEM_LIMIT),
        name="mix",
    )(y4, y_attn, gs, ga, x2, w_glu.astype(BF16), b_glu.reshape(1, SSM_WIDTH).astype(F32),
      w_ssm_br.astype(BF16), w_attn_br.astype(BF16), w_o.astype(BF16), norm_ffn.reshape(1, D_MODEL),
      wr_split, br)


def _prefix_kernel(route_ref, tri_ref, utri_ref, lpos_ref, lpos_t_ref, cnt_ref):
    route = route_ref[...].astype(F32)
    tm = route.shape[0]
    lane = lax.broadcasted_iota(I32, (tm, LANES), 1).astype(F32)
    mask = jnp.zeros((tm, LANES), F32)
    for kk in range(TOP_K):
        mask = jnp.where(lane == route[:, kk:kk + 1], 1.0, mask)
    incl = _dot(tri_ref[...], mask.astype(BF16))
    cnt = incl[tm - 1:tm, :]
    cnt_u = jnp.floor((cnt + (MOE_SEG - 1)) * (1.0 / MOE_SEG))
    loc = _dot(jnp.broadcast_to(cnt_u, (8, LANES)).astype(BF16), utri_ref[...])[:1, :] * float(MOE_SEG)
    where_to = loc + incl - mask
    lpos = jnp.full((tm, LANES), -1.0, F32)
    for kk in range(TOP_K):
        pos = jnp.sum(jnp.where(lane == route[:, kk:kk + 1], where_to, 0.0), axis=1, keepdims=True)
        lpos = jnp.where(lane == float(kk), pos, lpos)
    lpos_ref[...] = lpos.astype(I32)
    lpos_t_ref[...] = jnp.transpose(lpos)[:8, :].astype(I32)
    cnt_ref[...] = jnp.broadcast_to(cnt, (8, LANES)).astype(I32)


def _prefix(route):
    t = route.shape[0]
    tm = MOE_TT
    nt = t // tm
    tri = (jnp.arange(tm)[:, None] >= jnp.arange(tm)[None, :]).astype(BF16)
    utri = (jnp.arange(LANES)[:, None] < jnp.arange(LANES)[None, :]).astype(BF16)
    return pl.pallas_call(
        _prefix_kernel,
        grid=(nt,),
        in_specs=[pl.BlockSpec((tm, LANES), lambda i: (i, 0)), pl.BlockSpec((tm, tm), lambda i: (0, 0)),
                  pl.BlockSpec((LANES, LANES), lambda i: (0, 0))],
        out_specs=[pl.BlockSpec((tm, LANES), lambda i: (i, 0)), pl.BlockSpec((None, 8, tm), lambda i: (i, 0, 0)),
                   pl.BlockSpec((None, 8, LANES), lambda i: (i, 0, 0))],
        out_shape=[jax.ShapeDtypeStruct((t, LANES), I32), jax.ShapeDtypeStruct((nt, 8, tm), I32),
                   jax.ShapeDtypeStruct((nt, 8, LANES), I32)],
        compiler_params=pltpu.CompilerParams(dimension_semantics=("arbitrary",)),
        name="prefix",
    )(route, tri, utri)


def _segment_copies(seg_ref, e, make_copy, fn):
    c = seg_ref[0, e]
    s = seg_ref[0, N_EXPERTS + e]
    d = seg_ref[0, 2 * N_EXPERTS + e]
    for size in _SEG_SIZES:
        units = size // MOE_SEG
        hit = (c & units) != 0

        @pl.when(hit)
        def _():
            fn(make_copy(pl.multiple_of(s * MOE_SEG, MOE_SEG), pl.multiple_of(d * MOE_SEG, MOE_SEG), size))
        s = s + jnp.where(hit, units, 0)
        d = d + jnp.where(hit, units, 0)


def _dispatch_kernel(zt_ref, seg_ref, seg_prev_ref, lpos_ref, h2_ref, wtok_ref, xs_ref, buf_ref, zbuf_ref,
                     sem, zsem):
    tmx = zbuf_ref.shape[0]
    i = pl.program_id(0)
    slot = i % 2

    @pl.when(i == 0)
    def _():
        zbuf_ref[...] = jnp.zeros_like(zbuf_ref)

        def zero_copy(z):
            return pltpu.make_async_copy(
                zbuf_ref, xs_ref.at[pl.ds(pl.multiple_of(zt_ref[z] * tmx, tmx), tmx)], zsem)

        for z in range(zt_ref.shape[0]):
            @pl.when(zt_ref[z] >= 0)
            def _():
                zero_copy(z).start()
        for z in range(zt_ref.shape[0]):
            @pl.when(zt_ref[z] >= 0)
            def _():
                zero_copy(z).wait()

    rowi = lax.broadcasted_iota(I32, (MOE_NL, MOE_TT), 0)
    sel = jnp.zeros((MOE_NL, MOE_TT), F32)
    for kk in range(TOP_K):
        sel = jnp.where(rowi == lpos_ref[kk:kk + 1, :], 1.0, sel)
    src = jnp.concatenate([h2_ref[...], wtok_ref[...]], axis=1)
    buf_ref[slot] = _dot(sel.astype(BF16), src)

    def copies(table_ref, which, fn):
        def make_copy(s, d, size):
            return pltpu.make_async_copy(buf_ref.at[which, pl.ds(s, size)], xs_ref.at[pl.ds(d, size)],
                                         sem.at[which])

        def body(e, carry):
            _segment_copies(table_ref, e, make_copy, fn)
            return carry
        lax.fori_loop(0, N_EXPERTS, body, 0)

    @pl.when(i > 0)
    def _():
        copies(seg_prev_ref, 1 - slot, lambda cp: cp.wait())

    copies(seg_ref, slot, lambda cp: cp.start())

    @pl.when(i == pl.num_programs(0) - 1)
    def _():
        copies(seg_ref, slot, lambda cp: cp.wait())


def _dispatch(zero_tiles, seg, lpos_t, h2, wtok, n_rows, tmx):
    t = h2.shape[0]
    grid_spec = pltpu.PrefetchScalarGridSpec(
        num_scalar_prefetch=1,
        grid=(t // MOE_TT,),
        in_specs=[
            pl.BlockSpec((None, 1, 3 * N_EXPERTS), lambda i, zt: (i, 0, 0), memory_space=pltpu.SMEM),
            pl.BlockSpec((None, 1, 3 * N_EXPERTS), lambda i, zt: (jnp.maximum(i - 1, 0), 0, 0),
                         memory_space=pltpu.SMEM),
            pl.BlockSpec((None, 8, MOE_TT), lambda i, zt: (i, 0, 0)),
            pl.BlockSpec((MOE_TT, D_MODEL), lambda i, zt: (i, 0)),
            pl.BlockSpec((MOE_TT, LANES), lambda i, zt: (i, 0)),
        ],
        out_specs=pl.BlockSpec(memory_space=pl.ANY),
        scratch_shapes=[pltpu.VMEM((2, MOE_NL, MOE_W), F32), pltpu.VMEM((tmx, MOE_W), F32),
                        pltpu.SemaphoreType.DMA((2,)), pltpu.SemaphoreType.DMA(())],
    )
    return pl.pallas_call(
        _dispatch_kernel,
        grid_spec=grid_spec,
        out_shape=jax.ShapeDtypeStruct((n_rows, MOE_W), F32),
        compiler_params=pltpu.CompilerParams(dimension_semantics=("arbitrary",), has_side_effects=True,
                                             vmem_limit_bytes=VMEM_LIMIT),
        name="dispatch",
    )(zero_tiles, seg, seg, lpos_t, h2, wtok)


def _expert_kernel(te_ref, tv_ref, x_ref, wg_ref, bg_ref, wu_ref, bu_ref, wd_ref, bd_ref, y_ref,
                   wgb_ref, wub_ref, wdb_ref):
    i = pl.program_id(0)
    e = te_ref[i]
    prev = te_ref[jnp.maximum(i - 1, 0)]

    @pl.when((i == 0) | (e != prev))
    def _():
        wgb_ref[...] = wg_ref[...].astype(BF16)
        wub_ref[...] = wu_ref[...].astype(BF16)
        wdb_ref[...] = wd_ref[...].astype(BF16)

    @pl.when(tv_ref[i] != 0)
    def _():
        x = x_ref[:, :D_MODEL].astype(BF16)
        rt = x_ref[:, D_MODEL:]
        ef = e.astype(F32)
        w_row = jnp.zeros((rt.shape[0], 1), F32)
        for kk in range(TOP_K):
            wk = (rt[:, TOP_K + kk:TOP_K + kk + 1] + rt[:, 2 * TOP_K + kk:2 * TOP_K + kk + 1]
                  + rt[:, 3 * TOP_K + kk:3 * TOP_K + kk + 1])
            w_row = w_row + jnp.where(rt[:, kk:kk + 1] == ef, wk, 0.0)
        g = jnp.minimum(_dot(x, wgb_ref[...]) + bg_ref[...], SWIGLU_LIMIT)
        u = jnp.clip(_dot(x, wub_ref[...]) + bu_ref[...], -SWIGLU_LIMIT, SWIGLU_LIMIT)
        a = (u + 1.0) * (g * jax.nn.sigmoid(SWIGLU_ALPHA * g))
        y_ref[...] = (_dot(a.astype(BF16), wdb_ref[...]) + bd_ref[...]) * w_row

    @pl.when(tv_ref[i] == 0)
    def _():
        y_ref[...] = jnp.zeros_like(y_ref)


def _experts(tile_expert, tile_valid, xs, w_gate, b_gate, w_up, b_up, w_down, b_down, tmx):
    n_rows = xs.shape[0]
    wmap = lambda i, te, tv: (te[i], 0, 0)
    row = lambda i, te, tv: (i, 0)
    d_ff = w_gate.shape[2]
    grid_spec = pltpu.PrefetchScalarGridSpec(
        num_scalar_prefetch=2,
        grid=(n_rows // tmx,),
        in_specs=[
            pl.BlockSpec((tmx, MOE_W), row),
            pl.BlockSpec((None, D_MODEL, d_ff), wmap), pl.BlockSpec((None, 1, d_ff), wmap),
            pl.BlockSpec((None, D_MODEL, d_ff), wmap), pl.BlockSpec((None, 1, d_ff), wmap),
            pl.BlockSpec((None, d_ff, D_MODEL), wmap), pl.BlockSpec((None, 1, D_MODEL), wmap),
        ],
        out_specs=pl.BlockSpec((tmx, D_MODEL), row),
        scratch_shapes=[pltpu.VMEM((D_MODEL, d_ff), BF16), pltpu.VMEM((D_MODEL, d_ff), BF16),
                        pltpu.VMEM((d_ff, D_MODEL), BF16)],
    )
    return pl.pallas_call(
        _expert_kernel,
        grid_spec=grid_spec,
        out_shape=jax.ShapeDtypeStruct((n_rows, D_MODEL), F32),
        compiler_params=pltpu.CompilerParams(dimension_semantics=("arbitrary",),
                                             vmem_limit_bytes=VMEM_LIMIT),
        name="experts",
    )(tile_expert, tile_valid, xs, w_gate, b_gate.reshape(N_EXPERTS, 1, d_ff), w_up,
      b_up.reshape(N_EXPERTS, 1, d_ff), w_down, b_down.reshape(N_EXPERTS, 1, D_MODEL))


def _combine_kernel(seg_ref, seg_next_ref, lpos_ref, r1_ref, p_ref, ys_ref, wpg_ref, wpp_ref, nfin_ref, o_ref,
                    buf_ref, sem):
    i = pl.program_id(0)
    slot = i % 2

    def copies(table_ref, which, fn):
        def make_copy(s, d, size):
            return pltpu.make_async_copy(ys_ref.at[pl.ds(d, size)], buf_ref.at[which, pl.ds(s, size)],
                                         sem.at[which])

        def body(e, carry):
            _segment_copies(table_ref, e, make_copy, fn)
            return carry
        lax.fori_loop(0, N_EXPERTS, body, 0)

    @pl.when(i == 0)
    def _():
        buf_ref[...] = jnp.zeros_like(buf_ref)
        copies(seg_ref, slot, lambda cp: cp.start())

    @pl.when(i + 1 < pl.num_programs(0))
    def _():
        copies(seg_next_ref, 1 - slot, lambda cp: cp.start())

    coli = lax.broadcasted_iota(I32, (MOE_TT, MOE_NL), 1)
    lpos = lpos_ref[...]
    pick = jnp.zeros((MOE_TT, MOE_NL), F32)
    for kk in range(TOP_K):
        pick = jnp.where(coli == lpos[:, kk:kk + 1], 1.0, pick)
    copies(seg_ref, slot, lambda cp: cp.wait())
    moe = _dot(pick.astype(BF16), buf_ref[slot].astype(BF16))
    r2 = r1_ref[...] + moe
    gate = jax.nn.sigmoid(_dot(r2.astype(BF16), wpg_ref[...]))
    r3 = r2 + gate * _dot(p_ref[...].astype(BF16), wpp_ref[...])
    o_ref[...] = _rms(r3, nfin_ref[...])


def _combine(seg, lpos, r1, p2, ys, w_ple_gate, w_ple_proj, norm_final):
    t = r1.shape[0]
    nt = t // MOE_TT
    row = lambda i: (i, 0)
    const = lambda i: (0, 0)
    return pl.pallas_call(
        _combine_kernel,
        grid=(nt,),
        in_specs=[
            pl.BlockSpec((None, 1, 3 * N_EXPERTS), lambda i: (i, 0, 0), memory_space=pltpu.SMEM),
            pl.BlockSpec((None, 1, 3 * N_EXPERTS), lambda i: (jnp.minimum(i + 1, nt - 1), 0, 0),
                         memory_space=pltpu.SMEM),
            pl.BlockSpec((MOE_TT, LANES), row),
            pl.BlockSpec((MOE_TT, D_MODEL), row),
            pl.BlockSpec((MOE_TT, PLE_DIM), row),
            pl.BlockSpec(memory_space=pl.ANY),
            pl.BlockSpec((D_MODEL, D_MODEL), const), pl.BlockSpec((PLE_DIM, D_MODEL), const),
            pl.BlockSpec((1, D_MODEL), const),
        ],
        out_specs=pl.BlockSpec((MOE_TT, D_MODEL), row),
        out_shape=jax.ShapeDtypeStruct((t, D_MODEL), F32),
        scratch_shapes=[pltpu.VMEM((2, MOE_NL, D_MODEL), F32), pltpu.SemaphoreType.DMA((2,))],
        compiler_params=pltpu.CompilerParams(dimension_semantics=("arbitrary",),
                                             vmem_limit_bytes=VMEM_LIMIT),
        name="combine",
    )(seg, seg, lpos, r1, p2, ys, w_ple_gate.astype(BF16), w_ple_proj.astype(BF16),
      norm_final.reshape(1, D_MODEL))


def _moe(h2, route, r1, p2, w_gate, b_gate, w_up, b_up, w_down, b_down, w_ple_gate, w_ple_proj,
         norm_final, tmx):
    t = h2.shape[0]
    nt = t // MOE_TT
    lpos_l, lpos_t, cnt8 = _prefix(route)
    tile_cnt = cnt8[:, 0, :N_EXPERTS]
    cnt_u = (tile_cnt + MOE_SEG - 1) // MOE_SEG
    loc_u = jnp.cumsum(cnt_u, axis=1) - cnt_u
    reg_u = jnp.sum(cnt_u, axis=0)
    upt = tmx // MOE_SEG
    pad_u = ((reg_u + upt - 1) // upt) * upt
    end_u = jnp.cumsum(pad_u)
    glob_u = (end_u - pad_u)[None, :] + jnp.cumsum(cnt_u, axis=0) - cnt_u
    seg = jnp.concatenate([cnt_u, loc_u, glob_u], axis=1).astype(I32).reshape(nt, 1, 3 * N_EXPERTS)
    n_tiles = (t * TOP_K + nt * N_EXPERTS * (MOE_SEG - 1)) // tmx + N_EXPERTS
    ends = end_u * MOE_SEG
    tile_start = jnp.arange(n_tiles, dtype=I32) * tmx
    tile_valid = (tile_start < ends[-1]).astype(I32)
    tile_expert = jnp.minimum(jnp.sum((tile_start[:, None] >= ends[None, :]).astype(I32), axis=1), N_EXPERTS - 1)
    last_e = jnp.max(jnp.where(tile_valid != 0, tile_expert, 0))
    tile_expert = jnp.where(tile_valid != 0, tile_expert, last_e)
    n_spare = n_tiles - (t * TOP_K) // tmx
    used = ends[-1] // tmx
    last_tile = jnp.where(pad_u > 0, ends // tmx - 1, -1)
    spare = used + jnp.arange(n_spare, dtype=I32)
    zero_tiles = jnp.concatenate([last_tile, jnp.where(spare < n_tiles, spare, -1)]).astype(I32)
    xs = _dispatch(zero_tiles, seg, lpos_t, h2, route, n_tiles * tmx, tmx)
    ys = _experts(tile_expert, tile_valid, xs, w_gate, b_gate, w_up, b_up, w_down, b_down, tmx)
    return _combine(seg, lpos_l, r1, p2, ys, w_ple_gate, w_ple_proj, norm_final)


def _tile_plan(seq):
    return {
        "inproj_rows": 256,
        "attn_q": 256,
        "attn_k": min(1024, seq),
        "mix_rows": min(512, seq),
        "expert_rows": 512,
    }


def kernel(x, p, w_in, b_gates, lam_re, lam_im, log_dt, b_re, b_im, c_re, c_im, d_skip, w_glu, b_glu,
           w_ssm_br, w_attn_br, w_o, norm_mix, norm_ffn, w_router, b_router, w_gate, b_gate, w_up, b_up,
           w_down, b_down, w_ple_gate, w_ple_proj, norm_final):
    bsz, seq, _ = x.shape
    t = bsz * seq
    tiles = _tile_plan(seq)
    x2 = x.reshape(t, D_MODEL)
    u4, q, k, v, qi, ki8, wi, gs, ga = _inproj(x2, norm_mix[0], w_in[0], b_gates[0], seq, tm=tiles["inproj_rows"])
    mats = _ssm_mats(lam_re[0], lam_im[0], log_dt[0], b_re[0], b_im[0], c_re[0], c_im[0], d_skip[0])
    y4 = _ssm(u4, mats, bsz, seq)
    y_attn = _attn(q, k, v, qi, ki8, wi, bsz, seq, tq=tiles["attn_q"], tk=tiles["attn_k"])
    r1, h2, route = _mix(y4, y_attn, gs, ga, x2, w_glu[0], b_glu[0], w_ssm_br[0], w_attn_br[0], w_o[0],
                         norm_ffn[0], w_router[0], b_router[0], tm=tiles["mix_rows"])
    out = _moe(h2, route, r1, p[0].reshape(t, PLE_DIM), w_gate[0], b_gate[0], w_up[0], b_up[0], w_down[0],
               b_down[0], w_ple_gate[0], w_ple_proj[0], norm_final, tmx=tiles["expert_rows"])
    return out.reshape(bsz, seq, D_MODEL)
```

```python
import functools
import math

import jax
import jax.numpy as jnp
from jax import lax
from jax.experimental import pallas as pl
from jax.experimental.pallas import tpu as pltpu

F32 = jnp.float32
BF16 = jnp.bfloat16
I32 = jnp.int32

D_MODEL = 1024
CHUNK = 64
PLE_DIM = 256
EPS = 1e-6
SSM_WIDTH = 512
SSM_GROUP = 16
SSM_GROUPS = 32
SSM_STATE = 64
N_HEADS = 8
HEAD_DIM = 64
ATTN_WIDTH = 512
IDX_HEADS = 8
IDX_DIM = 32
TOPK_MAX = 256
ROPE_THETA = 10000.0
N_EXPERTS = 32
TOP_K = 4
SWIGLU_LIMIT = 7.0
SWIGLU_ALPHA = 1.702

LANES = 128
SSM_TC = 16
SSM_GL = LANES // SSM_GROUP
SSM_NLB = SSM_WIDTH // LANES
NEG_BIG = -1e30
LOG2E = 1.4426950408889634
CHUNK_SHIFT = CHUNK.bit_length() - 1
VMEM_LIMIT = 56 * 1024 * 1024
MOE_TT = 512
MOE_SEG = 8
MOE_NL = MOE_TT * TOP_K + N_EXPERTS * MOE_SEG
MOE_W = D_MODEL + LANES
_SEG_SIZES = tuple(MOE_SEG << b for b in range((MOE_TT // MOE_SEG).bit_length() - 1, -1, -1))

_C_U, _C_Q, _C_K, _C_V, _C_QI, _C_KI, _C_WI, _C_GS, _C_GA = 0, 512, 1024, 1536, 2048, 2304, 2560, 2688, 3712
_C_END = 4736


def _rms(x, g):
    return x * lax.rsqrt(jnp.mean(x * x, axis=-1, keepdims=True) + EPS) * g


def _dot(a, b):
    return jnp.dot(a, b, preferred_element_type=F32)


def _dot_nt(a, b):
    return lax.dot_general(a, b, (((1,), (1,)), ((), ())), preferred_element_type=F32)


def _inproj_kernel(x_ref, g_ref, w_ref, bg_ref, cq_ref, sq_ref, ci_ref, si_ref,
                   u_ref, q_ref, k_ref, v_ref, qi_ref, ki_ref, wi_ref, gs_ref, ga_ref):
    h = _rms(x_ref[...], g_ref[...]).astype(BF16)

    def mm(c0, n):
        return _dot(h, w_ref[:, c0:c0 + n])

    u = mm(_C_U, 512)
    for lb in range(SSM_NLB):
        u_ref[lb] = u[:, lb * LANES:(lb + 1) * LANES]

    def rope(z, cos, sin, d):
        n = z.shape[1]
        lane = lax.broadcasted_iota(I32, z.shape, 1)
        partner = jnp.where((lane & (d - 1)) < d // 2, pltpu.roll(z, n - d // 2, 1), pltpu.roll(z, d // 2, 1))
        return z * cos + partner * sin

    cq = cq_ref[...]
    sq = sq_ref[...]
    q_ref[...] = (rope(mm(_C_Q, 512), cq, sq, HEAD_DIM) * (HEAD_DIM ** -0.5 * LOG2E)).astype(BF16)
    k_ref[...] = rope(mm(_C_K, 512), cq, sq, HEAD_DIM).astype(BF16)
    v_ref[...] = mm(_C_V, 512).astype(BF16)
    ci = ci_ref[...]
    si = si_ref[...]
    qi_ref[...] = rope(mm(_C_QI, 256), ci, si, IDX_DIM).astype(BF16)
    ki_ref[...] = rope(mm(_C_KI, 256), ci, si, IDX_DIM).astype(BF16)
    wi_ref[...] = mm(_C_WI, 128) * ((IDX_HEADS * IDX_DIM) ** -0.5)
    bg = bg_ref[...]
    gs_ref[...] = jax.nn.sigmoid(mm(_C_GS, 1024) + bg[:, :1024]).astype(BF16)
    ga_ref[...] = jax.nn.sigmoid(mm(_C_GA, 1024) + bg[:, 1024:]).astype(BF16)


def _rope_tables(seq, d, reps):
    half = d // 2
    inv = ROPE_THETA ** (-jnp.arange(half, dtype=F32) * 2.0 / d)
    ang = jnp.arange(seq, dtype=F32)[:, None] * inv[None, :]
    cos, sin = jnp.cos(ang), jnp.sin(ang)
    c = jnp.concatenate([cos, cos], axis=-1)
    s = jnp.concatenate([-sin, sin], axis=-1)
    return jnp.tile(c, (1, reps)), jnp.tile(s, (1, reps))


def _inproj(x2, norm_mix, w_in, b_gates, seq, tm):
    t = x2.shape[0]
    w_u, w_q, w_k, w_v, w_qi, w_ki, w_wi, w_gs, w_ga = jnp.split(
        w_in, [512, 1024, 1536, 2048, 2304, 2336, 2344, 3368], axis=1)
    w_ki8 = jnp.tile(w_ki, (1, IDX_HEADS))
    w_wi_p = jnp.pad(w_wi, ((0, 0), (0, LANES - IDX_HEADS)))
    w_all = jnp.concatenate([w_u, w_q, w_k, w_v, w_qi, w_ki8, w_wi_p, w_gs, w_ga], axis=1).astype(BF16)
    assert w_all.shape[1] == _C_END
    cq, sq = _rope_tables(seq, HEAD_DIM, N_HEADS)
    ci, si = _rope_tables(seq, IDX_DIM, IDX_HEADS)
    nt = seq // tm
    row = lambda i: (i, 0)
    pos = lambda i: (i % nt, 0)
    const = lambda i: (0, 0)
    outs = pl.pallas_call(
        _inproj_kernel,
        grid=(t // tm,),
        in_specs=[
            pl.BlockSpec((tm, D_MODEL), row),
            pl.BlockSpec((1, D_MODEL), const),
            pl.BlockSpec((D_MODEL, _C_END), const),
            pl.BlockSpec((1, 2 * D_MODEL), const),
            pl.BlockSpec((tm, 512), pos), pl.BlockSpec((tm, 512), pos),
            pl.BlockSpec((tm, 256), pos), pl.BlockSpec((tm, 256), pos),
        ],
        out_specs=[
            pl.BlockSpec((SSM_NLB, tm, LANES), lambda i: (0, i, 0)),
            pl.BlockSpec((tm, 512), row), pl.BlockSpec((tm, 512), row), pl.BlockSpec((tm, 512), row),
            pl.BlockSpec((tm, 256), row), pl.BlockSpec((tm, 256), row), pl.BlockSpec((tm, LANES), row),
            pl.BlockSpec((tm, D_MODEL), row), pl.BlockSpec((tm, D_MODEL), row),
        ],
        out_shape=[
            jax.ShapeDtypeStruct((SSM_NLB, t, LANES), F32),
            jax.ShapeDtypeStruct((t, 512), BF16), jax.ShapeDtypeStruct((t, 512), BF16),
            jax.ShapeDtypeStruct((t, 512), BF16),
            jax.ShapeDtypeStruct((t, 256), BF16), jax.ShapeDtypeStruct((t, 256), BF16),
            jax.ShapeDtypeStruct((t, LANES), F32),
            jax.ShapeDtypeStruct((t, D_MODEL), BF16), jax.ShapeDtypeStruct((t, D_MODEL), BF16),
        ],
        compiler_params=pltpu.CompilerParams(dimension_semantics=("arbitrary",),
                                             vmem_limit_bytes=VMEM_LIMIT),
        name="inproj",
    )(x2, norm_mix.reshape(1, D_MODEL), w_all, b_gates.reshape(1, 2 * D_MODEL), cq, sq, ci, si)
    return outs


def _ssm_mats(lam_re, lam_im, log_dt, b_re, b_im, c_re, c_im, d_skip):
    g_, p_, h_, tc, gl, nlb = SSM_GROUPS, SSM_STATE, SSM_GROUP, SSM_TC, SSM_GL, SSM_NLB
    lam = lax.complex(lam_re.astype(F32), lam_im.astype(F32))
    dt = jnp.exp(log_dt.astype(F32))[:, None]
    lam_dt = lam * dt
    lam_bar = jnp.exp(lam_dt)
    b_bar = ((lam_bar - 1.0) / lam)[..., None] * lax.complex(b_re.astype(F32), b_im.astype(F32))
    c = lax.complex(c_re.astype(F32), c_im.astype(F32))
    steps = jnp.arange(tc + 1, dtype=F32)
    pw = jnp.exp(lam_dt[None] * steps[:, None, None])
    hp = lax.Precision.HIGHEST
    lane_g = jnp.arange(LANES) // h_
    st_g = (jnp.arange(2 * gl * p_) % (gl * p_)) // p_
    rep_o = (jnp.arange(h_)[:, None] == (jnp.arange(LANES) % h_)[None, :]).astype(F32)
    st_col = (jnp.arange(2 * gl * p_) // (gl * p_)) * p_ + jnp.arange(2 * gl * p_) % p_
    rep_s = (jnp.arange(2 * p_)[:, None] == st_col[None, :]).astype(F32)
    taps = jnp.einsum('gop,tgp,gpi->gtoi', c, pw[:tc], b_bar).real
    a3 = taps.reshape(nlb, gl, tc, h_, h_).transpose(0, 2, 1, 4, 3).reshape(nlb, tc, LANES, h_)
    d = jnp.einsum('btro,oc->btrc', a3, rep_o, precision=hp)
    d = d * (lane_g[:, None] == lane_g[None, :]).astype(F32)
    dcat = d.transpose(0, 2, 1, 3).reshape(nlb, LANES, tc * LANES)
    m_intra = jnp.stack(
        [jnp.pad(dcat[:, :, :(tc - j) * LANES], ((0, 0), (0, 0), (j * LANES, 0))) for j in range(tc)],
        axis=1).reshape(nlb, tc * LANES, tc * LANES)
    sc = pw[:tc][::-1][:, :, :, None] * b_bar[None]
    sc = jnp.stack([sc.real, sc.imag], axis=0).reshape(2, tc, nlb, gl, p_, h_)
    a_s = sc.transpose(2, 1, 3, 5, 0, 4).reshape(nlb, tc, LANES, 2 * p_)
    m_state = jnp.einsum('bjrq,qc->bjrc', a_s, rep_s, precision=hp)
    m_state = (m_state * (lane_g[:, None] == st_g[None, :]).astype(F32)).reshape(nlb, tc * LANES, 2 * gl * p_)
    oc = c[None] * pw[1:tc + 1][:, :, None, :]
    oc = jnp.stack([oc.real, -oc.imag], axis=0).reshape(2, tc, nlb, gl, h_, p_)
    a_o = oc.transpose(2, 1, 0, 3, 5, 4).reshape(nlb, tc, 2 * gl * p_, h_)
    m_out = jnp.einsum('blro,oc->blrc', a_o, rep_o, precision=hp)
    m_out = m_out * (st_g[:, None] == lane_g[None, :]).astype(F32)
    m_out = m_out.transpose(0, 2, 1, 3).reshape(nlb, 2 * gl * p_, tc * LANES)
    a = pw[tc].reshape(nlb, 1, gl * p_)
    d_l = jnp.tile(d_skip.astype(F32).reshape(nlb, 1, LANES), (1, 1, tc))
    return (m_intra.astype(BF16), m_state.astype(BF16), m_out.astype(BF16),
            a.real.astype(F32), a.imag.astype(F32), d_l)


def _ssm_kernel(u_ref, mi_ref, ms_ref, mo_ref, are_ref, aim_ref, d_ref, y_ref, con_ref, sp_ref):
    nch = u_ref.shape[0] // SSM_TC
    ns = are_ref.shape[1]
    uf = jnp.concatenate([u_ref[pl.ds(j, nch, stride=SSM_TC), :] for j in range(SSM_TC)], axis=1)
    u = uf.astype(BF16)
    con_ref[...] = _dot(u, ms_ref[...])
    a_re = are_ref[...]
    a_im = aim_ref[...]

    def step(c, carry):
        s_re, s_im = carry
        sp_ref[pl.ds(c, 1), :ns] = s_re
        sp_ref[pl.ds(c, 1), ns:] = s_im
        c_re = con_ref[pl.ds(c, 1), :ns]
        c_im = con_ref[pl.ds(c, 1), ns:]
        return (a_re * s_re - a_im * s_im + c_re, a_re * s_im + a_im * s_re + c_im)

    zero = jnp.zeros((1, ns), F32)
    lax.fori_loop(0, nch, step, (zero, zero))
    y = _dot(u, mi_ref[...]) + _dot(sp_ref[...].astype(BF16), mo_ref[...]) + d_ref[...] * uf
    for l in range(SSM_TC):
        y_ref[pl.ds(l, nch, stride=SSM_TC), :] = y[:, l * LANES:(l + 1) * LANES]


def _ssm(u4, mats, bsz, seq):
    m_intra, m_state, m_out, a_re, a_im, d_l = mats
    t = u4.shape[1]
    nch = seq // SSM_TC
    wc = SSM_TC * LANES
    ns = SSM_GL * SSM_STATE
    wmap = lambda lb, b: (lb, 0, 0)
    return pl.pallas_call(
        _ssm_kernel,
        grid=(SSM_NLB, bsz),
        in_specs=[
            pl.BlockSpec((None, seq, LANES), lambda lb, b: (lb, b, 0)),
            pl.BlockSpec((None, wc, wc), wmap),
            pl.BlockSpec((None, wc, 2 * ns), wmap),
            pl.BlockSpec((None, 2 * ns, wc), wmap),
            pl.BlockSpec((None, 1, ns), wmap), pl.BlockSpec((None, 1, ns), wmap),
            pl.BlockSpec((None, 1, wc), wmap),
        ],
        out_specs=pl.BlockSpec((None, seq, LANES), lambda lb, b: (lb, b, 0)),
        out_shape=jax.ShapeDtypeStruct((SSM_NLB, t, LANES), F32),
        scratch_shapes=[pltpu.VMEM((nch, 2 * ns), F32), pltpu.VMEM((nch, 2 * ns), F32)],
        compiler_params=pltpu.CompilerParams(dimension_semantics=("arbitrary", "arbitrary"),
                                             vmem_limit_bytes=VMEM_LIMIT),
        name="ssm",
    )(u4, m_intra, m_state, m_out, a_re, a_im, d_l)


def _attn_kernel(q_ref, k_ref, v_ref, qi_ref, ki_ref, wi_ref, tri_ref, o_ref, keys_ref, kt_ref, k16_ref,
                 qh_ref, qm_ref, acc_ref, m_ref, *, tq, tk, seq, topk):
    i = pl.program_id(1)
    s0 = i * tq
    nk = (s0 + tq + tk - 1) // tk
    row = s0 + lax.broadcasted_iota(I32, (tq, 1), 0)
    lim = ((row >> CHUNK_SHIFT) + 1) << CHUNK_SHIFT
    lim_q = (((s0 + lax.broadcasted_iota(I32, (1, tq), 1)) >> CHUNK_SHIFT) + 1) << CHUNK_SHIFT
    kvec = jnp.minimum(lim_q, topk).astype(F32)
    col0 = lax.broadcasted_iota(I32, (tq, tk), 1)

    qi = qi_ref[...]
    lane_i = lax.broadcasted_iota(I32, qi.shape, 1)
    for h in range(IDX_HEADS):
        qh_ref[h] = jnp.where((lane_i >> 5) == h, qi, jnp.zeros_like(qi))
    wi = wi_ref[...]
    wcol = [wi[:, h:h + 1] for h in range(IDX_HEADS)]

    def score_tile(j, carry):
        kt = ki_ref[pl.ds(pl.multiple_of(j * tk, tk), tk), :]
        acc = jnp.zeros((tq, tk), F32)
        for h in range(IDX_HEADS):
            acc = acc + wcol[h] * jnp.maximum(_dot_nt(qh_ref[h], kt), 0.0)
        sc = jnp.where(col0 + j * tk < lim, acc, -jnp.inf)
        bits = pltpu.bitcast(sc, I32)
        key = bits ^ ((bits >> 31) & 0x7FFFFFFF)
        keys_ref[j] = key
        key_t = key.T
        kt_ref[j] = key_t
        k16_ref[j] = (key_t >> 16).astype(jnp.int16)
        return carry

    lax.fori_loop(0, nk, score_tile, 0)

    rg = 16
    one16 = jnp.ones((rg, tq), jnp.int16)
    zero16 = jnp.zeros((rg, tq), jnp.int16)

    def count16(cand):
        cand16 = jnp.broadcast_to(cand, (rg, tq)).astype(jnp.int16)

        def body(j, part):
            kk = k16_ref[j]
            for r in range(tk // rg):
                part = part + jnp.where(kk[r * rg:(r + 1) * rg, :] >= cand16, one16, zero16)
            return part
        part = lax.fori_loop(0, nk, body, jnp.zeros((rg, tq), jnp.int16))
        return jnp.sum(part.astype(F32), axis=0, keepdims=True)

    def bit_step(it, carry):
        u_ans, g_lo, g_hi = carry
        cand_u = u_ans | lax.shift_left(jnp.int32(1), 15 - it)
        cnt = count16(cand_u - 32768)
        ok = cnt >= kvec
        return jnp.where(ok, cand_u, u_ans), jnp.where(ok, cnt, g_lo), jnp.where(ok, g_hi, cnt)

    u_ans, g_lo, g_hi = lax.fori_loop(
        0, 16, bit_step, (jnp.zeros((1, tq), I32), lim_q.astype(F32), jnp.zeros((1, tq), F32)))
    b16 = u_ans - 32768

    def rebase(j, carry):
        kk = kt_ref[j]
        hi16 = kk >> 16
        low = (kk & 0xFFFF) - 32768
        rel = jnp.where(hi16 > b16, 32767, jnp.where(hi16 < b16, -32768, low))
        k16_ref[j] = rel.astype(jnp.int16)
        return carry

    lax.fori_loop(0, nk, rebase, 0)

    def unfinished(lo, hi, g_lo):
        return jnp.where(g_lo != kvec, jnp.where(hi - lo > 1, 1.0, 0.0), 0.0)

    def refine_cond(carry):
        return (carry[0] < 18) & (carry[1] > 0.0)

    def refine_pass(lo, hi, g_lo, g_hi):
        todo = unfinished(lo, hi, g_lo)
        mid = lo + ((hi - lo) >> 1)
        cand = jnp.where(b16 == 0, jnp.where(hi - lo == 65536, 1, mid), mid)
        cnt = count16(cand - 32768)
        up = jnp.where(cnt >= kvec, todo, 0.0) > 0.0
        dn = jnp.where(cnt < kvec, todo, 0.0) > 0.0
        return (jnp.where(up, cand, lo), jnp.where(dn, cand, hi), jnp.where(up, cnt, g_lo),
                jnp.where(dn, cnt, g_hi))

    def refine(carry):
        it, _, lo, hi, g_lo, g_hi = carry
        lo, hi, g_lo, g_hi = refine_pass(lo, hi, g_lo, g_hi)
        lo, hi, g_lo, g_hi = refine_pass(lo, hi, g_lo, g_hi)
        return it + 2, jnp.max(unfinished(lo, hi, g_lo)), lo, hi, g_lo, g_hi

    off_lo = jnp.zeros((1, tq), I32)
    off_hi = jnp.full((1, tq), 65536, I32)
    _, _, off_lo, _, g_lo, g_hi = lax.while_loop(
        refine_cond, refine,
        (jnp.int32(0), jnp.max(unfinished(off_lo, off_hi, g_lo)), off_lo, off_hi, g_lo, g_hi))
    has_tie = jnp.max(g_lo - kvec) > 0.0

    def to_rows(x):
        return jnp.transpose(jnp.broadcast_to(x, (LANES, tq)))

    vth = to_rows((b16 << 16) + off_lo)

    @pl.when(jnp.logical_not(has_tie))
    def _():
        def to_bias(j, carry):
            kk = keys_ref[j]
            for c in range(tk // LANES):
                sl = slice(c * LANES, (c + 1) * LANES)
                keys_ref[j, :, sl] = pltpu.bitcast(jnp.where(kk[:, sl] >= vth, 0.0, NEG_BIG).astype(F32), I32)
            return carry
        lax.fori_loop(0, nk, to_bias, 0)

    @pl.when(has_tie)
    def _():
        need = to_rows(kvec - g_hi)

        def to_bias(j, seen):
            kk = keys_ref[j]
            eq = jnp.concatenate(
                [jnp.where(kk[:, c * LANES:(c + 1) * LANES] == vth, 1.0, 0.0) for c in range(tk // LANES)],
                axis=1)
            rank = _dot(eq.astype(BF16), tri_ref[...])
            for c in range(tk // LANES):
                sl = slice(c * LANES, (c + 1) * LANES)
                take = jnp.where(seen + rank[:, sl] <= need, eq[:, sl], 0.0)
                sel = jnp.where(kk[:, sl] > vth, 1.0, take)
                keys_ref[j, :, sl] = pltpu.bitcast(jnp.where(sel > 0.0, 0.0, NEG_BIG).astype(F32), I32)
            return seen + jnp.broadcast_to(rank[:, tk - 1:tk], (tq, LANES))
        lax.fori_loop(0, nk, to_bias, jnp.zeros((tq, LANES), F32))

    lane_q = lax.broadcasted_iota(I32, (tq, LANES), 1)
    lane_k = lax.broadcasted_iota(I32, (tk, LANES), 1)
    for h in range(N_HEADS):
        qp = q_ref[:, (h // 2) * LANES:(h // 2 + 1) * LANES]
        keep = (lane_q >= HEAD_DIM) if h % 2 else (lane_q < HEAD_DIM)
        qm_ref[h] = jnp.where(keep, qp, jnp.zeros_like(qp))
    acc_ref[...] = jnp.zeros_like(acc_ref)
    m_ref[...] = jnp.full(m_ref.shape, NEG_BIG, F32)
    ones_v = jnp.ones((tk, LANES), BF16)

    def attn_tile(j, carry):
        rows = pl.ds(pl.multiple_of(j * tk, tk), tk)
        bias = pltpu.bitcast(keys_ref[j], F32)
        for h in range(N_HEADS):
            cs = slice((h // 2) * LANES, (h // 2 + 1) * LANES)
            own = (lane_k >= HEAD_DIM) if h % 2 else (lane_k < HEAD_DIM)
            s = _dot_nt(qm_ref[h], k_ref[rows, cs]) + bias
            m_old = m_ref[h]
            m_new = jnp.maximum(m_old, jnp.max(s, axis=1, keepdims=True))
            p = jnp.concatenate(
                [jnp.exp2(s[:, c * LANES:(c + 1) * LANES] - m_new) for c in range(tk // LANES)],
                axis=1).astype(BF16)
            vh = jnp.where(own, v_ref[rows, cs], ones_v)
            acc_ref[h] = jnp.exp2(m_old - m_new) * acc_ref[h] + _dot(p, vh)
            m_ref[h] = m_new
        return carry

    lax.fori_loop(0, nk, attn_tile, 0)

    for hp in range(N_HEADS // 2):
        a0 = acc_ref[2 * hp]
        a1 = acc_ref[2 * hp + 1]
        out = jnp.where(lane_q < HEAD_DIM, a0 / pltpu.roll(a0, HEAD_DIM, 1), a1 / pltpu.roll(a1, HEAD_DIM, 1))
        o_ref[:, hp * LANES:(hp + 1) * LANES] = out.astype(o_ref.dtype)


def _attn(q, k, v, qi, ki8, wi, bsz, seq, tq, tk):
    t = q.shape[0]
    nq = seq // tq
    qrow = lambda b, i: (b * nq + i, 0)
    kv = lambda b, i: (b, 0)
    kern = functools.partial(_attn_kernel, tq=tq, tk=tk, seq=seq, topk=min(TOPK_MAX, seq // 4))
    tri = (jnp.arange(tk)[:, None] <= jnp.arange(tk)[None, :]).astype(BF16)
    return pl.pallas_call(
        kern,
        grid=(bsz, nq),
        in_specs=[
            pl.BlockSpec((tq, ATTN_WIDTH), qrow),
            pl.BlockSpec((seq, ATTN_WIDTH), kv), pl.BlockSpec((seq, ATTN_WIDTH), kv),
            pl.BlockSpec((tq, 256), qrow), pl.BlockSpec((seq, 256), kv),
            pl.BlockSpec((tq, LANES), qrow),
            pl.BlockSpec((tk, tk), lambda b, i: (0, 0)),
        ],
        out_specs=pl.BlockSpec((tq, ATTN_WIDTH), qrow),
        out_shape=jax.ShapeDtypeStruct((t, ATTN_WIDTH), BF16),
        scratch_shapes=[
            pltpu.VMEM((seq // tk, tq, tk), I32),
            pltpu.VMEM((seq // tk, tk, tq), I32),
            pltpu.VMEM((seq // tk, tk, tq), jnp.int16),
            pltpu.VMEM((IDX_HEADS, tq, 256), BF16),
            pltpu.VMEM((N_HEADS, tq, LANES), BF16),
            pltpu.VMEM((N_HEADS, tq, LANES), F32),
            pltpu.VMEM((N_HEADS, tq, LANES), F32),
        ],
        compiler_params=pltpu.CompilerParams(dimension_semantics=("arbitrary", "arbitrary"),
                                             vmem_limit_bytes=VMEM_LIMIT),
        name="attn",
    )(q, k, v, qi, ki8, wi, tri)


def _mix_kernel(y4_ref, ya_ref, gs_ref, ga_ref, x_ref, wglu_ref, bglu_ref, wsb_ref, wab_ref, wo_ref,
                nf_ref, wr_ref, br_ref, tri_ref, utri_ref, r1_ref, h2_ref, route_ref, lpos_ref, lpos_t_ref, cnt_ref):
    ys = jnp.concatenate([y4_ref[lb] for lb in range(SSM_NLB)], axis=1)
    ys = 0.5 * ys * (1.0 + jnp.tanh(math.sqrt(2.0 / math.pi) * (ys + 0.044715 * (ys * ys * ys))))
    ys = ys * jax.nn.sigmoid(_dot(ys.astype(BF16), wglu_ref[...]) + bglu_ref[...])
    br_s = _dot(ys.astype(BF16), wsb_ref[...])
    br_a = _dot(ya_ref[...], wab_ref[...])
    mix = gs_ref[...].astype(F32) * br_s + ga_ref[...].astype(F32) * br_a
    r1 = x_ref[...] + _dot(mix.astype(BF16), wo_ref[...])
    r1_ref[...] = r1
    h2 = _rms(r1, nf_ref[...])
    h2_ref[...] = h2.astype(BF16)
    hi = h2.astype(BF16)
    lo = (h2 - hi.astype(F32)).astype(BF16)
    hw = _dot(hi, wr_ref[...])
    logits = hw[:, :LANES] + (hw[:, LANES:] + _dot(lo, wr_ref[:, :LANES])) + br_ref[...]
    tm = logits.shape[0]
    lane = lax.broadcasted_iota(I32, (tm, LANES), 1).astype(F32)
    vals = logits
    tops, idxs = [], []
    for _ in range(TOP_K):
        m = jnp.max(vals, axis=1, keepdims=True)
        idx = jnp.min(jnp.where(vals == m, lane, float(LANES)), axis=1, keepdims=True)
        tops.append(m)
        idxs.append(idx)
        vals = jnp.where(lane == idx, -jnp.inf, vals)
    es = [jnp.exp(tv - tops[0]) for tv in tops]
    den = es[0] + es[1] + es[2] + es[3]
    route = jnp.zeros((tm, LANES), F32)
    for kk in range(TOP_K):
        w = es[kk] / den
        w_hi = w.astype(BF16).astype(F32)
        w_mid = (w - w_hi).astype(BF16).astype(F32)
        w_lo = (w - w_hi) - w_mid
        route = jnp.where(lane == float(kk), idxs[kk], route)
        route = jnp.where(lane == float(TOP_K + kk), w_hi, route)
        route = jnp.where(lane == float(2 * TOP_K + kk), w_mid, route)
        route = jnp.where(lane == float(3 * TOP_K + kk), w_lo, route)
    route_ref[...] = route.astype(BF16)
    mask = jnp.zeros((tm, LANES), F32)
    for kk in range(TOP_K):
        mask = jnp.where(lane == idxs[kk], 1.0, mask)
    incl = _dot(tri_ref[...], mask.astype(BF16))
    cnt = incl[tm - 1:tm, :]
    cnt_u = jnp.floor((cnt + (MOE_SEG - 1)) * (1.0 / MOE_SEG))
    loc = _dot(jnp.broadcast_to(cnt_u, (8, LANES)).astype(BF16), utri_ref[...])[:1, :] * float(MOE_SEG)
    where_to = loc + incl - mask
    lpos = jnp.full((tm, LANES), -1.0, F32)
    for kk in range(TOP_K):
        pos = jnp.sum(jnp.where(lane == idxs[kk], where_to, 0.0), axis=1, keepdims=True)
        lpos = jnp.where(lane == float(kk), pos, lpos)
    lpos_ref[...] = lpos.astype(I32)
    lpos_t_ref[...] = jnp.transpose(lpos)[:8, :].astype(I32)
    cnt_ref[...] = jnp.broadcast_to(cnt, (8, LANES)).astype(I32)


def _mix(y4, y_attn, gs, ga, x2, w_glu, b_glu, w_ssm_br, w_attn_br, w_o, norm_ffn, w_router, b_router, tm):
    t = x2.shape[0]
    wr = jnp.pad(w_router.astype(F32), ((0, 0), (0, LANES - N_EXPERTS)))
    wr_hi = wr.astype(BF16)
    wr_split = jnp.concatenate([wr_hi, (wr - wr_hi.astype(F32)).astype(BF16)], axis=1)
    br = jnp.concatenate([b_router.astype(F32), jnp.full((LANES - N_EXPERTS,), NEG_BIG, F32)]).reshape(1, LANES)
    assert tm == MOE_TT
    nt = t // tm
    tri = (jnp.arange(tm)[:, None] >= jnp.arange(tm)[None, :]).astype(BF16)
    utri = (jnp.arange(LANES)[:, None] < jnp.arange(LANES)[None, :]).astype(BF16)
    row = lambda i: (i, 0)
    const = lambda i: (0, 0)
    return pl.pallas_call(
        _mix_kernel,
        grid=(t // tm,),
        in_specs=[
            pl.BlockSpec((SSM_NLB, tm, LANES), lambda i: (0, i, 0)),
            pl.BlockSpec((tm, ATTN_WIDTH), row),
            pl.BlockSpec((tm, D_MODEL), row), pl.BlockSpec((tm, D_MODEL), row), pl.BlockSpec((tm, D_MODEL), row),
            pl.BlockSpec((SSM_WIDTH, SSM_WIDTH), const), pl.BlockSpec((1, SSM_WIDTH), const),
            pl.BlockSpec((SSM_WIDTH, D_MODEL), const), pl.BlockSpec((ATTN_WIDTH, D_MODEL), const),
            pl.BlockSpec((D_MODEL, D_MODEL), const), pl.BlockSpec((1, D_MODEL), const),
            pl.BlockSpec((D_MODEL, 2 * LANES), const),
            pl.BlockSpec((1, LANES), const),
            pl.BlockSpec((tm, tm), const), pl.BlockSpec((LANES, LANES), const),
        ],
        out_specs=[pl.BlockSpec((tm, D_MODEL), row), pl.BlockSpec((tm, D_MODEL), row),
                   pl.BlockSpec((tm, LANES), row), pl.BlockSpec((tm, LANES), row),
                   pl.BlockSpec((None, 8, tm), lambda i: (i, 0, 0)), pl.BlockSpec((None, 8, LANES), lambda i: (i, 0, 0))],
        out_shape=[jax.ShapeDtypeStruct((t, D_MODEL), F32), jax.ShapeDtypeStruct((t, D_MODEL), BF16),
                   jax.ShapeDtypeStruct((t, LANES), BF16), jax.ShapeDtypeStruct((t, LANES), I32),
                   jax.ShapeDtypeStruct((nt, 8, tm), I32), jax.ShapeDtypeStruct((nt, 8, LANES), I32)],
        compiler_params=pltpu.CompilerParams(dimension_semantics=("arbitrary",),
                                             vmem_limit_bytes=VMEM_LIMIT),
        name="mix",
    )(y4, y_attn, gs, ga, x2, w_glu.astype(BF16), b_glu.reshape(1, SSM_WIDTH).astype(F32),
      w_ssm_br.astype(BF16), w_attn_br.astype(BF16), w_o.astype(BF16), norm_ffn.reshape(1, D_MODEL),
      wr_split, br, tri, utri)


def _segment_copies(seg_ref, e, make_copy, fn):
    c = seg_ref[0, e]
    s = seg_ref[0, N_EXPERTS + e]
    d = seg_ref[0, 2 * N_EXPERTS + e]
    for size in _SEG_SIZES:
        units = size // MOE_SEG
        hit = (c & units) != 0

        @pl.when(hit)
        def _():
            fn(make_copy(pl.multiple_of(s * MOE_SEG, MOE_SEG), pl.multiple_of(d * MOE_SEG, MOE_SEG), size))
        s = s + jnp.where(hit, units, 0)
        d = d + jnp.where(hit, units, 0)


def _dispatch_kernel(zt_ref, seg_ref, seg_prev_ref, lpos_ref, h2_ref, wtok_ref, xs_ref, buf_ref, zbuf_ref,
                     sem, zsem):
    tmx = zbuf_ref.shape[0]
    i = pl.program_id(0)
    slot = i % 2

    @pl.when(i == 0)
    def _():
        zbuf_ref[...] = jnp.zeros_like(zbuf_ref)

        def zero_copy(z):
            return pltpu.make_async_copy(
                zbuf_ref, xs_ref.at[pl.ds(pl.multiple_of(zt_ref[z] * tmx, tmx), tmx)], zsem)

        for z in range(zt_ref.shape[0]):
            @pl.when(zt_ref[z] >= 0)
            def _():
                zero_copy(z).start()
        for z in range(zt_ref.shape[0]):
            @pl.when(zt_ref[z] >= 0)
            def _():
                zero_copy(z).wait()

    rowi = lax.broadcasted_iota(I32, (MOE_NL, MOE_TT), 0)
    sel = jnp.zeros((MOE_NL, MOE_TT), F32)
    for kk in range(TOP_K):
        sel = jnp.where(rowi == lpos_ref[kk:kk + 1, :], 1.0, sel)
    src = jnp.concatenate([h2_ref[...], wtok_ref[...]], axis=1)
    buf_ref[slot] = _dot(sel.astype(BF16), src)

    def copies(table_ref, which, fn):
        def make_copy(s, d, size):
            return pltpu.make_async_copy(buf_ref.at[which, pl.ds(s, size)], xs_ref.at[pl.ds(d, size)],
                                         sem.at[which])

        def body(e, carry):
            _segment_copies(table_ref, e, make_copy, fn)
            return carry
        lax.fori_loop(0, N_EXPERTS, body, 0)

    @pl.when(i > 0)
    def _():
        copies(seg_prev_ref, 1 - slot, lambda cp: cp.wait())

    copies(seg_ref, slot, lambda cp: cp.start())

    @pl.when(i == pl.num_programs(0) - 1)
    def _():
        copies(seg_ref, slot, lambda cp: cp.wait())


def _dispatch(zero_tiles, seg, lpos_t, h2, wtok, n_rows, tmx):
    t = h2.shape[0]
    grid_spec = pltpu.PrefetchScalarGridSpec(
        num_scalar_prefetch=1,
        grid=(t // MOE_TT,),
        in_specs=[
            pl.BlockSpec((None, 1, 3 * N_EXPERTS), lambda i, zt: (i, 0, 0), memory_space=pltpu.SMEM),
            pl.BlockSpec((None, 1, 3 * N_EXPERTS), lambda i, zt: (jnp.maximum(i - 1, 0), 0, 0),
                         memory_space=pltpu.SMEM),
            pl.BlockSpec((None, 8, MOE_TT), lambda i, zt: (i, 0, 0)),
            pl.BlockSpec((MOE_TT, D_MODEL), lambda i, zt: (i, 0)),
            pl.BlockSpec((MOE_TT, LANES), lambda i, zt: (i, 0)),
        ],
        out_specs=pl.BlockSpec(memory_space=pl.ANY),
        scratch_shapes=[pltpu.VMEM((2, MOE_NL, MOE_W), F32), pltpu.VMEM((tmx, MOE_W), F32),
                        pltpu.SemaphoreType.DMA((2,)), pltpu.SemaphoreType.DMA(())],
    )
    return pl.pallas_call(
        _dispatch_kernel,
        grid_spec=grid_spec,
        out_shape=jax.ShapeDtypeStruct((n_rows, MOE_W), F32),
        compiler_params=pltpu.CompilerParams(dimension_semantics=("arbitrary",), has_side_effects=True,
                                             vmem_limit_bytes=VMEM_LIMIT),
        name="dispatch",
    )(zero_tiles, seg, seg, lpos_t, h2, wtok)


def _expert_kernel(te_ref, tv_ref, x_ref, wg_ref, bg_ref, wu_ref, bu_ref, wd_ref, bd_ref, y_ref,
                   wgb_ref, wub_ref, wdb_ref):
    i = pl.program_id(0)
    e = te_ref[i]
    prev = te_ref[jnp.maximum(i - 1, 0)]

    @pl.when((i == 0) | (e != prev))
    def _():
        wgb_ref[...] = wg_ref[...].astype(BF16)
        wub_ref[...] = wu_ref[...].astype(BF16)
        wdb_ref[...] = wd_ref[...].astype(BF16)

    @pl.when(tv_ref[i] != 0)
    def _():
        x = x_ref[:, :D_MODEL].astype(BF16)
        rt = x_ref[:, D_MODEL:]
        ef = e.astype(F32)
        w_row = jnp.zeros((rt.shape[0], 1), F32)
        for kk in range(TOP_K):
            wk = (rt[:, TOP_K + kk:TOP_K + kk + 1] + rt[:, 2 * TOP_K + kk:2 * TOP_K + kk + 1]
                  + rt[:, 3 * TOP_K + kk:3 * TOP_K + kk + 1])
            w_row = w_row + jnp.where(rt[:, kk:kk + 1] == ef, wk, 0.0)
        g = jnp.minimum(_dot(x, wgb_ref[...]) + bg_ref[...], SWIGLU_LIMIT)
        u = jnp.clip(_dot(x, wub_ref[...]) + bu_ref[...], -SWIGLU_LIMIT, SWIGLU_LIMIT)
        a = (u + 1.0) * (g * jax.nn.sigmoid(SWIGLU_ALPHA * g))
        y_ref[...] = (_dot(a.astype(BF16), wdb_ref[...]) + bd_ref[...]) * w_row

    @pl.when(tv_ref[i] == 0)
    def _():
        y_ref[...] = jnp.zeros_like(y_ref)


def _experts(tile_expert, tile_valid, xs, w_gate, b_gate, w_up, b_up, w_down, b_down, tmx):
    n_rows = xs.shape[0]
    wmap = lambda i, te, tv: (te[i], 0, 0)
    row = lambda i, te, tv: (i, 0)
    d_ff = w_gate.shape[2]
    grid_spec = pltpu.PrefetchScalarGridSpec(
        num_scalar_prefetch=2,
        grid=(n_rows // tmx,),
        in_specs=[
            pl.BlockSpec((tmx, MOE_W), row),
            pl.BlockSpec((None, D_MODEL, d_ff), wmap), pl.BlockSpec((None, 1, d_ff), wmap),
            pl.BlockSpec((None, D_MODEL, d_ff), wmap), pl.BlockSpec((None, 1, d_ff), wmap),
            pl.BlockSpec((None, d_ff, D_MODEL), wmap), pl.BlockSpec((None, 1, D_MODEL), wmap),
        ],
        out_specs=pl.BlockSpec((tmx, D_MODEL), row),
        scratch_shapes=[pltpu.VMEM((D_MODEL, d_ff), BF16), pltpu.VMEM((D_MODEL, d_ff), BF16),
                        pltpu.VMEM((d_ff, D_MODEL), BF16)],
    )
    return pl.pallas_call(
        _expert_kernel,
        grid_spec=grid_spec,
        out_shape=jax.ShapeDtypeStruct((n_rows, D_MODEL), F32),
        compiler_params=pltpu.CompilerParams(dimension_semantics=("arbitrary",),
                                             vmem_limit_bytes=VMEM_LIMIT),
        name="experts",
    )(tile_expert, tile_valid, xs, w_gate, b_gate.reshape(N_EXPERTS, 1, d_ff), w_up,
      b_up.reshape(N_EXPERTS, 1, d_ff), w_down, b_down.reshape(N_EXPERTS, 1, D_MODEL))


def _combine_kernel(seg_ref, seg_next_ref, lpos_ref, r1_ref, p_ref, ys_ref, wpg_ref, wpp_ref, nfin_ref, o_ref,
                    buf_ref, sem):
    i = pl.program_id(0)
    slot = i % 2

    def copies(table_ref, which, fn):
        def make_copy(s, d, size):
            return pltpu.make_async_copy(ys_ref.at[pl.ds(d, size)], buf_ref.at[which, pl.ds(s, size)],
                                         sem.at[which])

        def body(e, carry):
            _segment_copies(table_ref, e, make_copy, fn)
            return carry
        lax.fori_loop(0, N_EXPERTS, body, 0)

    @pl.when(i == 0)
    def _():
        buf_ref[...] = jnp.zeros_like(buf_ref)
        copies(seg_ref, slot, lambda cp: cp.start())

    @pl.when(i + 1 < pl.num_programs(0))
    def _():
        copies(seg_next_ref, 1 - slot, lambda cp: cp.start())

    coli = lax.broadcasted_iota(I32, (MOE_TT, MOE_NL), 1)
    lpos = lpos_ref[...]
    pick = jnp.zeros((MOE_TT, MOE_NL), F32)
    for kk in range(TOP_K):
        pick = jnp.where(coli == lpos[:, kk:kk + 1], 1.0, pick)
    copies(seg_ref, slot, lambda cp: cp.wait())
    moe = _dot(pick.astype(BF16), buf_ref[slot].astype(BF16))
    r2 = r1_ref[...] + moe
    gate = jax.nn.sigmoid(_dot(r2.astype(BF16), wpg_ref[...]))
    r3 = r2 + gate * _dot(p_ref[...].astype(BF16), wpp_ref[...])
    o_ref[...] = _rms(r3, nfin_ref[...])


def _combine(seg, lpos, r1, p2, ys, w_ple_gate, w_ple_proj, norm_final):
    t = r1.shape[0]
    nt = t // MOE_TT
    row = lambda i: (i, 0)
    const = lambda i: (0, 0)
    return pl.pallas_call(
        _combine_kernel,
        grid=(nt,),
        in_specs=[
            pl.BlockSpec((None, 1, 3 * N_EXPERTS), lambda i: (i, 0, 0), memory_space=pltpu.SMEM),
            pl.BlockSpec((None, 1, 3 * N_EXPERTS), lambda i: (jnp.minimum(i + 1, nt - 1), 0, 0),
                         memory_space=pltpu.SMEM),
            pl.BlockSpec((MOE_TT, LANES), row),
            pl.BlockSpec((MOE_TT, D_MODEL), row),
            pl.BlockSpec((MOE_TT, PLE_DIM), row),
            pl.BlockSpec(memory_space=pl.ANY),
            pl.BlockSpec((D_MODEL, D_MODEL), const), pl.BlockSpec((PLE_DIM, D_MODEL), const),
            pl.BlockSpec((1, D_MODEL), const),
        ],
        out_specs=pl.BlockSpec((MOE_TT, D_MODEL), row),
        out_shape=jax.ShapeDtypeStruct((t, D_MODEL), F32),
        scratch_shapes=[pltpu.VMEM((2, MOE_NL, D_MODEL), F32), pltpu.SemaphoreType.DMA((2,))],
        compiler_params=pltpu.CompilerParams(dimension_semantics=("arbitrary",),
                                             vmem_limit_bytes=VMEM_LIMIT),
        name="combine",
    )(seg, seg, lpos, r1, p2, ys, w_ple_gate.astype(BF16), w_ple_proj.astype(BF16),
      norm_final.reshape(1, D_MODEL))


def _moe(h2, route, lpos_l, lpos_t, cnt8, r1, p2, w_gate, b_gate, w_up, b_up, w_down, b_down, w_ple_gate,
         w_ple_proj, norm_final, tmx):
    t = h2.shape[0]
    nt = t // MOE_TT
    tile_cnt = cnt8[:, 0, :N_EXPERTS]
    cnt_u = (tile_cnt + MOE_SEG - 1) // MOE_SEG
    loc_u = jnp.cumsum(cnt_u, axis=1) - cnt_u
    reg_u = jnp.sum(cnt_u, axis=0)
    upt = tmx // MOE_SEG
    pad_u = ((reg_u + upt - 1) // upt) * upt
    end_u = jnp.cumsum(pad_u)
    glob_u = (end_u - pad_u)[None, :] + jnp.cumsum(cnt_u, axis=0) - cnt_u
    seg = jnp.concatenate([cnt_u, loc_u, glob_u], axis=1).astype(I32).reshape(nt, 1, 3 * N_EXPERTS)
    n_tiles = (t * TOP_K + nt * N_EXPERTS * (MOE_SEG - 1)) // tmx + N_EXPERTS
    ends = end_u * MOE_SEG
    tile_start = jnp.arange(n_tiles, dtype=I32) * tmx
    tile_valid = (tile_start < ends[-1]).astype(I32)
    tile_expert = jnp.minimum(jnp.sum((tile_start[:, None] >= ends[None, :]).astype(I32), axis=1), N_EXPERTS - 1)
    last_e = jnp.max(jnp.where(tile_valid != 0, tile_expert, 0))
    tile_expert = jnp.where(tile_valid != 0, tile_expert, last_e)
    n_spare = n_tiles - (t * TOP_K) // tmx
    used = ends[-1] // tmx
    last_tile = jnp.where(pad_u > 0, ends // tmx - 1, -1)
    spare = used + jnp.arange(n_spare, dtype=I32)
    zero_tiles = jnp.concatenate([last_tile, jnp.where(spare < n_tiles, spare, -1)]).astype(I32)
    xs = _dispatch(zero_tiles, seg, lpos_t, h2, route, n_tiles * tmx, tmx)
    ys = _experts(tile_expert, tile_valid, xs, w_gate, b_gate, w_up, b_up, w_down, b_down, tmx)
    return _combine(seg, lpos_l, r1, p2, ys, w_ple_gate, w_ple_proj, norm_final)


def _tile_plan(seq):
    return {
        "inproj_rows": 256,
        "attn_q": 256,
        "attn_k": min(1024, seq),
        "mix_rows": MOE_TT,
        "expert_rows": 512,
    }


def kernel(x, p, w_in, b_gates, lam_re, lam_im, log_dt, b_re, b_im, c_re, c_im, d_skip, w_glu, b_glu,
           w_ssm_br, w_attn_br, w_o, norm_mix, norm_ffn, w_router, b_router, w_gate, b_gate, w_up, b_up,
           w_down, b_down, w_ple_gate, w_ple_proj, norm_final):
    bsz, seq, _ = x.shape
    t = bsz * seq
    tiles = _tile_plan(seq)
    x2 = x.reshape(t, D_MODEL)
    u4, q, k, v, qi, ki8, wi, gs, ga = _inproj(x2, norm_mix[0], w_in[0], b_gates[0], seq, tm=tiles["inproj_rows"])
    mats = _ssm_mats(lam_re[0], lam_im[0], log_dt[0], b_re[0], b_im[0], c_re[0], c_im[0], d_skip[0])
    y4 = _ssm(u4, mats, bsz, seq)
    y_attn = _attn(q, k, v, qi, ki8, wi, bsz, seq, tq=tiles["attn_q"], tk=tiles["attn_k"])
    r1, h2, route, lpos_l, lpos_t, cnt8 = _mix(y4, y_attn, gs, ga, x2, w_glu[0], b_glu[0], w_ssm_br[0], w_attn_br[0], w_o[0],
                         norm_ffn[0], w_router[0], b_router[0], tm=tiles["mix_rows"])
    out = _moe(h2, route, lpos_l, lpos_t, cnt8, r1, p[0].reshape(t, PLE_DIM), w_gate[0], b_gate[0], w_up[0], b_up[0], w_down[0],
               b_down[0], w_ple_gate[0], w_ple_proj[0], norm_final, tmx=tiles["expert_rows"])
    return out.reshape(bsz, seq, D_MODEL)
```

```python
import functools
import math

import jax
import jax.numpy as jnp
from jax import lax
from jax.experimental import pallas as pl
from jax.experimental.pallas import tpu as pltpu

F32 = jnp.float32
BF16 = jnp.bfloat16
I32 = jnp.int32

D_MODEL = 1024
CHUNK = 64
PLE_DIM = 256
EPS = 1e-6
SSM_WIDTH = 512
SSM_GROUP = 16
SSM_GROUPS = 32
SSM_STATE = 64
N_HEADS = 8
HEAD_DIM = 64
ATTN_WIDTH = 512
IDX_HEADS = 8
IDX_DIM = 32
TOPK_MAX = 256
ROPE_THETA = 10000.0
N_EXPERTS = 32
TOP_K = 4
SWIGLU_LIMIT = 7.0
SWIGLU_ALPHA = 1.702

LANES = 128
SSM_TC = 16
SSM_GL = LANES // SSM_GROUP
SSM_NLB = SSM_WIDTH // LANES
NEG_BIG = -1e30
LOG2E = 1.4426950408889634
CHUNK_SHIFT = CHUNK.bit_length() - 1
VMEM_LIMIT = 56 * 1024 * 1024
MOE_TT = 512
MOE_SEG = 8
MOE_NL = MOE_TT * TOP_K + N_EXPERTS * MOE_SEG
MOE_W = D_MODEL + LANES
_SEG_SIZES = tuple(MOE_SEG << b for b in range((MOE_TT // MOE_SEG).bit_length() - 1, -1, -1))

_C_U, _C_Q, _C_K, _C_V, _C_QI, _C_KI, _C_WI, _C_GS, _C_GA = 0, 512, 1024, 1536, 2048, 2304, 2560, 2688, 3712
_C_END = 4736


def _rms(x, g):
    return x * lax.rsqrt(jnp.mean(x * x, axis=-1, keepdims=True) + EPS) * g


def _dot(a, b):
    return jnp.dot(a, b, preferred_element_type=F32)


def _dot_nt(a, b):
    return lax.dot_general(a, b, (((1,), (1,)), ((), ())), preferred_element_type=F32)


def _inproj_kernel(x_ref, g_ref, w_ref, bg_ref, cq_ref, sq_ref, ci_ref, si_ref,
                   u_ref, q_ref, k_ref, v_ref, qi_ref, ki_ref, wi_ref, gs_ref, ga_ref):
    h = _rms(x_ref[...], g_ref[...]).astype(BF16)

    def mm(c0, n):
        return _dot(h, w_ref[:, c0:c0 + n])

    u = mm(_C_U, 512)
    for lb in range(SSM_NLB):
        u_ref[lb] = u[:, lb * LANES:(lb + 1) * LANES]

    def rope(z, cos, sin, d):
        n = z.shape[1]
        lane = lax.broadcasted_iota(I32, z.shape, 1)
        partner = jnp.where((lane & (d - 1)) < d // 2, pltpu.roll(z, n - d // 2, 1), pltpu.roll(z, d // 2, 1))
        return z * cos + partner * sin

    cq = cq_ref[...]
    sq = sq_ref[...]
    q_ref[...] = (rope(mm(_C_Q, 512), cq, sq, HEAD_DIM) * (HEAD_DIM ** -0.5 * LOG2E)).astype(BF16)
    k_ref[...] = rope(mm(_C_K, 512), cq, sq, HEAD_DIM).astype(BF16)
    v_ref[...] = mm(_C_V, 512).astype(BF16)
    ci = ci_ref[...]
    si = si_ref[...]
    qi_ref[...] = rope(mm(_C_QI, 256), ci, si, IDX_DIM).astype(BF16)
    ki_ref[...] = rope(mm(_C_KI, 256), ci, si, IDX_DIM).astype(BF16)
    wi_ref[...] = mm(_C_WI, 128) * ((IDX_HEADS * IDX_DIM) ** -0.5)
    bg = bg_ref[...]
    gs_ref[...] = jax.nn.sigmoid(mm(_C_GS, 1024) + bg[:, :1024]).astype(BF16)
    ga_ref[...] = jax.nn.sigmoid(mm(_C_GA, 1024) + bg[:, 1024:]).astype(BF16)


def _rope_tables(seq, d, reps):
    half = d // 2
    inv = ROPE_THETA ** (-jnp.arange(half, dtype=F32) * 2.0 / d)
    ang = jnp.arange(seq, dtype=F32)[:, None] * inv[None, :]
    cos, sin = jnp.cos(ang), jnp.sin(ang)
    c = jnp.concatenate([cos, cos], axis=-1)
    s = jnp.concatenate([-sin, sin], axis=-1)
    return jnp.tile(c, (1, reps)), jnp.tile(s, (1, reps))


def _inproj(x2, norm_mix, w_in, b_gates, seq, tm):
    t = x2.shape[0]
    w_u, w_q, w_k, w_v, w_qi, w_ki, w_wi, w_gs, w_ga = jnp.split(
        w_in, [512, 1024, 1536, 2048, 2304, 2336, 2344, 3368], axis=1)
    w_ki8 = jnp.tile(w_ki, (1, IDX_HEADS))
    w_wi_p = jnp.pad(w_wi, ((0, 0), (0, LANES - IDX_HEADS)))
    w_all = jnp.concatenate([w_u, w_q, w_k, w_v, w_qi, w_ki8, w_wi_p, w_gs, w_ga], axis=1).astype(BF16)
    assert w_all.shape[1] == _C_END
    cq, sq = _rope_tables(seq, HEAD_DIM, N_HEADS)
    ci, si = _rope_tables(seq, IDX_DIM, IDX_HEADS)
    nt = seq // tm
    row = lambda i: (i, 0)
    pos = lambda i: (i % nt, 0)
    const = lambda i: (0, 0)
    outs = pl.pallas_call(
        _inproj_kernel,
        grid=(t // tm,),
        in_specs=[
            pl.BlockSpec((tm, D_MODEL), row),
            pl.BlockSpec((1, D_MODEL), const),
            pl.BlockSpec((D_MODEL, _C_END), const),
            pl.BlockSpec((1, 2 * D_MODEL), const),
            pl.BlockSpec((tm, 512), pos), pl.BlockSpec((tm, 512), pos),
            pl.BlockSpec((tm, 256), pos), pl.BlockSpec((tm, 256), pos),
        ],
        out_specs=[
            pl.BlockSpec((SSM_NLB, tm, LANES), lambda i: (0, i, 0)),
            pl.BlockSpec((tm, 512), row), pl.BlockSpec((tm, 512), row), pl.BlockSpec((tm, 512), row),
            pl.BlockSpec((tm, 256), row), pl.BlockSpec((tm, 256), row), pl.BlockSpec((tm, LANES), row),
            pl.BlockSpec((tm, D_MODEL), row), pl.BlockSpec((tm, D_MODEL), row),
        ],
        out_shape=[
            jax.ShapeDtypeStruct((SSM_NLB, t, LANES), F32),
            jax.ShapeDtypeStruct((t, 512), BF16), jax.ShapeDtypeStruct((t, 512), BF16),
            jax.ShapeDtypeStruct((t, 512), BF16),
            jax.ShapeDtypeStruct((t, 256), BF16), jax.ShapeDtypeStruct((t, 256), BF16),
            jax.ShapeDtypeStruct((t, LANES), F32),
            jax.ShapeDtypeStruct((t, D_MODEL), BF16), jax.ShapeDtypeStruct((t, D_MODEL), BF16),
        ],
        compiler_params=pltpu.CompilerParams(dimension_semantics=("arbitrary",),
                                             vmem_limit_bytes=VMEM_LIMIT),
        name="inproj",
    )(x2, norm_mix.reshape(1, D_MODEL), w_all, b_gates.reshape(1, 2 * D_MODEL), cq, sq, ci, si)
    return outs


def _ssm_mats(lam_re, lam_im, log_dt, b_re, b_im, c_re, c_im, d_skip):
    g_, p_, h_, tc, gl, nlb = SSM_GROUPS, SSM_STATE, SSM_GROUP, SSM_TC, SSM_GL, SSM_NLB
    lam = lax.complex(lam_re.astype(F32), lam_im.astype(F32))
    dt = jnp.exp(log_dt.astype(F32))[:, None]
    lam_dt = lam * dt
    lam_bar = jnp.exp(lam_dt)
    b_bar = ((lam_bar - 1.0) / lam)[..., None] * lax.complex(b_re.astype(F32), b_im.astype(F32))
    c = lax.complex(c_re.astype(F32), c_im.astype(F32))
    steps = jnp.arange(tc + 1, dtype=F32)
    pw = jnp.exp(lam_dt[None] * steps[:, None, None])
    hp = lax.Precision.HIGHEST
    lane_g = jnp.arange(LANES) // h_
    st_g = (jnp.arange(2 * gl * p_) % (gl * p_)) // p_
    rep_o = (jnp.arange(h_)[:, None] == (jnp.arange(LANES) % h_)[None, :]).astype(F32)
    st_col = (jnp.arange(2 * gl * p_) // (gl * p_)) * p_ + jnp.arange(2 * gl * p_) % p_
    rep_s = (jnp.arange(2 * p_)[:, None] == st_col[None, :]).astype(F32)
    taps = jnp.einsum('gop,tgp,gpi->gtoi', c, pw[:tc], b_bar).real
    a3 = taps.reshape(nlb, gl, tc, h_, h_).transpose(0, 2, 1, 4, 3).reshape(nlb, tc, LANES, h_)
    d = jnp.einsum('btro,oc->btrc', a3, rep_o, precision=hp)
    d = d * (lane_g[:, None] == lane_g[None, :]).astype(F32)
    dcat = d.transpose(0, 2, 1, 3).reshape(nlb, LANES, tc * LANES)
    m_intra = jnp.stack(
        [jnp.pad(dcat[:, :, :(tc - j) * LANES], ((0, 0), (0, 0), (j * LANES, 0))) for j in range(tc)],
        axis=1).reshape(nlb, tc * LANES, tc * LANES)
    sc = pw[:tc][::-1][:, :, :, None] * b_bar[None]
    sc = jnp.stack([sc.real, sc.imag], axis=0).reshape(2, tc, nlb, gl, p_, h_)
    a_s = sc.transpose(2, 1, 3, 5, 0, 4).reshape(nlb, tc, LANES, 2 * p_)
    m_state = jnp.einsum('bjrq,qc->bjrc', a_s, rep_s, precision=hp)
    m_state = (m_state * (lane_g[:, None] == st_g[None, :]).astype(F32)).reshape(nlb, tc * LANES, 2 * gl * p_)
    oc = c[None] * pw[1:tc + 1][:, :, None, :]
    oc = jnp.stack([oc.real, -oc.imag], axis=0).reshape(2, tc, nlb, gl, h_, p_)
    a_o = oc.transpose(2, 1, 0, 3, 5, 4).reshape(nlb, tc, 2 * gl * p_, h_)
    m_out = jnp.einsum('blro,oc->blrc', a_o, rep_o, precision=hp)
    m_out = m_out * (st_g[:, None] == lane_g[None, :]).astype(F32)
    m_out = m_out.transpose(0, 2, 1, 3).reshape(nlb, 2 * gl * p_, tc * LANES)
    a = pw[tc].reshape(nlb, 1, gl * p_)
    d_l = jnp.tile(d_skip.astype(F32).reshape(nlb, 1, LANES), (1, 1, tc))
    return (m_intra.astype(BF16), m_state.astype(BF16), m_out.astype(BF16),
            a.real.astype(F32), a.imag.astype(F32), d_l)


def _ssm_kernel(u_ref, mi_ref, ms_ref, mo_ref, are_ref, aim_ref, d_ref, y_ref, con_ref, sp_ref):
    nch = u_ref.shape[0] // SSM_TC
    ns = are_ref.shape[1]
    uf = jnp.concatenate([u_ref[pl.ds(j, nch, stride=SSM_TC), :] for j in range(SSM_TC)], axis=1)
    u = uf.astype(BF16)
    con_ref[...] = _dot(u, ms_ref[...])
    a_re = are_ref[...]
    a_im = aim_ref[...]

    def step(c, carry):
        s_re, s_im = carry
        sp_ref[pl.ds(c, 1), :ns] = s_re
        sp_ref[pl.ds(c, 1), ns:] = s_im
        c_re = con_ref[pl.ds(c, 1), :ns]
        c_im = con_ref[pl.ds(c, 1), ns:]
        return (a_re * s_re - a_im * s_im + c_re, a_re * s_im + a_im * s_re + c_im)

    zero = jnp.zeros((1, ns), F32)
    lax.fori_loop(0, nch, step, (zero, zero))
    y = _dot(u, mi_ref[...]) + _dot(sp_ref[...].astype(BF16), mo_ref[...]) + d_ref[...] * uf
    for l in range(SSM_TC):
        y_ref[pl.ds(l, nch, stride=SSM_TC), :] = y[:, l * LANES:(l + 1) * LANES]


def _ssm(u4, mats, bsz, seq):
    m_intra, m_state, m_out, a_re, a_im, d_l = mats
    t = u4.shape[1]
    nch = seq // SSM_TC
    wc = SSM_TC * LANES
    ns = SSM_GL * SSM_STATE
    wmap = lambda lb, b: (lb, 0, 0)
    return pl.pallas_call(
        _ssm_kernel,
        grid=(SSM_NLB, bsz),
        in_specs=[
            pl.BlockSpec((None, seq, LANES), lambda lb, b: (lb, b, 0)),
            pl.BlockSpec((None, wc, wc), wmap),
            pl.BlockSpec((None, wc, 2 * ns), wmap),
            pl.BlockSpec((None, 2 * ns, wc), wmap),
            pl.BlockSpec((None, 1, ns), wmap), pl.BlockSpec((None, 1, ns), wmap),
            pl.BlockSpec((None, 1, wc), wmap),
        ],
        out_specs=pl.BlockSpec((None, seq, LANES), lambda lb, b: (lb, b, 0)),
        out_shape=jax.ShapeDtypeStruct((SSM_NLB, t, LANES), F32),
        scratch_shapes=[pltpu.VMEM((nch, 2 * ns), F32), pltpu.VMEM((nch, 2 * ns), F32)],
        compiler_params=pltpu.CompilerParams(dimension_semantics=("arbitrary", "arbitrary"),
                                             vmem_limit_bytes=VMEM_LIMIT),
        name="ssm",
    )(u4, m_intra, m_state, m_out, a_re, a_im, d_l)


def _attn_kernel(q_ref, k_ref, v_ref, qi_ref, ki_ref, wi_ref, tri_ref, o_ref, keys_ref, kt_ref, k16_ref,
                 qh_ref, qm_ref, acc_ref, m_ref, *, tq, tk, seq, topk):
    i = pl.program_id(1)
    s0 = i * tq
    nk = (s0 + tq + tk - 1) // tk
    row = s0 + lax.broadcasted_iota(I32, (tq, 1), 0)
    lim = ((row >> CHUNK_SHIFT) + 1) << CHUNK_SHIFT
    lim_q = (((s0 + lax.broadcasted_iota(I32, (1, tq), 1)) >> CHUNK_SHIFT) + 1) << CHUNK_SHIFT
    kvec = jnp.minimum(lim_q, topk).astype(F32)
    col0 = lax.broadcasted_iota(I32, (tq, tk), 1)

    qi = qi_ref[...]
    lane_i = lax.broadcasted_iota(I32, qi.shape, 1)
    for h in range(IDX_HEADS):
        qh_ref[h] = jnp.where((lane_i >> 5) == h, qi, jnp.zeros_like(qi))
    wi = wi_ref[...]
    wcol = [wi[:, h:h + 1] for h in range(IDX_HEADS)]

    def score_tile(j, carry):
        kt = ki_ref[pl.ds(pl.multiple_of(j * tk, tk), tk), :]
        acc = jnp.zeros((tq, tk), F32)
        dots = _dot_nt(qh_ref[...].reshape(IDX_HEADS * tq, 256), kt)
        for h in range(IDX_HEADS):
            acc = acc + wcol[h] * jnp.maximum(dots[h * tq:(h + 1) * tq], 0.0)
        sc = jnp.where(col0 + j * tk < lim, acc, -jnp.inf)
        bits = pltpu.bitcast(sc, I32)
        key = bits ^ ((bits >> 31) & 0x7FFFFFFF)
        keys_ref[j] = key
        key_t = key.T
        kt_ref[j] = key_t
        k16_ref[j] = (key_t >> 16).astype(jnp.int16)
        return carry

    lax.fori_loop(0, nk, score_tile, 0)

    rg = 16
    one16 = jnp.ones((rg, tq), jnp.int16)
    zero16 = jnp.zeros((rg, tq), jnp.int16)

    def count16(cand):
        cand16 = jnp.broadcast_to(cand, (rg, tq)).astype(jnp.int16)

        def body(j, part):
            kk = k16_ref[j]
            for r in range(tk // rg):
                part = part + jnp.where(kk[r * rg:(r + 1) * rg, :] >= cand16, one16, zero16)
            return part
        part = lax.fori_loop(0, nk, body, jnp.zeros((rg, tq), jnp.int16))
        return jnp.sum(part.astype(F32), axis=0, keepdims=True)

    def bit_step(it, carry):
        u_ans, g_lo, g_hi = carry
        cand_u = u_ans | lax.shift_left(jnp.int32(1), 15 - it)
        cnt = count16(cand_u - 32768)
        ok = cnt >= kvec
        return jnp.where(ok, cand_u, u_ans), jnp.where(ok, cnt, g_lo), jnp.where(ok, g_hi, cnt)

    u_ans, g_lo, g_hi = lax.fori_loop(
        0, 16, bit_step, (jnp.zeros((1, tq), I32), lim_q.astype(F32), jnp.zeros((1, tq), F32)))
    b16 = u_ans - 32768

    def rebase(j, carry):
        kk = kt_ref[j]
        hi16 = kk >> 16
        low = (kk & 0xFFFF) - 32768
        rel = jnp.where(hi16 > b16, 32767, jnp.where(hi16 < b16, -32768, low))
        k16_ref[j] = rel.astype(jnp.int16)
        return carry

    lax.fori_loop(0, nk, rebase, 0)

    def unfinished(lo, hi, g_lo):
        return jnp.where(g_lo != kvec, jnp.where(hi - lo > 1, 1.0, 0.0), 0.0)

    def refine_cond(carry):
        return (carry[0] < 18) & (carry[1] > 0.0)

    def refine_pass(lo, hi, g_lo, g_hi):
        todo = unfinished(lo, hi, g_lo)
        mid = lo + ((hi - lo) >> 1)
        cand = jnp.where(b16 == 0, jnp.where(hi - lo == 65536, 1, mid), mid)
        cnt = count16(cand - 32768)
        up = jnp.where(cnt >= kvec, todo, 0.0) > 0.0
        dn = jnp.where(cnt < kvec, todo, 0.0) > 0.0
        return (jnp.where(up, cand, lo), jnp.where(dn, cand, hi), jnp.where(up, cnt, g_lo),
                jnp.where(dn, cnt, g_hi))

    def refine(carry):
        it, _, lo, hi, g_lo, g_hi = carry
        lo, hi, g_lo, g_hi = refine_pass(lo, hi, g_lo, g_hi)
        lo, hi, g_lo, g_hi = refine_pass(lo, hi, g_lo, g_hi)
        return it + 2, jnp.max(unfinished(lo, hi, g_lo)), lo, hi, g_lo, g_hi

    off_lo = jnp.zeros((1, tq), I32)
    off_hi = jnp.full((1, tq), 65536, I32)
    _, _, off_lo, _, g_lo, g_hi = lax.while_loop(
        refine_cond, refine,
        (jnp.int32(0), jnp.max(unfinished(off_lo, off_hi, g_lo)), off_lo, off_hi, g_lo, g_hi))
    has_tie = jnp.max(g_lo - kvec) > 0.0

    def to_rows(x):
        return jnp.transpose(jnp.broadcast_to(x, (LANES, tq)))

    vth = to_rows((b16 << 16) + off_lo)

    @pl.when(jnp.logical_not(has_tie))
    def _():
        def to_bias(j, carry):
            kk = keys_ref[j]
            for c in range(tk // LANES):
                sl = slice(c * LANES, (c + 1) * LANES)
                keys_ref[j, :, sl] = pltpu.bitcast(jnp.where(kk[:, sl] >= vth, 0.0, NEG_BIG).astype(F32), I32)
            return carry
        lax.fori_loop(0, nk, to_bias, 0)

    @pl.when(has_tie)
    def _():
        need = to_rows(kvec - g_hi)

        def to_bias(j, seen):
            kk = keys_ref[j]
            eq = jnp.concatenate(
                [jnp.where(kk[:, c * LANES:(c + 1) * LANES] == vth, 1.0, 0.0) for c in range(tk // LANES)],
                axis=1)
            rank = _dot(eq.astype(BF16), tri_ref[...])
            for c in range(tk // LANES):
                sl = slice(c * LANES, (c + 1) * LANES)
                take = jnp.where(seen + rank[:, sl] <= need, eq[:, sl], 0.0)
                sel = jnp.where(kk[:, sl] > vth, 1.0, take)
                keys_ref[j, :, sl] = pltpu.bitcast(jnp.where(sel > 0.0, 0.0, NEG_BIG).astype(F32), I32)
            return seen + jnp.broadcast_to(rank[:, tk - 1:tk], (tq, LANES))
        lax.fori_loop(0, nk, to_bias, jnp.zeros((tq, LANES), F32))

    lane_q = lax.broadcasted_iota(I32, (tq, LANES), 1)
    lane_k = lax.broadcasted_iota(I32, (tk, LANES), 1)
    for h in range(N_HEADS):
        qp = q_ref[:, (h // 2) * LANES:(h // 2 + 1) * LANES]
        keep = (lane_q >= HEAD_DIM) if h % 2 else (lane_q < HEAD_DIM)
        qm_ref[h] = jnp.where(keep, qp, jnp.zeros_like(qp))
    acc_ref[...] = jnp.zeros_like(acc_ref)
    m_ref[...] = jnp.full(m_ref.shape, NEG_BIG, F32)
    ones_v = jnp.ones((tk, LANES), BF16)

    def attn_tile(j, carry):
        rows = pl.ds(pl.multiple_of(j * tk, tk), tk)
        bias = pltpu.bitcast(keys_ref[j], F32)
        for h in range(N_HEADS):
            cs = slice((h // 2) * LANES, (h // 2 + 1) * LANES)
            own = (lane_k >= HEAD_DIM) if h % 2 else (lane_k < HEAD_DIM)
            if h % 2 == 0:
                s_pair = _dot_nt(qm_ref[h:h + 2].reshape(2 * tq, LANES), k_ref[rows, cs])
            s = s_pair[(h % 2) * tq:(h % 2 + 1) * tq] + bias
            m_old = m_ref[h]
            m_new = jnp.maximum(m_old, jnp.max(s, axis=1, keepdims=True))
            p = jnp.concatenate(
                [jnp.exp2(s[:, c * LANES:(c + 1) * LANES] - m_new) for c in range(tk // LANES)],
                axis=1).astype(BF16)
            vh = jnp.where(own, v_ref[rows, cs], ones_v)
            acc_ref[h] = jnp.exp2(m_old - m_new) * acc_ref[h] + _dot(p, vh)
            m_ref[h] = m_new
        return carry

    lax.fori_loop(0, nk, attn_tile, 0)

    for hp in range(N_HEADS // 2):
        a0 = acc_ref[2 * hp]
        a1 = acc_ref[2 * hp + 1]
        out = jnp.where(lane_q < HEAD_DIM, a0 / pltpu.roll(a0, HEAD_DIM, 1), a1 / pltpu.roll(a1, HEAD_DIM, 1))
        o_ref[:, hp * LANES:(hp + 1) * LANES] = out.astype(o_ref.dtype)


def _attn(q, k, v, qi, ki8, wi, bsz, seq, tq, tk):
    t = q.shape[0]
    nq = seq // tq
    qrow = lambda b, i: (b * nq + i, 0)
    kv = lambda b, i: (b, 0)
    kern = functools.partial(_attn_kernel, tq=tq, tk=tk, seq=seq, topk=min(TOPK_MAX, seq // 4))
    tri = (jnp.arange(tk)[:, None] <= jnp.arange(tk)[None, :]).astype(BF16)
    return pl.pallas_call(
        kern,
        grid=(bsz, nq),
        in_specs=[
            pl.BlockSpec((tq, ATTN_WIDTH), qrow),
            pl.BlockSpec((seq, ATTN_WIDTH), kv), pl.BlockSpec((seq, ATTN_WIDTH), kv),
            pl.BlockSpec((tq, 256), qrow), pl.BlockSpec((seq, 256), kv),
            pl.BlockSpec((tq, LANES), qrow),
            pl.BlockSpec((tk, tk), lambda b, i: (0, 0)),
        ],
        out_specs=pl.BlockSpec((tq, ATTN_WIDTH), qrow),
        out_shape=jax.ShapeDtypeStruct((t, ATTN_WIDTH), BF16),
        scratch_shapes=[
            pltpu.VMEM((seq // tk, tq, tk), I32),
            pltpu.VMEM((seq // tk, tk, tq), I32),
            pltpu.VMEM((seq // tk, tk, tq), jnp.int16),
            pltpu.VMEM((IDX_HEADS, tq, 256), BF16),
            pltpu.VMEM((N_HEADS, tq, LANES), BF16),
            pltpu.VMEM((N_HEADS, tq, LANES), F32),
            pltpu.VMEM((N_HEADS, tq, LANES), F32),
        ],
        compiler_params=pltpu.CompilerParams(dimension_semantics=("arbitrary", "arbitrary"),
                                             vmem_limit_bytes=VMEM_LIMIT),
        name="attn",
    )(q, k, v, qi, ki8, wi, tri)


def _mix_kernel(y4_ref, ya_ref, gs_ref, ga_ref, x_ref, wglu_ref, bglu_ref, wsb_ref, wab_ref, wo_ref,
                nf_ref, wr_ref, br_ref, tri_ref, utri_ref, r1_ref, h2_ref, route_ref, lpos_ref, lpos_t_ref, cnt_ref):
    ys = jnp.concatenate([y4_ref[lb] for lb in range(SSM_NLB)], axis=1)
    ys = 0.5 * ys * (1.0 + jnp.tanh(math.sqrt(2.0 / math.pi) * (ys + 0.044715 * (ys * ys * ys))))
    ys = ys * jax.nn.sigmoid(_dot(ys.astype(BF16), wglu_ref[...]) + bglu_ref[...])
    br_s = _dot(ys.astype(BF16), wsb_ref[...])
    br_a = _dot(ya_ref[...], wab_ref[...])
    mix = gs_ref[...].astype(F32) * br_s + ga_ref[...].astype(F32) * br_a
    r1 = x_ref[...] + _dot(mix.astype(BF16), wo_ref[...])
    r1_ref[...] = r1
    h2 = _rms(r1, nf_ref[...])
    h2_ref[...] = h2.astype(BF16)
    hi = h2.astype(BF16)
    lo = (h2 - hi.astype(F32)).astype(BF16)
    hw = _dot(hi, wr_ref[...])
    logits = hw[:, :LANES] + (hw[:, LANES:] + _dot(lo, wr_ref[:, :LANES])) + br_ref[...]
    tm = logits.shape[0]
    lane = lax.broadcasted_iota(I32, (tm, LANES), 1).astype(F32)
    vals = logits
    tops, idxs = [], []
    for _ in range(TOP_K):
        m = jnp.max(vals, axis=1, keepdims=True)
        idx = jnp.min(jnp.where(vals == m, lane, float(LANES)), axis=1, keepdims=True)
        tops.append(m)
        idxs.append(idx)
        vals = jnp.where(lane == idx, -jnp.inf, vals)
    es = [jnp.exp(tv - tops[0]) for tv in tops]
    den = es[0] + es[1] + es[2] + es[3]
    route = jnp.zeros((tm, LANES), F32)
    for kk in range(TOP_K):
        w = es[kk] / den
        w_hi = w.astype(BF16).astype(F32)
        w_mid = (w - w_hi).astype(BF16).astype(F32)
        w_lo = (w - w_hi) - w_mid
        route = jnp.where(lane == float(kk), idxs[kk], route)
        route = jnp.where(lane == float(TOP_K + kk), w_hi, route)
        route = jnp.where(lane == float(2 * TOP_K + kk), w_mid, route)
        route = jnp.where(lane == float(3 * TOP_K + kk), w_lo, route)
    route_ref[...] = route.astype(BF16)
    mask = jnp.zeros((tm, LANES), F32)
    for kk in range(TOP_K):
        mask = jnp.where(lane == idxs[kk], 1.0, mask)
    incl = _dot(tri_ref[...], mask.astype(BF16))
    cnt = incl[tm - 1:tm, :]
    cnt_u = jnp.floor((cnt + (MOE_SEG - 1)) * (1.0 / MOE_SEG))
    loc = _dot(jnp.broadcast_to(cnt_u, (8, LANES)).astype(BF16), utri_ref[...])[:1, :] * float(MOE_SEG)
    where_to = loc + incl - mask
    lpos = jnp.full((tm, LANES), -1.0, F32)
    for kk in range(TOP_K):
        pos = jnp.sum(jnp.where(lane == idxs[kk], where_to, 0.0), axis=1, keepdims=True)
        lpos = jnp.where(lane == float(kk), pos, lpos)
    lpos_ref[...] = lpos.astype(I32)
    lpos_t_ref[...] = jnp.transpose(lpos)[:8, :].astype(I32)
    cnt_ref[...] = jnp.broadcast_to(cnt, (8, LANES)).astype(I32)


def _mix(y4, y_attn, gs, ga, x2, w_glu, b_glu, w_ssm_br, w_attn_br, w_o, norm_ffn, w_router, b_router, tm):
    t = x2.shape[0]
    wr = jnp.pad(w_router.astype(F32), ((0, 0), (0, LANES - N_EXPERTS)))
    wr_hi = wr.astype(BF16)
    wr_split = jnp.concatenate([wr_hi, (wr - wr_hi.astype(F32)).astype(BF16)], axis=1)
    br = jnp.concatenate([b_router.astype(F32), jnp.full((LANES - N_EXPERTS,), NEG_BIG, F32)]).reshape(1, LANES)
    assert tm == MOE_TT
    nt = t // tm
    tri = (jnp.arange(tm)[:, None] >= jnp.arange(tm)[None, :]).astype(BF16)
    utri = (jnp.arange(LANES)[:, None] < jnp.arange(LANES)[None, :]).astype(BF16)
    row = lambda i: (i, 0)
    const = lambda i: (0, 0)
    return pl.pallas_call(
        _mix_kernel,
        grid=(t // tm,),
        in_specs=[
            pl.BlockSpec((SSM_NLB, tm, LANES), lambda i: (0, i, 0)),
            pl.BlockSpec((tm, ATTN_WIDTH), row),
            pl.BlockSpec((tm, D_MODEL), row), pl.BlockSpec((tm, D_MODEL), row), pl.BlockSpec((tm, D_MODEL), row),
            pl.BlockSpec((SSM_WIDTH, SSM_WIDTH), const), pl.BlockSpec((1, SSM_WIDTH), const),
            pl.BlockSpec((SSM_WIDTH, D_MODEL), const), pl.BlockSpec((ATTN_WIDTH, D_MODEL), const),
            pl.BlockSpec((D_MODEL, D_MODEL), const), pl.BlockSpec((1, D_MODEL), const),
            pl.BlockSpec((D_MODEL, 2 * LANES), const),
            pl.BlockSpec((1, LANES), const),
            pl.BlockSpec((tm, tm), const), pl.BlockSpec((LANES, LANES), const),
        ],
        out_specs=[pl.BlockSpec((tm, D_MODEL), row), pl.BlockSpec((tm, D_MODEL), row),
                   pl.BlockSpec((tm, LANES), row), pl.BlockSpec((tm, LANES), row),
                   pl.BlockSpec((None, 8, tm), lambda i: (i, 0, 0)), pl.BlockSpec((None, 8, LANES), lambda i: (i, 0, 0))],
        out_shape=[jax.ShapeDtypeStruct((t, D_MODEL), F32), jax.ShapeDtypeStruct((t, D_MODEL), BF16),
                   jax.ShapeDtypeStruct((t, LANES), BF16), jax.ShapeDtypeStruct((t, LANES), I32),
                   jax.ShapeDtypeStruct((nt, 8, tm), I32), jax.ShapeDtypeStruct((nt, 8, LANES), I32)],
        compiler_params=pltpu.CompilerParams(dimension_semantics=("arbitrary",),
                                             vmem_limit_bytes=VMEM_LIMIT),
        name="mix",
    )(y4, y_attn, gs, ga, x2, w_glu.astype(BF16), b_glu.reshape(1, SSM_WIDTH).astype(F32),
      w_ssm_br.astype(BF16), w_attn_br.astype(BF16), w_o.astype(BF16), norm_ffn.reshape(1, D_MODEL),
      wr_split, br, tri, utri)


def _segment_copies(seg_ref, e, make_copy, fn):
    c = seg_ref[0, e]
    s = seg_ref[0, N_EXPERTS + e]
    d = seg_ref[0, 2 * N_EXPERTS + e]
    for size in _SEG_SIZES:
        units = size // MOE_SEG
        hit = (c & units) != 0

        @pl.when(hit)
        def _():
            fn(make_copy(pl.multiple_of(s * MOE_SEG, MOE_SEG), pl.multiple_of(d * MOE_SEG, MOE_SEG), size))
        s = s + jnp.where(hit, units, 0)
        d = d + jnp.where(hit, units, 0)


def _dispatch_kernel(zt_ref, seg_ref, seg_prev_ref, lpos_ref, h2_ref, wtok_ref, xs_ref, buf_ref, zbuf_ref,
                     sem, zsem):
    tmx = zbuf_ref.shape[0]
    i = pl.program_id(0)
    slot = i % 2

    @pl.when(i == 0)
    def _():
        zbuf_ref[...] = jnp.zeros_like(zbuf_ref)

        def zero_copy(z):
            return pltpu.make_async_copy(
                zbuf_ref, xs_ref.at[pl.ds(pl.multiple_of(zt_ref[z] * tmx, tmx), tmx)], zsem)

        for z in range(zt_ref.shape[0]):
            @pl.when(zt_ref[z] >= 0)
            def _():
                zero_copy(z).start()
        for z in range(zt_ref.shape[0]):
            @pl.when(zt_ref[z] >= 0)
            def _():
                zero_copy(z).wait()

    rowi = lax.broadcasted_iota(I32, (MOE_NL, MOE_TT), 0)
    sel = jnp.zeros((MOE_NL, MOE_TT), F32)
    for kk in range(TOP_K):
        sel = jnp.where(rowi == lpos_ref[kk:kk + 1, :], 1.0, sel)
    src = jnp.concatenate([h2_ref[...], wtok_ref[...]], axis=1)
    buf_ref[slot] = _dot(sel.astype(BF16), src)

    def copies(table_ref, which, fn):
        def make_copy(s, d, size):
            return pltpu.make_async_copy(buf_ref.at[which, pl.ds(s, size)], xs_ref.at[pl.ds(d, size)],
                                         sem.at[which])

        def body(e, carry):
            _segment_copies(table_ref, e, make_copy, fn)
            return carry
        lax.fori_loop(0, N_EXPERTS, body, 0)

    @pl.when(i > 0)
    def _():
        copies(seg_prev_ref, 1 - slot, lambda cp: cp.wait())

    copies(seg_ref, slot, lambda cp: cp.start())

    @pl.when(i == pl.num_programs(0) - 1)
    def _():
        copies(seg_ref, slot, lambda cp: cp.wait())


def _dispatch(zero_tiles, seg, lpos_t, h2, wtok, n_rows, tmx):
    t = h2.shape[0]
    grid_spec = pltpu.PrefetchScalarGridSpec(
        num_scalar_prefetch=1,
        grid=(t // MOE_TT,),
        in_specs=[
            pl.BlockSpec((None, 1, 3 * N_EXPERTS), lambda i, zt: (i, 0, 0), memory_space=pltpu.SMEM),
            pl.BlockSpec((None, 1, 3 * N_EXPERTS), lambda i, zt: (jnp.maximum(i - 1, 0), 0, 0),
                         memory_space=pltpu.SMEM),
            pl.BlockSpec((None, 8, MOE_TT), lambda i, zt: (i, 0, 0)),
            pl.BlockSpec((MOE_TT, D_MODEL), lambda i, zt: (i, 0)),
            pl.BlockSpec((MOE_TT, LANES), lambda i, zt: (i, 0)),
        ],
        out_specs=pl.BlockSpec(memory_space=pl.ANY),
        scratch_shapes=[pltpu.VMEM((2, MOE_NL, MOE_W), F32), pltpu.VMEM((tmx, MOE_W), F32),
                        pltpu.SemaphoreType.DMA((2,)), pltpu.SemaphoreType.DMA(())],
    )
    return pl.pallas_call(
        _dispatch_kernel,
        grid_spec=grid_spec,
        out_shape=jax.ShapeDtypeStruct((n_rows, MOE_W), F32),
        compiler_params=pltpu.CompilerParams(dimension_semantics=("arbitrary",), has_side_effects=True,
                                             vmem_limit_bytes=VMEM_LIMIT),
        name="dispatch",
    )(zero_tiles, seg, seg, lpos_t, h2, wtok)


def _expert_kernel(te_ref, tv_ref, x_ref, wg_ref, bg_ref, wu_ref, bu_ref, wd_ref, bd_ref, y_ref,
                   wgb_ref, wub_ref, wdb_ref):
    i = pl.program_id(0)
    e = te_ref[i]
    prev = te_ref[jnp.maximum(i - 1, 0)]

    @pl.when((i == 0) | (e != prev))
    def _():
        wgb_ref[...] = wg_ref[...].astype(BF16)
        wub_ref[...] = wu_ref[...].astype(BF16)
        wdb_ref[...] = wd_ref[...].astype(BF16)

    @pl.when(tv_ref[i] != 0)
    def _():
        x = x_ref[:, :D_MODEL].astype(BF16)
        rt = x_ref[:, D_MODEL:]
        ef = e.astype(F32)
        w_row = jnp.zeros((rt.shape[0], 1), F32)
        for kk in range(TOP_K):
            wk = (rt[:, TOP_K + kk:TOP_K + kk + 1] + rt[:, 2 * TOP_K + kk:2 * TOP_K + kk + 1]
                  + rt[:, 3 * TOP_K + kk:3 * TOP_K + kk + 1])
            w_row = w_row + jnp.where(rt[:, kk:kk + 1] == ef, wk, 0.0)
        g = jnp.minimum(_dot(x, wgb_ref[...]) + bg_ref[...], SWIGLU_LIMIT)
        u = jnp.clip(_dot(x, wub_ref[...]) + bu_ref[...], -SWIGLU_LIMIT, SWIGLU_LIMIT)
        a = (u + 1.0) * (g * jax.nn.sigmoid(SWIGLU_ALPHA * g))
        y_ref[...] = (_dot(a.astype(BF16), wdb_ref[...]) + bd_ref[...]) * w_row

    @pl.when(tv_ref[i] == 0)
    def _():
        y_ref[...] = jnp.zeros_like(y_ref)


def _experts(tile_expert, tile_valid, xs, w_gate, b_gate, w_up, b_up, w_down, b_down, tmx):
    n_rows = xs.shape[0]
    wmap = lambda i, te, tv: (te[i], 0, 0)
    row = lambda i, te, tv: (i, 0)
    d_ff = w_gate.shape[2]
    grid_spec = pltpu.PrefetchScalarGridSpec(
        num_scalar_prefetch=2,
        grid=(n_rows // tmx,),
        in_specs=[
            pl.BlockSpec((tmx, MOE_W), row),
            pl.BlockSpec((None, D_MODEL, d_ff), wmap), pl.BlockSpec((None, 1, d_ff), wmap),
            pl.BlockSpec((None, D_MODEL, d_ff), wmap), pl.BlockSpec((None, 1, d_ff), wmap),
            pl.BlockSpec((None, d_ff, D_MODEL), wmap), pl.BlockSpec((None, 1, D_MODEL), wmap),
        ],
        out_specs=pl.BlockSpec((tmx, D_MODEL), row),
        scratch_shapes=[pltpu.VMEM((D_MODEL, d_ff), BF16), pltpu.VMEM((D_MODEL, d_ff), BF16),
                        pltpu.VMEM((d_ff, D_MODEL), BF16)],
    )
    return pl.pallas_call(
        _expert_kernel,
        grid_spec=grid_spec,
        out_shape=jax.ShapeDtypeStruct((n_rows, D_MODEL), F32),
        compiler_params=pltpu.CompilerParams(dimension_semantics=("arbitrary",),
                                             vmem_limit_bytes=VMEM_LIMIT),
        name="experts",
    )(tile_expert, tile_valid, xs, w_gate, b_gate.reshape(N_EXPERTS, 1, d_ff), w_up,
      b_up.reshape(N_EXPERTS, 1, d_ff), w_down, b_down.reshape(N_EXPERTS, 1, D_MODEL))


def _combine_kernel(seg_ref, seg_next_ref, lpos_ref, r1_ref, p_ref, ys_ref, wpg_ref, wpp_ref, nfin_ref, o_ref,
                    buf_ref, sem):
    i = pl.program_id(0)
    slot = i % 2

    def copies(table_ref, which, fn):
        def make_copy(s, d, size):
            return pltpu.make_async_copy(ys_ref.at[pl.ds(d, size)], buf_ref.at[which, pl.ds(s, size)],
                                         sem.at[which])

        def body(e, carry):
            _segment_copies(table_ref, e, make_copy, fn)
            return carry
        lax.fori_loop(0, N_EXPERTS, body, 0)

    @pl.when(i == 0)
    def _():
        buf_ref[...] = jnp.zeros_like(buf_ref)
        copies(seg_ref, slot, lambda cp: cp.start())

    @pl.when(i + 1 < pl.num_programs(0))
    def _():
        copies(seg_next_ref, 1 - slot, lambda cp: cp.start())

    coli = lax.broadcasted_iota(I32, (MOE_TT, MOE_NL), 1)
    lpos = lpos_ref[...]
    pick = jnp.zeros((MOE_TT, MOE_NL), F32)
    for kk in range(TOP_K):
        pick = jnp.where(coli == lpos[:, kk:kk + 1], 1.0, pick)
    copies(seg_ref, slot, lambda cp: cp.wait())
    moe = _dot(pick.astype(BF16), buf_ref[slot].astype(BF16))
    r2 = r1_ref[...] + moe
    gate = jax.nn.sigmoid(_dot(r2.astype(BF16), wpg_ref[...]))
    r3 = r2 + gate * _dot(p_ref[...].astype(BF16), wpp_ref[...])
    o_ref[...] = _rms(r3, nfin_ref[...])


def _combine(seg, lpos, r1, p2, ys, w_ple_gate, w_ple_proj, norm_final):
    t = r1.shape[0]
    nt = t // MOE_TT
    row = lambda i: (i, 0)
    const = lambda i: (0, 0)
    return pl.pallas_call(
        _combine_kernel,
        grid=(nt,),
        in_specs=[
            pl.BlockSpec((None, 1, 3 * N_EXPERTS), lambda i: (i, 0, 0), memory_space=pltpu.SMEM),
            pl.BlockSpec((None, 1, 3 * N_EXPERTS), lambda i: (jnp.minimum(i + 1, nt - 1), 0, 0),
                         memory_space=pltpu.SMEM),
            pl.BlockSpec((MOE_TT, LANES), row),
            pl.BlockSpec((MOE_TT, D_MODEL), row),
            pl.BlockSpec((MOE_TT, PLE_DIM), row),
            pl.BlockSpec(memory_space=pl.ANY),
            pl.BlockSpec((D_MODEL, D_MODEL), const), pl.BlockSpec((PLE_DIM, D_MODEL), const),
            pl.BlockSpec((1, D_MODEL), const),
        ],
        out_specs=pl.BlockSpec((MOE_TT, D_MODEL), row),
        out_shape=jax.ShapeDtypeStruct((t, D_MODEL), F32),
        scratch_shapes=[pltpu.VMEM((2, MOE_NL, D_MODEL), F32), pltpu.SemaphoreType.DMA((2,))],
        compiler_params=pltpu.CompilerParams(dimension_semantics=("arbitrary",),
                                             vmem_limit_bytes=VMEM_LIMIT),
        name="combine",
    )(seg, seg, lpos, r1, p2, ys, w_ple_gate.astype(BF16), w_ple_proj.astype(BF16),
      norm_final.reshape(1, D_MODEL))


def _moe(h2, route, lpos_l, lpos_t, cnt8, r1, p2, w_gate, b_gate, w_up, b_up, w_down, b_down, w_ple_gate,
         w_ple_proj, norm_final, tmx):
    t = h2.shape[0]
    nt = t // MOE_TT
    tile_cnt = cnt8[:, 0, :N_EXPERTS]
    cnt_u = (tile_cnt + MOE_SEG - 1) // MOE_SEG
    loc_u = jnp.cumsum(cnt_u, axis=1) - cnt_u
    reg_u = jnp.sum(cnt_u, axis=0)
    upt = tmx // MOE_SEG
    pad_u = ((reg_u + upt - 1) // upt) * upt
    end_u = jnp.cumsum(pad_u)
    glob_u = (end_u - pad_u)[None, :] + jnp.cumsum(cnt_u, axis=0) - cnt_u
    seg = jnp.concatenate([cnt_u, loc_u, glob_u], axis=1).astype(I32).reshape(nt, 1, 3 * N_EXPERTS)
    n_tiles = (t * TOP_K + nt * N_EXPERTS * (MOE_SEG - 1)) // tmx + N_EXPERTS
    ends = end_u * MOE_SEG
    tile_start = jnp.arange(n_tiles, dtype=I32) * tmx
    tile_valid = (tile_start < ends[-1]).astype(I32)
    tile_expert = jnp.minimum(jnp.sum((tile_start[:, None] >= ends[None, :]).astype(I32), axis=1), N_EXPERTS - 1)
    last_e = jnp.max(jnp.where(tile_valid != 0, tile_expert, 0))
    tile_expert = jnp.where(tile_valid != 0, tile_expert, last_e)
    n_spare = n_tiles - (t * TOP_K) // tmx
    used = ends[-1] // tmx
    last_tile = jnp.where(pad_u > 0, ends // tmx - 1, -1)
    spare = used + jnp.arange(n_spare, dtype=I32)
    zero_tiles = jnp.concatenate([last_tile, jnp.where(spare < n_tiles, spare, -1)]).astype(I32)
    xs = _dispatch(zero_tiles, seg, lpos_t, h2, route, n_tiles * tmx, tmx)
    ys = _experts(tile_expert, tile_valid, xs, w_gate, b_gate, w_up, b_up, w_down, b_down, tmx)
    return _combine(seg, lpos_l, r1, p2, ys, w_ple_gate, w_ple_proj, norm_final)


def _tile_plan(seq):
    return {
        "inproj_rows": 256,
        "attn_q": 256,
        "attn_k": min(1024, seq),
        "mix_rows": MOE_TT,
        "expert_rows": 512,
    }


def kernel(x, p, w_in, b_gates, lam_re, lam_im, log_dt, b_re, b_im, c_re, c_im, d_skip, w_glu, b_glu,
           w_ssm_br, w_attn_br, w_o, norm_mix, norm_ffn, w_router, b_router, w_gate, b_gate, w_up, b_up,
           w_down, b_down, w_ple_gate, w_ple_proj, norm_final):
    bsz, seq, _ = x.shape
    t = bsz * seq
    tiles = _tile_plan(seq)
    x2 = x.reshape(t, D_MODEL)
    u4, q, k, v, qi, ki8, wi, gs, ga = _inproj(x2, norm_mix[0], w_in[0], b_gates[0], seq, tm=tiles["inproj_rows"])
    mats = _ssm_mats(lam_re[0], lam_im[0], log_dt[0], b_re[0], b_im[0], c_re[0], c_im[0], d_skip[0])
    y4 = _ssm(u4, mats, bsz, seq)
    y_attn = _attn(q, k, v, qi, ki8, wi, bsz, seq, tq=tiles["attn_q"], tk=tiles["attn_k"])
    r1, h2, route, lpos_l, lpos_t, cnt8 = _mix(y4, y_attn, gs, ga, x2, w_glu[0], b_glu[0], w_ssm_br[0], w_attn_br[0], w_o[0],
                         norm_ffn[0], w_router[0], b_router[0], tm=tiles["mix_rows"])
    out = _moe(h2, route, lpos_l, lpos_t, cnt8, r1, p[0].reshape(t, PLE_DIM), w_gate[0], b_gate[0], w_up[0], b_up[0], w_down[0],
               b_down[0], w_ple_gate[0], w_ple_proj[0], norm_final, tmx=tiles["expert_rows"])
    return out.reshape(bsz, seq, D_MODEL)
```

```python
import functools
import math

import jax
import jax.numpy as jnp
from jax import lax
from jax.experimental import pallas as pl
from jax.experimental.pallas import tpu as pltpu

F32 = jnp.float32
BF16 = jnp.bfloat16
I32 = jnp.int32

D_MODEL = 1024
CHUNK = 64
PLE_DIM = 256
EPS = 1e-6
SSM_WIDTH = 512
SSM_GROUP = 16
SSM_GROUPS = 32
SSM_STATE = 64
N_HEADS = 8
HEAD_DIM = 64
ATTN_WIDTH = 512
IDX_HEADS = 8
IDX_DIM = 32
TOPK_MAX = 256
ROPE_THETA = 10000.0
N_EXPERTS = 32
TOP_K = 4
SWIGLU_LIMIT = 7.0
SWIGLU_ALPHA = 1.702

LANES = 128
SSM_TC = 16
SSM_GL = LANES // SSM_GROUP
SSM_NLB = SSM_WIDTH // LANES
NEG_BIG = -1e30
LOG2E = 1.4426950408889634
CHUNK_SHIFT = CHUNK.bit_length() - 1
VMEM_LIMIT = 56 * 1024 * 1024
MOE_TT = 512
MOE_SEG = 8
MOE_NL = MOE_TT * TOP_K + N_EXPERTS * MOE_SEG
MOE_W = D_MODEL + LANES
_SEG_SIZES = tuple(MOE_SEG << b for b in range((MOE_TT // MOE_SEG).bit_length() - 1, -1, -1))

_C_U, _C_Q, _C_K, _C_V, _C_QI, _C_KI, _C_WI, _C_GS, _C_GA = 0, 512, 1024, 1536, 2048, 2304, 2560, 2688, 3712
_C_END = 4736


def _rms(x, g):
    return x * lax.rsqrt(jnp.mean(x * x, axis=-1, keepdims=True) + EPS) * g


def _dot(a, b):
    return jnp.dot(a, b, preferred_element_type=F32)


def _dot_nt(a, b):
    return lax.dot_general(a, b, (((1,), (1,)), ((), ())), preferred_element_type=F32)


def _inproj_kernel(x_ref, g_ref, w_ref, bg_ref, cq_ref, sq_ref, ci_ref, si_ref,
                   u_ref, q_ref, k_ref, v_ref, qi_ref, ki_ref, wi_ref, gs_ref, ga_ref):
    h = _rms(x_ref[...], g_ref[...]).astype(BF16)

    def mm(c0, n):
        return _dot(h, w_ref[:, c0:c0 + n])

    u = mm(_C_U, 512)
    for lb in range(SSM_NLB):
        u_ref[lb] = u[:, lb * LANES:(lb + 1) * LANES]

    def rope(z, cos, sin, d):
        n = z.shape[1]
        lane = lax.broadcasted_iota(I32, z.shape, 1)
        partner = jnp.where((lane & (d - 1)) < d // 2, pltpu.roll(z, n - d // 2, 1), pltpu.roll(z, d // 2, 1))
        return z * cos + partner * sin

    cq = cq_ref[...]
    sq = sq_ref[...]
    q_ref[...] = (rope(mm(_C_Q, 512), cq, sq, HEAD_DIM) * (HEAD_DIM ** -0.5 * LOG2E)).astype(BF16)
    k_ref[...] = rope(mm(_C_K, 512), cq, sq, HEAD_DIM).astype(BF16)
    v_ref[...] = mm(_C_V, 512).astype(BF16)
    ci = ci_ref[...]
    si = si_ref[...]
    qi_ref[...] = rope(mm(_C_QI, 256), ci, si, IDX_DIM).astype(BF16)
    ki_ref[...] = rope(mm(_C_KI, 256), ci, si, IDX_DIM).astype(BF16)
    wi_ref[...] = mm(_C_WI, 128) * ((IDX_HEADS * IDX_DIM) ** -0.5)
    bg = bg_ref[...]
    gs_ref[...] = jax.nn.sigmoid(mm(_C_GS, 1024) + bg[:, :1024]).astype(BF16)
    ga_ref[...] = jax.nn.sigmoid(mm(_C_GA, 1024) + bg[:, 1024:]).astype(BF16)


def _rope_tables(seq, d, reps):
    half = d // 2
    inv = ROPE_THETA ** (-jnp.arange(half, dtype=F32) * 2.0 / d)
    ang = jnp.arange(seq, dtype=F32)[:, None] * inv[None, :]
    cos, sin = jnp.cos(ang), jnp.sin(ang)
    c = jnp.concatenate([cos, cos], axis=-1)
    s = jnp.concatenate([-sin, sin], axis=-1)
    return jnp.tile(c, (1, reps)), jnp.tile(s, (1, reps))


def _inproj(x2, norm_mix, w_in, b_gates, seq, tm):
    t = x2.shape[0]
    w_u, w_q, w_k, w_v, w_qi, w_ki, w_wi, w_gs, w_ga = jnp.split(
        w_in, [512, 1024, 1536, 2048, 2304, 2336, 2344, 3368], axis=1)
    w_ki8 = jnp.tile(w_ki, (1, IDX_HEADS))
    w_wi_p = jnp.pad(w_wi, ((0, 0), (0, LANES - IDX_HEADS)))
    w_all = jnp.concatenate([w_u, w_q, w_k, w_v, w_qi, w_ki8, w_wi_p, w_gs, w_ga], axis=1).astype(BF16)
    assert w_all.shape[1] == _C_END
    cq, sq = _rope_tables(seq, HEAD_DIM, N_HEADS)
    ci, si = _rope_tables(seq, IDX_DIM, IDX_HEADS)
    nt = seq // tm
    row = lambda i: (i, 0)
    pos = lambda i: (i % nt, 0)
    const = lambda i: (0, 0)
    outs = pl.pallas_call(
        _inproj_kernel,
        grid=(t // tm,),
        in_specs=[
            pl.BlockSpec((tm, D_MODEL), row),
            pl.BlockSpec((1, D_MODEL), const),
            pl.BlockSpec((D_MODEL, _C_END), const),
            pl.BlockSpec((1, 2 * D_MODEL), const),
            pl.BlockSpec((tm, 512), pos), pl.BlockSpec((tm, 512), pos),
            pl.BlockSpec((tm, 256), pos), pl.BlockSpec((tm, 256), pos),
        ],
        out_specs=[
            pl.BlockSpec((SSM_NLB, tm, LANES), lambda i: (0, i, 0)),
            pl.BlockSpec((tm, 512), row), pl.BlockSpec((tm, 512), row), pl.BlockSpec((tm, 512), row),
            pl.BlockSpec((tm, 256), row), pl.BlockSpec((tm, 256), row), pl.BlockSpec((tm, LANES), row),
            pl.BlockSpec((tm, D_MODEL), row), pl.BlockSpec((tm, D_MODEL), row),
        ],
        out_shape=[
            jax.ShapeDtypeStruct((SSM_NLB, t, LANES), F32),
            jax.ShapeDtypeStruct((t, 512), BF16), jax.ShapeDtypeStruct((t, 512), BF16),
            jax.ShapeDtypeStruct((t, 512), BF16),
            jax.ShapeDtypeStruct((t, 256), BF16), jax.ShapeDtypeStruct((t, 256), BF16),
            jax.ShapeDtypeStruct((t, LANES), F32),
            jax.ShapeDtypeStruct((t, D_MODEL), BF16), jax.ShapeDtypeStruct((t, D_MODEL), BF16),
        ],
        compiler_params=pltpu.CompilerParams(dimension_semantics=("arbitrary",),
                                             vmem_limit_bytes=VMEM_LIMIT),
        name="inproj",
    )(x2, norm_mix.reshape(1, D_MODEL), w_all, b_gates.reshape(1, 2 * D_MODEL), cq, sq, ci, si)
    return outs


def _ssm_mats(lam_re, lam_im, log_dt, b_re, b_im, c_re, c_im, d_skip):
    g_, p_, h_, tc, gl, nlb = SSM_GROUPS, SSM_STATE, SSM_GROUP, SSM_TC, SSM_GL, SSM_NLB
    lam = lax.complex(lam_re.astype(F32), lam_im.astype(F32))
    dt = jnp.exp(log_dt.astype(F32))[:, None]
    lam_dt = lam * dt
    lam_bar = jnp.exp(lam_dt)
    b_bar = ((lam_bar - 1.0) / lam)[..., None] * lax.complex(b_re.astype(F32), b_im.astype(F32))
    c = lax.complex(c_re.astype(F32), c_im.astype(F32))
    steps = jnp.arange(tc + 1, dtype=F32)
    pw = jnp.exp(lam_dt[None] * steps[:, None, None])
    hp = lax.Precision.HIGHEST
    lane_g = jnp.arange(LANES) // h_
    st_g = (jnp.arange(2 * gl * p_) % (gl * p_)) // p_
    rep_o = (jnp.arange(h_)[:, None] == (jnp.arange(LANES) % h_)[None, :]).astype(F32)
    st_col = (jnp.arange(2 * gl * p_) // (gl * p_)) * p_ + jnp.arange(2 * gl * p_) % p_
    rep_s = (jnp.arange(2 * p_)[:, None] == st_col[None, :]).astype(F32)
    taps = jnp.einsum('gop,tgp,gpi->gtoi', c, pw[:tc], b_bar).real
    a3 = taps.reshape(nlb, gl, tc, h_, h_).transpose(0, 2, 1, 4, 3).reshape(nlb, tc, LANES, h_)
    d = jnp.einsum('btro,oc->btrc', a3, rep_o, precision=hp)
    d = d * (lane_g[:, None] == lane_g[None, :]).astype(F32)
    dcat = d.transpose(0, 2, 1, 3).reshape(nlb, LANES, tc * LANES)
    m_intra = jnp.stack(
        [jnp.pad(dcat[:, :, :(tc - j) * LANES], ((0, 0), (0, 0), (j * LANES, 0))) for j in range(tc)],
        axis=1).reshape(nlb, tc * LANES, tc * LANES)
    sc = pw[:tc][::-1][:, :, :, None] * b_bar[None]
    sc = jnp.stack([sc.real, sc.imag], axis=0).reshape(2, tc, nlb, gl, p_, h_)
    a_s = sc.transpose(2, 1, 3, 5, 0, 4).reshape(nlb, tc, LANES, 2 * p_)
    m_state = jnp.einsum('bjrq,qc->bjrc', a_s, rep_s, precision=hp)
    m_state = (m_state * (lane_g[:, None] == st_g[None, :]).astype(F32)).reshape(nlb, tc * LANES, 2 * gl * p_)
    oc = c[None] * pw[1:tc + 1][:, :, None, :]
    oc = jnp.stack([oc.real, -oc.imag], axis=0).reshape(2, tc, nlb, gl, h_, p_)
    a_o = oc.transpose(2, 1, 0, 3, 5, 4).reshape(nlb, tc, 2 * gl * p_, h_)
    m_out = jnp.einsum('blro,oc->blrc', a_o, rep_o, precision=hp)
    m_out = m_out * (st_g[:, None] == lane_g[None, :]).astype(F32)
    m_out = m_out.transpose(0, 2, 1, 3).reshape(nlb, 2 * gl * p_, tc * LANES)
    a = pw[tc].reshape(nlb, 1, gl * p_)
    d_l = jnp.tile(d_skip.astype(F32).reshape(nlb, 1, LANES), (1, 1, tc))
    return (m_intra.astype(BF16), m_state.astype(BF16), m_out.astype(BF16),
            a.real.astype(F32), a.imag.astype(F32), d_l)


def _ssm_kernel(u_ref, mi_ref, ms_ref, mo_ref, are_ref, aim_ref, d_ref, y_ref, con_ref, sp_ref):
    nch = u_ref.shape[0] // SSM_TC
    ns = are_ref.shape[1]
    uf = jnp.concatenate([u_ref[pl.ds(j, nch, stride=SSM_TC), :] for j in range(SSM_TC)], axis=1)
    u = uf.astype(BF16)
    con_ref[...] = _dot(u, ms_ref[...])
    a_re = are_ref[...]
    a_im = aim_ref[...]

    def step(c, carry):
        s_re, s_im = carry
        sp_ref[pl.ds(c, 1), :ns] = s_re
        sp_ref[pl.ds(c, 1), ns:] = s_im
        c_re = con_ref[pl.ds(c, 1), :ns]
        c_im = con_ref[pl.ds(c, 1), ns:]
        return (a_re * s_re - a_im * s_im + c_re, a_re * s_im + a_im * s_re + c_im)

    zero = jnp.zeros((1, ns), F32)
    lax.fori_loop(0, nch, step, (zero, zero))
    y = _dot(u, mi_ref[...]) + _dot(sp_ref[...].astype(BF16), mo_ref[...]) + d_ref[...] * uf
    for l in range(SSM_TC):
        y_ref[pl.ds(l, nch, stride=SSM_TC), :] = y[:, l * LANES:(l + 1) * LANES]


def _ssm(u4, mats, bsz, seq):
    m_intra, m_state, m_out, a_re, a_im, d_l = mats
    t = u4.shape[1]
    nch = seq // SSM_TC
    wc = SSM_TC * LANES
    ns = SSM_GL * SSM_STATE
    wmap = lambda lb, b: (lb, 0, 0)
    return pl.pallas_call(
        _ssm_kernel,
        grid=(SSM_NLB, bsz),
        in_specs=[
            pl.BlockSpec((None, seq, LANES), lambda lb, b: (lb, b, 0)),
            pl.BlockSpec((None, wc, wc), wmap),
            pl.BlockSpec((None, wc, 2 * ns), wmap),
            pl.BlockSpec((None, 2 * ns, wc), wmap),
            pl.BlockSpec((None, 1, ns), wmap), pl.BlockSpec((None, 1, ns), wmap),
            pl.BlockSpec((None, 1, wc), wmap),
        ],
        out_specs=pl.BlockSpec((None, seq, LANES), lambda lb, b: (lb, b, 0)),
        out_shape=jax.ShapeDtypeStruct((SSM_NLB, t, LANES), F32),
        scratch_shapes=[pltpu.VMEM((nch, 2 * ns), F32), pltpu.VMEM((nch, 2 * ns), F32)],
        compiler_params=pltpu.CompilerParams(dimension_semantics=("arbitrary", "arbitrary"),
                                             vmem_limit_bytes=VMEM_LIMIT),
        name="ssm",
    )(u4, m_intra, m_state, m_out, a_re, a_im, d_l)


def _attn_kernel(q_ref, k_ref, v_ref, qi_ref, ki_ref, wi_ref, tri_ref, o_ref, keys_ref, kt_ref, k16_ref,
                 qh_ref, qm_ref, acc_ref, m_ref, *, tq, tk, seq, topk):
    i = pl.program_id(1)
    s0 = i * tq
    nk = (s0 + tq + tk - 1) // tk
    row = s0 + lax.broadcasted_iota(I32, (tq, 1), 0)
    lim = ((row >> CHUNK_SHIFT) + 1) << CHUNK_SHIFT
    lim_q = (((s0 + lax.broadcasted_iota(I32, (1, tq), 1)) >> CHUNK_SHIFT) + 1) << CHUNK_SHIFT
    kvec = jnp.minimum(lim_q, topk).astype(F32)
    col0 = lax.broadcasted_iota(I32, (tq, tk), 1)

    qi = qi_ref[...]
    lane_i = lax.broadcasted_iota(I32, qi.shape, 1)
    for h in range(IDX_HEADS):
        qh_ref[h] = jnp.where((lane_i >> 5) == h, qi, jnp.zeros_like(qi))
    wi = wi_ref[...]
    wcol = [wi[:, h:h + 1] for h in range(IDX_HEADS)]

    def score_tile(j, carry):
        kt = ki_ref[pl.ds(pl.multiple_of(j * tk, tk), tk), :]
        acc = jnp.zeros((tq, tk), F32)
        dots = _dot_nt(qh_ref[...].reshape(IDX_HEADS * tq, 256), kt)
        for h in range(IDX_HEADS):
            acc = acc + wcol[h] * jnp.maximum(dots[h * tq:(h + 1) * tq], 0.0)
        sc = jnp.where(col0 + j * tk < lim, acc, -jnp.inf)
        bits = pltpu.bitcast(sc, I32)
        key = bits ^ ((bits >> 31) & 0x7FFFFFFF)
        keys_ref[j] = key
        key_t = key.T
        kt_ref[j] = key_t
        k16_ref[j] = (key_t >> 16).astype(jnp.int16)
        return carry

    lax.fori_loop(0, nk, score_tile, 0)

    rg = 16
    one16 = jnp.ones((rg, tq), jnp.int16)
    zero16 = jnp.zeros((rg, tq), jnp.int16)

    def count16(cand):
        cand16 = jnp.broadcast_to(cand, (rg, tq)).astype(jnp.int16)

        def body(j, part):
            kk = k16_ref[j]
            for r in range(tk // rg):
                part = part + jnp.where(kk[r * rg:(r + 1) * rg, :] >= cand16, one16, zero16)
            return part
        part = lax.fori_loop(0, nk, body, jnp.zeros((rg, tq), jnp.int16))
        return jnp.sum(part.astype(F32), axis=0, keepdims=True)

    def bit_step(it, carry):
        u_ans, g_lo, g_hi = carry
        cand_u = u_ans | lax.shift_left(jnp.int32(1), 15 - it)
        cnt = count16(cand_u - 32768)
        ok = cnt >= kvec
        return jnp.where(ok, cand_u, u_ans), jnp.where(ok, cnt, g_lo), jnp.where(ok, g_hi, cnt)

    u_ans, g_lo, g_hi = lax.fori_loop(
        0, 16, bit_step, (jnp.zeros((1, tq), I32), lim_q.astype(F32), jnp.zeros((1, tq), F32)))
    b16 = u_ans - 32768

    def rebase(j, carry):
        kk = kt_ref[j]
        hi16 = kk >> 16
        low = (kk & 0xFFFF) - 32768
        rel = jnp.where(hi16 > b16, 32767, jnp.where(hi16 < b16, -32768, low))
        k16_ref[j] = rel.astype(jnp.int16)
        return carry

    lax.fori_loop(0, nk, rebase, 0)

    def unfinished(lo, hi, g_lo):
        return jnp.where(g_lo != kvec, jnp.where(hi - lo > 1, 1.0, 0.0), 0.0)

    def refine_cond(carry):
        return (carry[0] < 18) & (carry[1] > 0.0)

    def refine_pass(lo, hi, g_lo, g_hi):
        todo = unfinished(lo, hi, g_lo)
        mid = lo + ((hi - lo) >> 1)
        cand = jnp.where(b16 == 0, jnp.where(hi - lo == 65536, 1, mid), mid)
        cnt = count16(cand - 32768)
        up = jnp.where(cnt >= kvec, todo, 0.0) > 0.0
        dn = jnp.where(cnt < kvec, todo, 0.0) > 0.0
        return (jnp.where(up, cand, lo), jnp.where(dn, cand, hi), jnp.where(up, cnt, g_lo),
                jnp.where(dn, cnt, g_hi))

    def refine(carry):
        it, _, lo, hi, g_lo, g_hi = carry
        lo, hi, g_lo, g_hi = refine_pass(lo, hi, g_lo, g_hi)
        lo, hi, g_lo, g_hi = refine_pass(lo, hi, g_lo, g_hi)
        return it + 2, jnp.max(unfinished(lo, hi, g_lo)), lo, hi, g_lo, g_hi

    off_lo = jnp.zeros((1, tq), I32)
    off_hi = jnp.full((1, tq), 65536, I32)
    _, _, off_lo, _, g_lo, g_hi = lax.while_loop(
        refine_cond, refine,
        (jnp.int32(0), jnp.max(unfinished(off_lo, off_hi, g_lo)), off_lo, off_hi, g_lo, g_hi))
    has_tie = jnp.max(g_lo - kvec) > 0.0

    def to_rows(x):
        return jnp.transpose(jnp.broadcast_to(x, (LANES, tq)))

    vth = to_rows((b16 << 16) + off_lo)

    @pl.when(jnp.logical_not(has_tie))
    def _():
        def to_bias(j, carry):
            kk = keys_ref[j]
            for c in range(tk // LANES):
                sl = slice(c * LANES, (c + 1) * LANES)
                keys_ref[j, :, sl] = pltpu.bitcast(jnp.where(kk[:, sl] >= vth, 0.0, NEG_BIG).astype(F32), I32)
            return carry
        lax.fori_loop(0, nk, to_bias, 0)

    @pl.when(has_tie)
    def _():
        need = to_rows(kvec - g_hi)

        def to_bias(j, seen):
            kk = keys_ref[j]
            eq = jnp.concatenate(
                [jnp.where(kk[:, c * LANES:(c + 1) * LANES] == vth, 1.0, 0.0) for c in range(tk // LANES)],
                axis=1)
            rank = _dot(eq.astype(BF16), tri_ref[...])
            for c in range(tk // LANES):
                sl = slice(c * LANES, (c + 1) * LANES)
                take = jnp.where(seen + rank[:, sl] <= need, eq[:, sl], 0.0)
                sel = jnp.where(kk[:, sl] > vth, 1.0, take)
                keys_ref[j, :, sl] = pltpu.bitcast(jnp.where(sel > 0.0, 0.0, NEG_BIG).astype(F32), I32)
            return seen + jnp.broadcast_to(rank[:, tk - 1:tk], (tq, LANES))
        lax.fori_loop(0, nk, to_bias, jnp.zeros((tq, LANES), F32))

    lane_q = lax.broadcasted_iota(I32, (tq, LANES), 1)
    lane_k = lax.broadcasted_iota(I32, (tk, LANES), 1)
    for h in range(N_HEADS):
        qp = q_ref[:, (h // 2) * LANES:(h // 2 + 1) * LANES]
        keep = (lane_q >= HEAD_DIM) if h % 2 else (lane_q < HEAD_DIM)
        qm_ref[h] = jnp.where(keep, qp, jnp.zeros_like(qp))
    acc_ref[...] = jnp.zeros_like(acc_ref)
    m_ref[...] = jnp.full(m_ref.shape, NEG_BIG, F32)
    ones_v = jnp.ones((tk, LANES), BF16)

    def attn_tile(j, carry):
        rows = pl.ds(pl.multiple_of(j * tk, tk), tk)
        bias = pltpu.bitcast(keys_ref[j], F32)
        for h in range(N_HEADS):
            cs = slice((h // 2) * LANES, (h // 2 + 1) * LANES)
            own = (lane_k >= HEAD_DIM) if h % 2 else (lane_k < HEAD_DIM)
            if h % 2 == 0:
                s_pair = _dot_nt(qm_ref[h:h + 2].reshape(2 * tq, LANES), k_ref[rows, cs])
            s = s_pair[(h % 2) * tq:(h % 2 + 1) * tq] + bias
            m_old = m_ref[h]
            m_new = jnp.maximum(m_old, jnp.max(s, axis=1, keepdims=True))
            p = jnp.concatenate(
                [jnp.exp2(s[:, c * LANES:(c + 1) * LANES] - m_new) for c in range(tk // LANES)],
                axis=1).astype(BF16)
            vh = jnp.where(own, v_ref[rows, cs], ones_v)
            acc_ref[h] = jnp.exp2(m_old - m_new) * acc_ref[h] + _dot(p, vh)
            m_ref[h] = m_new
        return carry

    lax.fori_loop(0, nk, attn_tile, 0)

    for hp in range(N_HEADS // 2):
        a0 = acc_ref[2 * hp]
        a1 = acc_ref[2 * hp + 1]
        out = jnp.where(lane_q < HEAD_DIM, a0 / pltpu.roll(a0, HEAD_DIM, 1), a1 / pltpu.roll(a1, HEAD_DIM, 1))
        o_ref[:, hp * LANES:(hp + 1) * LANES] = out.astype(o_ref.dtype)


def _attn(q, k, v, qi, ki8, wi, bsz, seq, tq, tk):
    t = q.shape[0]
    nq = seq // tq
    qrow = lambda b, i: (b * nq + i, 0)
    kv = lambda b, i: (b, 0)
    kern = functools.partial(_attn_kernel, tq=tq, tk=tk, seq=seq, topk=min(TOPK_MAX, seq // 4))
    tri = (jnp.arange(tk)[:, None] <= jnp.arange(tk)[None, :]).astype(BF16)
    return pl.pallas_call(
        kern,
        grid=(bsz, nq),
        in_specs=[
            pl.BlockSpec((tq, ATTN_WIDTH), qrow),
            pl.BlockSpec((seq, ATTN_WIDTH), kv, pipeline_mode=pl.Buffered(1)),
            pl.BlockSpec((seq, ATTN_WIDTH), kv, pipeline_mode=pl.Buffered(1)),
            pl.BlockSpec((tq, 256), qrow), pl.BlockSpec((seq, 256), kv, pipeline_mode=pl.Buffered(1)),
            pl.BlockSpec((tq, LANES), qrow),
            pl.BlockSpec((tk, tk), lambda b, i: (0, 0), pipeline_mode=pl.Buffered(1)),
        ],
        out_specs=pl.BlockSpec((tq, ATTN_WIDTH), qrow),
        out_shape=jax.ShapeDtypeStruct((t, ATTN_WIDTH), BF16),
        scratch_shapes=[
            pltpu.VMEM((seq // tk, tq, tk), I32),
            pltpu.VMEM((seq // tk, tk, tq), I32),
            pltpu.VMEM((seq // tk, tk, tq), jnp.int16),
            pltpu.VMEM((IDX_HEADS, tq, 256), BF16),
            pltpu.VMEM((N_HEADS, tq, LANES), BF16),
            pltpu.VMEM((N_HEADS, tq, LANES), F32),
            pltpu.VMEM((N_HEADS, tq, LANES), F32),
        ],
        compiler_params=pltpu.CompilerParams(dimension_semantics=("arbitrary", "arbitrary"),
                                             vmem_limit_bytes=VMEM_LIMIT),
        name="attn",
    )(q, k, v, qi, ki8, wi, tri)


def _mix_kernel(y4_ref, ya_ref, gs_ref, ga_ref, x_ref, wglu_ref, bglu_ref, wsb_ref, wab_ref, wo_ref,
                nf_ref, wr_ref, br_ref, tri_ref, utri_ref, r1_ref, h2_ref, route_ref, lpos_ref, lpos_t_ref, cnt_ref):
    ys = jnp.concatenate([y4_ref[lb] for lb in range(SSM_NLB)], axis=1)
    ys = 0.5 * ys * (1.0 + jnp.tanh(math.sqrt(2.0 / math.pi) * (ys + 0.044715 * (ys * ys * ys))))
    ys = ys * jax.nn.sigmoid(_dot(ys.astype(BF16), wglu_ref[...]) + bglu_ref[...])
    br_s = _dot(ys.astype(BF16), wsb_ref[...])
    br_a = _dot(ya_ref[...], wab_ref[...])
    mix = gs_ref[...].astype(F32) * br_s + ga_ref[...].astype(F32) * br_a
    r1 = x_ref[...] + _dot(mix.astype(BF16), wo_ref[...])
    r1_ref[...] = r1
    h2 = _rms(r1, nf_ref[...])
    h2_ref[...] = h2.astype(BF16)
    hi = h2.astype(BF16)
    lo = (h2 - hi.astype(F32)).astype(BF16)
    hw = _dot(hi, wr_ref[...])
    logits = hw[:, :LANES] + (hw[:, LANES:] + _dot(lo, wr_ref[:, :LANES])) + br_ref[...]
    tm = logits.shape[0]
    lane = lax.broadcasted_iota(I32, (tm, LANES), 1).astype(F32)
    vals = logits
    tops, idxs = [], []
    for _ in range(TOP_K):
        m = jnp.max(vals, axis=1, keepdims=True)
        idx = jnp.min(jnp.where(vals == m, lane, float(LANES)), axis=1, keepdims=True)
        tops.append(m)
        idxs.append(idx)
        vals = jnp.where(lane == idx, -jnp.inf, vals)
    es = [jnp.exp(tv - tops[0]) for tv in tops]
    den = es[0] + es[1] + es[2] + es[3]
    route = jnp.zeros((tm, LANES), F32)
    for kk in range(TOP_K):
        w = es[kk] / den
        w_hi = w.astype(BF16).astype(F32)
        w_mid = (w - w_hi).astype(BF16).astype(F32)
        w_lo = (w - w_hi) - w_mid
        route = jnp.where(lane == float(kk), idxs[kk], route)
        route = jnp.where(lane == float(TOP_K + kk), w_hi, route)
        route = jnp.where(lane == float(2 * TOP_K + kk), w_mid, route)
        route = jnp.where(lane == float(3 * TOP_K + kk), w_lo, route)
    route_ref[...] = route.astype(BF16)
    mask = jnp.zeros((tm, LANES), F32)
    for kk in range(TOP_K):
        mask = jnp.where(lane == idxs[kk], 1.0, mask)
    incl = _dot(tri_ref[...], mask.astype(BF16))
    cnt = incl[tm - 1:tm, :]
    cnt_u = jnp.floor((cnt + (MOE_SEG - 1)) * (1.0 / MOE_SEG))
    loc = _dot(jnp.broadcast_to(cnt_u, (8, LANES)).astype(BF16), utri_ref[...])[:1, :] * float(MOE_SEG)
    where_to = loc + incl - mask
    lpos = jnp.full((tm, LANES), -1.0, F32)
    for kk in range(TOP_K):
        pos = jnp.sum(jnp.where(lane == idxs[kk], where_to, 0.0), axis=1, keepdims=True)
        lpos = jnp.where(lane == float(kk), pos, lpos)
    lpos_ref[...] = lpos.astype(I32)
    lpos_t_ref[...] = jnp.transpose(lpos)[:8, :].astype(I32)
    cnt_ref[...] = jnp.broadcast_to(cnt, (8, LANES)).astype(I32)


def _mix(y4, y_attn, gs, ga, x2, w_glu, b_glu, w_ssm_br, w_attn_br, w_o, norm_ffn, w_router, b_router, tm):
    t = x2.shape[0]
    wr = jnp.pad(w_router.astype(F32), ((0, 0), (0, LANES - N_EXPERTS)))
    wr_hi = wr.astype(BF16)
    wr_split = jnp.concatenate([wr_hi, (wr - wr_hi.astype(F32)).astype(BF16)], axis=1)
    br = jnp.concatenate([b_router.astype(F32), jnp.full((LANES - N_EXPERTS,), NEG_BIG, F32)]).reshape(1, LANES)
    assert tm == MOE_TT
    nt = t // tm
    tri = (jnp.arange(tm)[:, None] >= jnp.arange(tm)[None, :]).astype(BF16)
    utri = (jnp.arange(LANES)[:, None] < jnp.arange(LANES)[None, :]).astype(BF16)
    row = lambda i: (i, 0)
    const = lambda i: (0, 0)
    return pl.pallas_call(
        _mix_kernel,
        grid=(t // tm,),
        in_specs=[
            pl.BlockSpec((SSM_NLB, tm, LANES), lambda i: (0, i, 0)),
            pl.BlockSpec((tm, ATTN_WIDTH), row),
            pl.BlockSpec((tm, D_MODEL), row), pl.BlockSpec((tm, D_MODEL), row), pl.BlockSpec((tm, D_MODEL), row),
            pl.BlockSpec((SSM_WIDTH, SSM_WIDTH), const), pl.BlockSpec((1, SSM_WIDTH), const),
            pl.BlockSpec((SSM_WIDTH, D_MODEL), const), pl.BlockSpec((ATTN_WIDTH, D_MODEL), const),
            pl.BlockSpec((D_MODEL, D_MODEL), const), pl.BlockSpec((1, D_MODEL), const),
            pl.BlockSpec((D_MODEL, 2 * LANES), const),
            pl.BlockSpec((1, LANES), const),
            pl.BlockSpec((tm, tm), const), pl.BlockSpec((LANES, LANES), const),
        ],
        out_specs=[pl.BlockSpec((tm, D_MODEL), row), pl.BlockSpec((tm, D_MODEL), row),
                   pl.BlockSpec((tm, LANES), row), pl.BlockSpec((tm, LANES), row),
                   pl.BlockSpec((None, 8, tm), lambda i: (i, 0, 0)), pl.BlockSpec((None, 8, LANES), lambda i: (i, 0, 0))],
        out_shape=[jax.ShapeDtypeStruct((t, D_MODEL), F32), jax.ShapeDtypeStruct((t, D_MODEL), BF16),
                   jax.ShapeDtypeStruct((t, LANES), BF16), jax.ShapeDtypeStruct((t, LANES), I32),
                   jax.ShapeDtypeStruct((nt, 8, tm), I32), jax.ShapeDtypeStruct((nt, 8, LANES), I32)],
        compiler_params=pltpu.CompilerParams(dimension_semantics=("arbitrary",),
                                             vmem_limit_bytes=VMEM_LIMIT),
        name="mix",
    )(y4, y_attn, gs, ga, x2, w_glu.astype(BF16), b_glu.reshape(1, SSM_WIDTH).astype(F32),
      w_ssm_br.astype(BF16), w_attn_br.astype(BF16), w_o.astype(BF16), norm_ffn.reshape(1, D_MODEL),
      wr_split, br, tri, utri)


def _segment_copies(seg_ref, e, make_copy, fn):
    c = seg_ref[0, e]
    s = seg_ref[0, N_EXPERTS + e]
    d = seg_ref[0, 2 * N_EXPERTS + e]
    for size in _SEG_SIZES:
        units = size // MOE_SEG
        hit = (c & units) != 0

        @pl.when(hit)
        def _():
            fn(make_copy(pl.multiple_of(s * MOE_SEG, MOE_SEG), pl.multiple_of(d * MOE_SEG, MOE_SEG), size))
        s = s + jnp.where(hit, units, 0)
        d = d + jnp.where(hit, units, 0)


def _dispatch_kernel(zt_ref, seg_ref, seg_prev_ref, lpos_ref, h2_ref, wtok_ref, xs_ref, buf_ref, zbuf_ref,
                     sem, zsem):
    tmx = zbuf_ref.shape[0]
    i = pl.program_id(0)
    slot = i % 2

    @pl.when(i == 0)
    def _():
        zbuf_ref[...] = jnp.zeros_like(zbuf_ref)

        def zero_copy(z):
            return pltpu.make_async_copy(
                zbuf_ref, xs_ref.at[pl.ds(pl.multiple_of(zt_ref[z] * tmx, tmx), tmx)], zsem)

        for z in range(zt_ref.shape[0]):
            @pl.when(zt_ref[z] >= 0)
            def _():
                zero_copy(z).start()
        for z in range(zt_ref.shape[0]):
            @pl.when(zt_ref[z] >= 0)
            def _():
                zero_copy(z).wait()

    rowi = lax.broadcasted_iota(I32, (MOE_NL, MOE_TT), 0)
    sel = jnp.zeros((MOE_NL, MOE_TT), F32)
    for kk in range(TOP_K):
        sel = jnp.where(rowi == lpos_ref[kk:kk + 1, :], 1.0, sel)
    src = jnp.concatenate([h2_ref[...], wtok_ref[...]], axis=1)
    buf_ref[slot] = _dot(sel.astype(BF16), src)

    def copies(table_ref, which, fn):
        def make_copy(s, d, size):
            return pltpu.make_async_copy(buf_ref.at[which, pl.ds(s, size)], xs_ref.at[pl.ds(d, size)],
                                         sem.at[which])

        def body(e, carry):
            _segment_copies(table_ref, e, make_copy, fn)
            return carry
        lax.fori_loop(0, N_EXPERTS, body, 0)

    @pl.when(i > 0)
    def _():
        copies(seg_prev_ref, 1 - slot, lambda cp: cp.wait())

    copies(seg_ref, slot, lambda cp: cp.start())

    @pl.when(i == pl.num_programs(0) - 1)
    def _():
        copies(seg_ref, slot, lambda cp: cp.wait())


def _dispatch(zero_tiles, seg, lpos_t, h2, wtok, n_rows, tmx):
    t = h2.shape[0]
    grid_spec = pltpu.PrefetchScalarGridSpec(
        num_scalar_prefetch=1,
        grid=(t // MOE_TT,),
        in_specs=[
            pl.BlockSpec((None, 1, 3 * N_EXPERTS), lambda i, zt: (i, 0, 0), memory_space=pltpu.SMEM),
            pl.BlockSpec((None, 1, 3 * N_EXPERTS), lambda i, zt: (jnp.maximum(i - 1, 0), 0, 0),
                         memory_space=pltpu.SMEM),
            pl.BlockSpec((None, 8, MOE_TT), lambda i, zt: (i, 0, 0)),
            pl.BlockSpec((MOE_TT, D_MODEL), lambda i, zt: (i, 0)),
            pl.BlockSpec((MOE_TT, LANES), lambda i, zt: (i, 0)),
        ],
        out_specs=pl.BlockSpec(memory_space=pl.ANY),
        scratch_shapes=[pltpu.VMEM((2, MOE_NL, MOE_W), F32), pltpu.VMEM((tmx, MOE_W), F32),
                        pltpu.SemaphoreType.DMA((2,)), pltpu.SemaphoreType.DMA(())],
    )
    return pl.pallas_call(
        _dispatch_kernel,
        grid_spec=grid_spec,
        out_shape=jax.ShapeDtypeStruct((n_rows, MOE_W), F32),
        compiler_params=pltpu.CompilerParams(dimension_semantics=("arbitrary",), has_side_effects=True,
                                             vmem_limit_bytes=VMEM_LIMIT),
        name="dispatch",
    )(zero_tiles, seg, seg, lpos_t, h2, wtok)


def _expert_kernel(te_ref, tv_ref, x_ref, wg_ref, bg_ref, wu_ref, bu_ref, wd_ref, bd_ref, y_ref,
                   wgb_ref, wub_ref, wdb_ref):
    i = pl.program_id(0)
    e = te_ref[i]
    prev = te_ref[jnp.maximum(i - 1, 0)]

    @pl.when((i == 0) | (e != prev))
    def _():
        wgb_ref[...] = wg_ref[...].astype(BF16)
        wub_ref[...] = wu_ref[...].astype(BF16)
        wdb_ref[...] = wd_ref[...].astype(BF16)

    @pl.when(tv_ref[i] != 0)
    def _():
        x = x_ref[:, :D_MODEL].astype(BF16)
        rt = x_ref[:, D_MODEL:]
        ef = e.astype(F32)
        w_row = jnp.zeros((rt.shape[0], 1), F32)
        for kk in range(TOP_K):
            wk = (rt[:, TOP_K + kk:TOP_K + kk + 1] + rt[:, 2 * TOP_K + kk:2 * TOP_K + kk + 1]
                  + rt[:, 3 * TOP_K + kk:3 * TOP_K + kk + 1])
            w_row = w_row + jnp.where(rt[:, kk:kk + 1] == ef, wk, 0.0)
        g = jnp.minimum(_dot(x, wgb_ref[...]) + bg_ref[...], SWIGLU_LIMIT)
        u = jnp.clip(_dot(x, wub_ref[...]) + bu_ref[...], -SWIGLU_LIMIT, SWIGLU_LIMIT)
        a = (u + 1.0) * (g * jax.nn.sigmoid(SWIGLU_ALPHA * g))
        y_ref[...] = (_dot(a.astype(BF16), wdb_ref[...]) + bd_ref[...]) * w_row

    @pl.when(tv_ref[i] == 0)
    def _():
        y_ref[...] = jnp.zeros_like(y_ref)


def _experts(tile_expert, tile_valid, xs, w_gate, b_gate, w_up, b_up, w_down, b_down, tmx):
    n_rows = xs.shape[0]
    wmap = lambda i, te, tv: (te[i], 0, 0)
    row = lambda i, te, tv: (i, 0)
    d_ff = w_gate.shape[2]
    grid_spec = pltpu.PrefetchScalarGridSpec(
        num_scalar_prefetch=2,
        grid=(n_rows // tmx,),
        in_specs=[
            pl.BlockSpec((tmx, MOE_W), row),
            pl.BlockSpec((None, D_MODEL, d_ff), wmap), pl.BlockSpec((None, 1, d_ff), wmap),
            pl.BlockSpec((None, D_MODEL, d_ff), wmap), pl.BlockSpec((None, 1, d_ff), wmap),
            pl.BlockSpec((None, d_ff, D_MODEL), wmap), pl.BlockSpec((None, 1, D_MODEL), wmap),
        ],
        out_specs=pl.BlockSpec((tmx, D_MODEL), row),
        scratch_shapes=[pltpu.VMEM((D_MODEL, d_ff), BF16), pltpu.VMEM((D_MODEL, d_ff), BF16),
                        pltpu.VMEM((d_ff, D_MODEL), BF16)],
    )
    return pl.pallas_call(
        _expert_kernel,
        grid_spec=grid_spec,
        out_shape=jax.ShapeDtypeStruct((n_rows, D_MODEL), F32),
        compiler_params=pltpu.CompilerParams(dimension_semantics=("arbitrary",),
                                             vmem_limit_bytes=VMEM_LIMIT),
        name="experts",
    )(tile_expert, tile_valid, xs, w_gate, b_gate.reshape(N_EXPERTS, 1, d_ff), w_up,
      b_up.reshape(N_EXPERTS, 1, d_ff), w_down, b_down.reshape(N_EXPERTS, 1, D_MODEL))


def _combine_kernel(seg_ref, seg_next_ref, lpos_ref, r1_ref, p_ref, ys_ref, wpg_ref, wpp_ref, nfin_ref, o_ref,
                    buf_ref, sem):
    i = pl.program_id(0)
    slot = i % 2

    def copies(table_ref, which, fn):
        def make_copy(s, d, size):
            return pltpu.make_async_copy(ys_ref.at[pl.ds(d, size)], buf_ref.at[which, pl.ds(s, size)],
                                         sem.at[which])

        def body(e, carry):
            _segment_copies(table_ref, e, make_copy, fn)
            return carry
        lax.fori_loop(0, N_EXPERTS, body, 0)

    @pl.when(i == 0)
    def _():
        buf_ref[...] = jnp.zeros_like(buf_ref)
        copies(seg_ref, slot, lambda cp: cp.start())

    @pl.when(i + 1 < pl.num_programs(0))
    def _():
        copies(seg_next_ref, 1 - slot, lambda cp: cp.start())

    coli = lax.broadcasted_iota(I32, (MOE_TT, MOE_NL), 1)
    lpos = lpos_ref[...]
    pick = jnp.zeros((MOE_TT, MOE_NL), F32)
    for kk in range(TOP_K):
        pick = jnp.where(coli == lpos[:, kk:kk + 1], 1.0, pick)
    copies(seg_ref, slot, lambda cp: cp.wait())
    moe = _dot(pick.astype(BF16), buf_ref[slot].astype(BF16))
    r2 = r1_ref[...] + moe
    gate = jax.nn.sigmoid(_dot(r2.astype(BF16), wpg_ref[...]))
    r3 = r2 + gate * _dot(p_ref[...].astype(BF16), wpp_ref[...])
    o_ref[...] = _rms(r3, nfin_ref[...])


def _combine(seg, lpos, r1, p2, ys, w_ple_gate, w_ple_proj, norm_final):
    t = r1.shape[0]
    nt = t // MOE_TT
    row = lambda i: (i, 0)
    const = lambda i: (0, 0)
    return pl.pallas_call(
        _combine_kernel,
        grid=(nt,),
        in_specs=[
            pl.BlockSpec((None, 1, 3 * N_EXPERTS), lambda i: (i, 0, 0), memory_space=pltpu.SMEM),
            pl.BlockSpec((None, 1, 3 * N_EXPERTS), lambda i: (jnp.minimum(i + 1, nt - 1), 0, 0),
                         memory_space=pltpu.SMEM),
            pl.BlockSpec((MOE_TT, LANES), row),
            pl.BlockSpec((MOE_TT, D_MODEL), row),
            pl.BlockSpec((MOE_TT, PLE_DIM), row),
            pl.BlockSpec(memory_space=pl.ANY),
            pl.BlockSpec((D_MODEL, D_MODEL), const), pl.BlockSpec((PLE_DIM, D_MODEL), const),
            pl.BlockSpec((1, D_MODEL), const),
        ],
        out_specs=pl.BlockSpec((MOE_TT, D_MODEL), row),
        out_shape=jax.ShapeDtypeStruct((t, D_MODEL), F32),
        scratch_shapes=[pltpu.VMEM((2, MOE_NL, D_MODEL), F32), pltpu.SemaphoreType.DMA((2,))],
        compiler_params=pltpu.CompilerParams(dimension_semantics=("arbitrary",),
                                             vmem_limit_bytes=VMEM_LIMIT),
        name="combine",
    )(seg, seg, lpos, r1, p2, ys, w_ple_gate.astype(BF16), w_ple_proj.astype(BF16),
      norm_final.reshape(1, D_MODEL))


def _moe(h2, route, lpos_l, lpos_t, cnt8, r1, p2, w_gate, b_gate, w_up, b_up, w_down, b_down, w_ple_gate,
         w_ple_proj, norm_final, tmx):
    t = h2.shape[0]
    nt = t // MOE_TT
    tile_cnt = cnt8[:, 0, :N_EXPERTS]
    cnt_u = (tile_cnt + MOE_SEG - 1) // MOE_SEG
    loc_u = jnp.cumsum(cnt_u, axis=1) - cnt_u
    reg_u = jnp.sum(cnt_u, axis=0)
    upt = tmx // MOE_SEG
    pad_u = ((reg_u + upt - 1) // upt) * upt
    end_u = jnp.cumsum(pad_u)
    glob_u = (end_u - pad_u)[None, :] + jnp.cumsum(cnt_u, axis=0) - cnt_u
    seg = jnp.concatenate([cnt_u, loc_u, glob_u], axis=1).astype(I32).reshape(nt, 1, 3 * N_EXPERTS)
    n_tiles = (t * TOP_K + nt * N_EXPERTS * (MOE_SEG - 1)) // tmx + N_EXPERTS
    ends = end_u * MOE_SEG
    tile_start = jnp.arange(n_tiles, dtype=I32) * tmx
    tile_valid = (tile_start < ends[-1]).astype(I32)
    tile_expert = jnp.minimum(jnp.sum((tile_start[:, None] >= ends[None, :]).astype(I32), axis=1), N_EXPERTS - 1)
    last_e = jnp.max(jnp.where(tile_valid != 0, tile_expert, 0))
    tile_expert = jnp.where(tile_valid != 0, tile_expert, last_e)
    n_spare = n_tiles - (t * TOP_K) // tmx
    used = ends[-1] // tmx
    last_tile = jnp.where(pad_u > 0, ends // tmx - 1, -1)
    spare = used + jnp.arange(n_spare, dtype=I32)
    zero_tiles = jnp.concatenate([last_tile, jnp.where(spare < n_tiles, spare, -1)]).astype(I32)
    xs = _dispatch(zero_tiles, seg, lpos_t, h2, route, n_tiles * tmx, tmx)
    ys = _experts(tile_expert, tile_valid, xs, w_gate, b_gate, w_up, b_up, w_down, b_down, tmx)
    return _combine(seg, lpos_l, r1, p2, ys, w_ple_gate, w_ple_proj, norm_final)


def _tile_plan(seq):
    return {
        "inproj_rows": 256,
        "attn_q": min(512, seq),
        "attn_k": min(1024, seq),
        "mix_rows": MOE_TT,
        "expert_rows": 512,
    }


def kernel(x, p, w_in, b_gates, lam_re, lam_im, log_dt, b_re, b_im, c_re, c_im, d_skip, w_glu, b_glu,
           w_ssm_br, w_attn_br, w_o, norm_mix, norm_ffn, w_router, b_router, w_gate, b_gate, w_up, b_up,
           w_down, b_down, w_ple_gate, w_ple_proj, norm_final):
    bsz, seq, _ = x.shape
    t = bsz * seq
    tiles = _tile_plan(seq)
    x2 = x.reshape(t, D_MODEL)
    u4, q, k, v, qi, ki8, wi, gs, ga = _inproj(x2, norm_mix[0], w_in[0], b_gates[0], seq, tm=tiles["inproj_rows"])
    mats = _ssm_mats(lam_re[0], lam_im[0], log_dt[0], b_re[0], b_im[0], c_re[0], c_im[0], d_skip[0])
    y4 = _ssm(u4, mats, bsz, seq)
    y_attn = _attn(q, k, v, qi, ki8, wi, bsz, seq, tq=tiles["attn_q"], tk=tiles["attn_k"])
    r1, h2, route, lpos_l, lpos_t, cnt8 = _mix(y4, y_attn, gs, ga, x2, w_glu[0], b_glu[0], w_ssm_br[0], w_attn_br[0], w_o[0],
                         norm_ffn[0], w_router[0], b_router[0], tm=tiles["mix_rows"])
    out = _moe(h2, route, lpos_l, lpos_t, cnt8, r1, p[0].reshape(t, PLE_DIM), w_gate[0], b_gate[0], w_up[0], b_up[0], w_down[0],
               b_down[0], w_ple_gate[0], w_ple_proj[0], norm_final, tmx=tiles["expert_rows"])
    return out.reshape(bsz, seq, D_MODEL)
```
